```python
import jax
import jax.numpy as jnp
from jax import lax
import numpy as np

D_MODEL = 1024
BATCH = 4
SEQ = 4096
DEPTH = 4

D_MIX = D_MODEL
ML_WIDTH = D_MIX // 2
ML_HEADS = 4
ML_HEAD_DIM = ML_WIDTH // ML_HEADS
RG_WIDTH = D_MIX - ML_WIDTH
RG_BLOCKS = 8
RG_BLOCK_DIM = RG_WIDTH // RG_BLOCKS
RG_C = 8.0
CONV_WIDTH = 4
CHUNK = 128
D_IN = 4 * ML_WIDTH + 2 * ML_HEADS + 2 * RG_WIDTH
D_FF_DENSE = ((8 * D_MODEL // 3 + 127) // 128) * 128
N_EXPERTS = 8
TOP_K = 2
D_FF_EXPERT = 7 * D_MODEL // 2
N_DENSE = (DEPTH + 1) // 2
N_MOE = DEPTH // 2
EPS = 1e-6

kernel_name = 'hybrid_mlstm_rglru_moe'


def _rms(x):
    x = x.astype(jnp.float32)
    return x * lax.rsqrt(jnp.mean(x * x, axis=-1, keepdims=True) + EPS)


def rmsnorm(x, g):
    return (_rms(x) * g.astype(jnp.float32)).astype(x.dtype)


def mlstm_chunkwise(q, k, v, i_pre, f_pre):
    B, S, H, Dh = q.shape
    L = CHUNK
    NC = S // L

    def chunks(t):
        t = t.astype(jnp.float32).reshape((B, NC, L, H) + t.shape[3:])
        return jnp.moveaxis(t, 3, 1)

    q = chunks(q)
    k = chunks(k) * (Dh ** -0.5)
    v = chunks(v)
    log_i = chunks(i_pre)
    log_f = jax.nn.log_sigmoid(chunks(f_pre))
    b = jnp.cumsum(log_f, axis=-1)
    g = b[..., -1]

    a = g[..., None] - b + log_i
    m_loc = jnp.max(a, axis=-1)
    w = jnp.exp(a - m_loc[..., None])
    C_loc = jnp.einsum('bhcl,bhcld,bhcle->bhcde', w, k, v)
    n_loc = jnp.einsum('bhcl,bhcld->bhcd', w, k)

    def step(carry, xs):
        C, n, m = carry
        g_c, m_l, C_l, n_l = xs
        m_new = jnp.maximum(g_c + m, m_l)
        s_old = jnp.exp(g_c + m - m_new)
        s_loc = jnp.exp(m_l - m_new)
        C_new = s_old[..., None, None] * C + s_loc[..., None, None] * C_l
        n_new = s_old[..., None] * n + s_loc[..., None] * n_l
        return (C_new, n_new, m_new), (C, n, m)

    init = (jnp.zeros((B, H, Dh, Dh), jnp.float32),
            jnp.zeros((B, H, Dh), jnp.float32),
            jnp.zeros((B, H), jnp.float32))
    xs = (jnp.moveaxis(g, 2, 0), jnp.moveaxis(m_loc, 2, 0),
          jnp.moveaxis(C_loc, 2, 0), jnp.moveaxis(n_loc, 2, 0))
    _, (C_prev, n_prev, m_prev) = lax.scan(step, init, xs)
    C_prev = jnp.moveaxis(C_prev, 0, 2)
    n_prev = jnp.moveaxis(n_prev, 0, 2)
    m_prev = jnp.moveaxis(m_prev, 0, 2)

    causal = jnp.tril(jnp.ones((L, L), dtype=bool))
    d_log = b[..., :, None] - b[..., None, :] + log_i[..., None, :]
    d_log = jnp.where(causal, d_log, -jnp.inf)
    m_inter = b + m_prev[..., None]
    m_comb = jnp.maximum(m_inter, jnp.max(d_log, axis=-1))
    s_w = jnp.exp(d_log - m_comb[..., None]) * jnp.einsum('bhcld,bhcsd->bhcls', q, k)
    inter_scale = jnp.exp(m_inter - m_comb)
    num = (jnp.einsum('bhcls,bhcse->bhcle', s_w, v)
           + inter_scale[..., None] * jnp.einsum('bhcld,bhcde->bhcle', q, C_prev))
    den = jnp.sum(s_w, axis=-1) + inter_scale * jnp.einsum('bhcld,bhcd->bhcl', q, n_prev)
    h = num / jnp.maximum(jnp.abs(den), jnp.exp(-m_comb))[..., None]
    return jnp.moveaxis(h, 1, 3).reshape(B, S, H, Dh)


def causal_depthwise_conv(x, w, bias):
    K, C = w.shape
    y = lax.conv_general_dilated(x, w[:, None, :], window_strides=(1,),
                                 padding=[(K - 1, 0)],
                                 dimension_numbers=('NWC', 'WIO', 'NWC'),
                                 feature_group_count=C)
    return y + bias


def rg_lru(x, w_a, b_a, w_x, b_x, lam):
    B, S, _ = x.shape
    xf = x.astype(jnp.float32)
    xb = xf.reshape(B, S, RG_BLOCKS, RG_BLOCK_DIM)
    r = jax.nn.sigmoid(jnp.einsum('bsgi,gij->bsgj', xb, w_a.astype(jnp.float32)).reshape(B, S, RG_WIDTH)
                       + b_a.astype(jnp.float32))
    i = jax.nn.sigmoid(jnp.einsum('bsgi,gij->bsgj', xb, w_x.astype(jnp.float32)).reshape(B, S, RG_WIDTH)
                       + b_x.astype(jnp.float32))
    log_a = -RG_C * r * jax.nn.softplus(-lam.astype(jnp.float32))
    a = jnp.exp(log_a)
    u = jnp.sqrt(-jnp.expm1(2.0 * log_a)) * (i * xf)

    def combine(left, right):
        a1, b1 = left
        a2, b2 = right
        return a1 * a2, a2 * b1 + b2

    _, h = lax.associative_scan(combine, (a, u), axis=1)
    return h


def hybrid_mixer(h, w_in, ml_b_if, ml_norm_g, rg_conv_w, rg_conv_b, rg_w_a, rg_b_a,
                 rg_w_x, rg_b_x, rg_lam, rg_norm_g, w_out):
    B, S, _ = h.shape
    z = h @ w_in
    o0 = 0
    q = z[..., o0:o0 + ML_WIDTH]; o0 += ML_WIDTH
    k = z[..., o0:o0 + ML_WIDTH]; o0 += ML_WIDTH
    v = z[..., o0:o0 + ML_WIDTH]; o0 += ML_WIDTH
    o = z[..., o0:o0 + ML_WIDTH]; o0 += ML_WIDTH
    if_pre = z[..., o0:o0 + 2 * ML_HEADS].astype(jnp.float32) + ml_b_if.astype(jnp.float32); o0 += 2 * ML_HEADS
    rx = z[..., o0:o0 + RG_WIDTH]; o0 += RG_WIDTH
    rgate = z[..., o0:o0 + RG_WIDTH]

    shp = (B, S, ML_HEADS, ML_HEAD_DIM)
    ht = mlstm_chunkwise(q.reshape(shp), k.reshape(shp), v.reshape(shp),
                         if_pre[..., :ML_HEADS], if_pre[..., ML_HEADS:])
    ht = _rms(ht) * ml_norm_g.astype(jnp.float32).reshape(ML_HEADS, ML_HEAD_DIM)
    y_ml = jax.nn.sigmoid(o.astype(jnp.float32)) * ht.reshape(B, S, ML_WIDTH)

    xc = causal_depthwise_conv(rx, rg_conv_w, rg_conv_b)
    hr = rg_lru(xc, rg_w_a, rg_b_a, rg_w_x, rg_b_x, rg_lam)
    y_rg = hr * jax.nn.gelu(rgate.astype(jnp.float32), approximate=True)
    y_rg = _rms(y_rg) * rg_norm_g.astype(jnp.float32)

    y = jnp.concatenate([y_ml, y_rg], axis=-1).astype(h.dtype)
    return y @ w_out


def swiglu(x, w_gate, w_up, w_down):
    return (jax.nn.silu(x @ w_gate) * (x @ w_up)) @ w_down


def moe_swiglu(x, w_router, w_gate, w_up, w_down):
    B, S, D = x.shape
    t = x.reshape(B * S, D)
    logits = (t @ w_router).astype(jnp.float32)
    top_val, top_idx = lax.top_k(logits, TOP_K)
    top_w = jax.nn.softmax(top_val, axis=-1)
    gates = jnp.sum(jax.nn.one_hot(top_idx, N_EXPERTS, dtype=jnp.float32) * top_w[..., None], axis=1)
    out = jnp.zeros((B * S, D), jnp.float32)
    for e in range(N_EXPERTS):
        out = out + gates[:, e:e + 1] * swiglu(t, w_gate[e], w_up[e], w_down[e]).astype(jnp.float32)
    return out.astype(x.dtype).reshape(B, S, D)


def setup_inputs(seed: int = 0) -> dict:
    key = jax.random.key(seed)
    ks = jax.random.split(key, 32)

    def nrm(k, shape, scale):
        return scale * jax.random.normal(k, shape, jnp.float32)

    def gain(k, shape):
        return 1.0 + 0.02 * jax.random.normal(k, shape, jnp.float32)

    f_bias = jnp.linspace(3.0, 6.0, ML_HEADS, dtype=jnp.float32)
    ml_b_if = jnp.concatenate([nrm(ks[3], (DEPTH, ML_HEADS), 0.1),
                               f_bias + nrm(ks[4], (DEPTH, ML_HEADS), 0.1)], axis=-1)
    u = jax.random.uniform(ks[10], (DEPTH, RG_WIDTH), jnp.float32, 0.9, 0.999)
    a0 = u ** (1.0 / RG_C)
    rg_lam = jnp.log(a0) - jnp.log1p(-a0)
    return {
        'x': nrm(ks[0], (BATCH, SEQ, D_MODEL), 1.0),
        'norm_mix_g': gain(ks[1], (DEPTH, D_MODEL)),
        'w_in': nrm(ks[2], (DEPTH, D_MODEL, D_IN), D_MODEL ** -0.5),
        'ml_b_if': ml_b_if,
        'ml_norm_g': gain(ks[5], (DEPTH, ML_WIDTH)),
        'rg_conv_w': nrm(ks[6], (DEPTH, CONV_WIDTH, RG_WIDTH), CONV_WIDTH ** -0.5),
        'rg_conv_b': nrm(ks[7], (DEPTH, RG_WIDTH), 0.01),
        'rg_w_a': nrm(ks[8], (DEPTH, RG_BLOCKS, RG_BLOCK_DIM, RG_BLOCK_DIM), RG_BLOCK_DIM ** -0.5),
        'rg_b_a': nrm(ks[9], (DEPTH, RG_WIDTH), 0.01),
        'rg_w_x': nrm(ks[11], (DEPTH, RG_BLOCKS, RG_BLOCK_DIM, RG_BLOCK_DIM), RG_BLOCK_DIM ** -0.5),
        'rg_b_x': nrm(ks[12], (DEPTH, RG_WIDTH), 0.01),
        'rg_lam': rg_lam,
        'rg_norm_g': gain(ks[13], (DEPTH, RG_WIDTH)),
        'w_out': nrm(ks[14], (DEPTH, D_MIX, D_MODEL), D_MIX ** -0.5),
        'norm_ffn_g': gain(ks[15], (DEPTH, D_MODEL)),
        'ffn_w_gate': nrm(ks[16], (N_DENSE, D_MODEL, D_FF_DENSE), D_MODEL ** -0.5),
        'ffn_w_up': nrm(ks[17], (N_DENSE, D_MODEL, D_FF_DENSE), D_MODEL ** -0.5),
        'ffn_w_down': nrm(ks[18], (N_DENSE, D_FF_DENSE, D_MODEL), D_FF_DENSE ** -0.5),
        'moe_w_router': nrm(ks[19], (N_MOE, D_MODEL, N_EXPERTS), D_MODEL ** -0.5),
        'moe_w_gate': nrm(ks[20], (N_MOE, N_EXPERTS, D_MODEL, D_FF_EXPERT), D_MODEL ** -0.5),
        'moe_w_up': nrm(ks[21], (N_MOE, N_EXPERTS, D_MODEL, D_FF_EXPERT), D_MODEL ** -0.5),
        'moe_w_down': nrm(ks[22], (N_MOE, N_EXPERTS, D_FF_EXPERT, D_MODEL), D_FF_EXPERT ** -0.5),
        'norm_final_g': gain(ks[23], (D_MODEL,)),
    }


def reference(x, norm_mix_g, w_in, ml_b_if, ml_norm_g, rg_conv_w, rg_conv_b, rg_w_a, rg_b_a,
              rg_w_x, rg_b_x, rg_lam, rg_norm_g, w_out, norm_ffn_g, ffn_w_gate, ffn_w_up,
              ffn_w_down, moe_w_router, moe_w_gate, moe_w_up, moe_w_down, norm_final_g):
    h = x
    for l in range(DEPTH):
        h = h + hybrid_mixer(rmsnorm(h, norm_mix_g[l]), w_in[l], ml_b_if[l], ml_norm_g[l],
                             rg_conv_w[l], rg_conv_b[l], rg_w_a[l], rg_b_a[l], rg_w_x[l],
                             rg_b_x[l], rg_lam[l], rg_norm_g[l], w_out[l])
        u = rmsnorm(h, norm_ffn_g[l])
        j = l // 2
        if l % 2 == 0:
            h = h + swiglu(u, ffn_w_gate[j], ffn_w_up[j], ffn_w_down[j])
        else:
            h = h + moe_swiglu(u, moe_w_router[j], moe_w_gate[j], moe_w_up[j], moe_w_down[j])
    return rmsnorm(h, norm_final_g)
```

```python
import functools

import jax
import jax.numpy as jnp
from jax import lax
from jax.experimental import pallas as pl
from jax.experimental.pallas import tpu as pltpu

EPS = 1e-6
ML_HEADS = 4
ML_HEAD_DIM = 128
CHUNK = 128
RG_BLOCKS = 8
RG_C = 8.0
CONV_WIDTH = 4
N_EXPERTS = 8
LANES = 128
SUBLANES = 8
VMEM_LIMIT = 56 * 1024 * 1024

BF16 = jnp.bfloat16
F32 = jnp.float32


def _cparams(*sem):
    return pltpu.CompilerParams(dimension_semantics=sem, vmem_limit_bytes=VMEM_LIMIT)


def _dot(a, b):
    return jnp.dot(a, b, preferred_element_type=F32)


def _dot_nt(a, b):
    return lax.dot_general(a, b, (((1,), (1,)), ((), ())), preferred_element_type=F32)


def _dot_tn(a, b):
    return lax.dot_general(a, b, (((0,), (0,)), ((), ())), preferred_element_type=F32)


def _dot_f32(a, b):
    return jnp.dot(a, b, preferred_element_type=F32, precision=lax.Precision.HIGHEST)


def _sigmoid(x):
    return 1.0 / (1.0 + jnp.exp(-x))


def _log_sigmoid(x):
    return jnp.minimum(x, 0.0) - jnp.log(1.0 + jnp.exp(-jnp.abs(x)))


def _softplus(x):
    return jnp.maximum(x, 0.0) + jnp.log(1.0 + jnp.exp(-jnp.abs(x)))


def _gelu_tanh(x):
    return 0.5 * x * (1.0 + jnp.tanh(0.7978845608028654 * (x + 0.044715 * (x * x * x))))


def _rms_rows(x):
    return x * lax.rsqrt(jnp.mean(x * x, axis=-1, keepdims=True) + EPS)


def _inproj_kernel(h_ref, g_ref, b_ref, w_ref, zq_ref, zr_ref, zg_ref, *, n_q, n_r, n_col):
    xn = (_rms_rows(h_ref[...]) * g_ref[...]).astype(BF16)
    for c0 in range(0, n_q, n_col):
        zq_ref[:, c0:c0 + n_col] = _dot(xn, w_ref[:, c0:c0 + n_col]).astype(BF16)
    for c0 in range(0, n_r, n_col):
        zr_ref[:, c0:c0 + n_col] = _dot(xn, w_ref[:, n_q + c0:n_q + c0 + n_col]).astype(BF16)
    zg_ref[...] = _dot(xn, w_ref[:, n_q + n_r:]) + b_ref[...]


def _inproj(h, g, b_if, w_all, n_q, n_r, tm=512):
    T, D = h.shape
    tm = min(tm, T)
    n_all = w_all.shape[1]
    n_g = n_all - n_q - n_r
    kern = functools.partial(_inproj_kernel, n_q=n_q, n_r=n_r, n_col=512)
    return pl.pallas_call(
        kern,
        grid=(T // tm,),
        in_specs=[
            pl.BlockSpec((tm, D), lambda i: (i, 0)),
            pl.BlockSpec((1, D), lambda i: (0, 0)),
            pl.BlockSpec((1, n_g), lambda i: (0, 0)),
            pl.BlockSpec((D, n_all), lambda i: (0, 0)),
        ],
        out_specs=[
            pl.BlockSpec((tm, n_q), lambda i: (i, 0)),
            pl.BlockSpec((tm, n_r), lambda i: (i, 0)),
            pl.BlockSpec((tm, n_g), lambda i: (i, 0)),
        ],
        out_shape=[
            jax.ShapeDtypeStruct((T, n_q), BF16),
            jax.ShapeDtypeStruct((T, n_r), BF16),
            jax.ShapeDtypeStruct((T, n_g), F32),
        ],
        compiler_params=_cparams("parallel"),
        name="inproj",
    )(h, g, b_if, w_all)


def _mlstm_kernel(zq_ref, zg_ref, zgt_ref, gain_ref, y_ref, c_ref, m_ref, *, n_chunks):
    H, Dh, L = ML_HEADS, ML_HEAD_DIM, CHUNK
    W = H * Dh
    scale = Dh ** -0.5

    @pl.when(pl.program_id(1) == 0)
    def _():
        c_ref[...] = jnp.zeros_like(c_ref)
        m_ref[...] = jnp.zeros_like(m_ref)

    row = lax.broadcasted_iota(jnp.int32, (L, L), 0)
    col = lax.broadcasted_iota(jnp.int32, (L, L), 1)
    causal = col <= row
    tri_l = causal.astype(F32)
    tri_u = (row <= col).astype(F32)
    ones_blk = (lax.broadcasted_iota(jnp.int32, (L, Dh), 1) == 0).astype(BF16)

    def chunk_body(c, carry):
        r0 = pl.multiple_of(c * L, L)
        gcol = zg_ref[pl.ds(r0, L), :]
        grow = zgt_ref[c]
        bcol_all = _dot_f32(tri_l, _log_sigmoid(gcol))
        brow_all = _dot_f32(_log_sigmoid(grow), tri_u)
        for hd in range(H):
            q = zq_ref[pl.ds(r0, L), hd * Dh:(hd + 1) * Dh]
            k = zq_ref[pl.ds(r0, L), W + hd * Dh:W + (hd + 1) * Dh]
            v = zq_ref[pl.ds(r0, L), 2 * W + hd * Dh:2 * W + (hd + 1) * Dh]
            o = zq_ref[pl.ds(r0, L), 3 * W + hd * Dh:3 * W + (hd + 1) * Dh]
            v_aug = jnp.concatenate([v, ones_blk], axis=1)

            li_col = gcol[:, hd:hd + 1]
            li_row = grow[hd:hd + 1, :]
            b_col = bcol_all[:, H + hd:H + hd + 1]
            b_row = brow_all[H + hd:H + hd + 1, :]
            g_tot = b_col[L - 1:L, :]

            c_prev = c_ref[hd]
            m_prev = m_ref[hd]

            d_log = jnp.where(causal, b_col - b_row + li_row, -jnp.inf)
            m_inter = b_col + m_prev
            m_comb = jnp.maximum(m_inter, jnp.max(d_log, axis=1, keepdims=True))
            s_w = jnp.exp(d_log - m_comb) * (_dot_nt(q, k) * scale)
            inter_scale = jnp.exp(m_inter - m_comb)
            num = (_dot(s_w.astype(BF16), v_aug)
                   + inter_scale * _dot(q, c_prev.astype(BF16)))
            den = num[:, Dh:Dh + 1]
            hh = num[:, :Dh] / jnp.maximum(jnp.abs(den), jnp.exp(-m_comb))
            ht = _rms_rows(hh) * gain_ref[:, hd * Dh:(hd + 1) * Dh]
            y_ref[pl.ds(r0, L), hd * Dh:(hd + 1) * Dh] = (_sigmoid(o.astype(F32)) * ht).astype(BF16)

            a = g_tot - b_col + li_col
            m_loc = jnp.max(a, axis=0, keepdims=True)
            w = jnp.exp(a - m_loc)
            kw = (k.astype(F32) * (w * scale)).astype(BF16)
            c_loc = _dot_tn(kw, v_aug)
            m_new = jnp.maximum(g_tot + m_prev, m_loc)
            s_old = jnp.exp(g_tot + m_prev - m_new)
            s_loc = jnp.exp(m_loc - m_new)
            c_ref[hd] = s_old * c_prev + s_loc * c_loc
            m_ref[hd] = m_new
        return carry

    lax.fori_loop(0, n_chunks, chunk_body, 0)


def _mlstm(zq, zg, zgt, gain, B, S, rows=1024):
    T = B * S
    rows = min(rows, S)
    n_chunks = rows // CHUNK
    steps = S // rows
    W = ML_HEADS * ML_HEAD_DIM
    kern = functools.partial(_mlstm_kernel, n_chunks=n_chunks)
    return pl.pallas_call(
        kern,
        grid=(B, steps),
        in_specs=[
            pl.BlockSpec((rows, 4 * W), lambda b, s: (b * steps + s, 0)),
            pl.BlockSpec((rows, LANES), lambda b, s: (b * steps + s, 0)),
            pl.BlockSpec((n_chunks, SUBLANES, CHUNK), lambda b, s: (b * steps + s, 0, 0)),
            pl.BlockSpec((1, W), lambda b, s: (0, 0)),
        ],
        out_specs=pl.BlockSpec((rows, W), lambda b, s: (b * steps + s, 0)),
        out_shape=jax.ShapeDtypeStruct((T, W), BF16),
        scratch_shapes=[
            pltpu.VMEM((ML_HEADS, ML_HEAD_DIM, 2 * ML_HEAD_DIM), F32),
            pltpu.VMEM((ML_HEADS, 1, 1), F32),
        ],
        compiler_params=_cparams("parallel", "arbitrary"),
        name="mlstm",
    )(zq, zg, zgt, gain)


def _rglru_kernel(zr_ref, cw_ref, cb_ref, wg_ref, bg_ref, lam_ref, gain_ref, y_ref,
                  xpad_ref, a_ref, u_ref, h_ref, *, rows):
    Wd = y_ref.shape[1]
    PAD = SUBLANES

    @pl.when(pl.program_id(1) == 0)
    def _():
        xpad_ref[0:PAD, :] = jnp.zeros((PAD, Wd), F32)
        h_ref[...] = jnp.zeros_like(h_ref)

    xpad_ref[PAD:PAD + rows, :] = zr_ref[:, :Wd].astype(F32)
    xc = cb_ref[...] + jnp.zeros((rows, Wd), F32)
    for j in range(CONV_WIDTH):
        off = PAD - (CONV_WIDTH - 1) + j
        xc = xc + cw_ref[j:j + 1, :] * xpad_ref[off:off + rows, :]
    tail = xpad_ref[rows:rows + PAD, :]
    xpad_ref[0:PAD, :] = tail

    gates = _dot(xc.astype(BF16), wg_ref[...]) + bg_ref[...]
    r = _sigmoid(gates[:, :Wd])
    ig = _sigmoid(gates[:, Wd:])
    log_a = (-RG_C) * r * _softplus(-lam_ref[...])
    a = jnp.exp(log_a)
    a_ref[...] = a
    u_ref[...] = jnp.sqrt(1.0 - a * a) * (ig * xc)

    sub = lax.broadcasted_iota(jnp.int32, (SUBLANES, Wd), 0)

    def scan_body(i, hprev):
        r0 = pl.multiple_of(i * SUBLANES, SUBLANES)
        av = a_ref[pl.ds(r0, SUBLANES), :]
        uv = u_ref[pl.ds(r0, SUBLANES), :]
        for d in (1, 2, 4):
            keep = sub >= d
            a_s = jnp.where(keep, pltpu.roll(av, d, 0), 1.0)
            u_s = jnp.where(keep, pltpu.roll(uv, d, 0), 0.0)
            uv = av * u_s + uv
            av = av * a_s
        hv = av * hprev + uv
        u_ref[pl.ds(r0, SUBLANES), :] = hv
        return jnp.broadcast_to(hv[SUBLANES - 1:SUBLANES, :], (SUBLANES, Wd))

    hlast = lax.fori_loop(0, rows // SUBLANES, scan_body, h_ref[...])
    h_ref[...] = hlast

    yv = u_ref[...] * _gelu_tanh(zr_ref[:, Wd:].astype(F32))
    y_ref[...] = (_rms_rows(yv) * gain_ref[...]).astype(BF16)


def _rglru(zr, conv_w, conv_b, w_gates, b_gates, lam, gain, B, S, rows=512):
    T = B * S
    Wd = zr.shape[1] // 2
    rows = min(rows, S)
    steps = S // rows
    kern = functools.partial(_rglru_kernel, rows=rows)
    const = lambda b, s: (0, 0)
    return pl.pallas_call(
        kern,
        grid=(B, steps),
        in_specs=[
            pl.BlockSpec((rows, 2 * Wd), lambda b, s: (b * steps + s, 0)),
            pl.BlockSpec((CONV_WIDTH, Wd), const),
            pl.BlockSpec((1, Wd), const),
            pl.BlockSpec((Wd, 2 * Wd), const),
            pl.BlockSpec((1, 2 * Wd), const),
            pl.BlockSpec((1, Wd), const),
            pl.BlockSpec((1, Wd), const),
        ],
        out_specs=pl.BlockSpec((rows, Wd), lambda b, s: (b * steps + s, 0)),
        out_shape=jax.ShapeDtypeStruct((T, Wd), BF16),
        scratch_shapes=[
            pltpu.VMEM((rows + SUBLANES, Wd), F32),
            pltpu.VMEM((rows, Wd), F32),
            pltpu.VMEM((rows, Wd), F32),
            pltpu.VMEM((SUBLANES, Wd), F32),
        ],
        compiler_params=_cparams("parallel", "arbitrary"),
        name="rglru",
    )(zr, conv_w, conv_b, w_gates, b_gates, lam, gain)


def _outproj_kernel(yml_ref, yrg_ref, h_ref, w_ref, g_ref, *rest):
    half = yml_ref.shape[1]
    hn = (h_ref[...] + _dot(yml_ref[...], w_ref[:half, :]) + _dot(yrg_ref[...], w_ref[half:, :]))
    un = _rms_rows(hn) * g_ref[...]
    if len(rest) == 4:
        wr_ref, hn_ref, un_ref, lg_ref = rest
        lg_ref[...] = _dot_f32(un, wr_ref[...])
    else:
        hn_ref, un_ref = rest
    hn_ref[...] = hn
    un_ref[...] = un.astype(un_ref.dtype)


def _outproj(yml, yrg, h, w_out, g_ffn, w_router, tm=512):
    T, D = h.shape
    half = yml.shape[1]
    tm = min(tm, T)
    with_router = w_router is not None
    row = lambda i: (i, 0)
    const = lambda i: (0, 0)
    in_specs = [
        pl.BlockSpec((tm, half), row),
        pl.BlockSpec((tm, half), row),
        pl.BlockSpec((tm, D), row),
        pl.BlockSpec((2 * half, D), const),
        pl.BlockSpec((1, D), const),
    ]
    out_specs = [pl.BlockSpec((tm, D), row), pl.BlockSpec((tm, D), row)]
    out_shape = [jax.ShapeDtypeStruct((T, D), F32),
                 jax.ShapeDtypeStruct((T, D), F32 if with_router else BF16)]
    args = [yml, yrg, h, w_out, g_ffn]
    if with_router:
        in_specs.append(pl.BlockSpec((D, LANES), const))
        out_specs.append(pl.BlockSpec((tm, LANES), row))
        out_shape.append(jax.ShapeDtypeStruct((T, LANES), F32))
        args.append(w_router)
    return pl.pallas_call(
        _outproj_kernel,
        grid=(T // tm,),
        in_specs=in_specs,
        out_specs=out_specs,
        out_shape=out_shape,
        compiler_params=_cparams("parallel"),
        name="outproj_router" if with_router else "outproj",
    )(*args)


def _swiglu_step(x, wg_ref, wu_ref, wd_ref, acc_ref, j):
    g = _dot(x, wg_ref[...])
    u = _dot(x, wu_ref[...])
    act = (g * _sigmoid(g) * u).astype(BF16)
    part = _dot(act, wd_ref[...])

    @pl.when(j == 0)
    def _():
        acc_ref[...] = part

    @pl.when(j > 0)
    def _():
        acc_ref[...] += part


def _ffn_dense_kernel(u_ref, h_ref, wg_ref, wu_ref, wd_ref, o_ref, acc_ref):
    j = pl.program_id(1)
    _swiglu_step(u_ref[...], wg_ref, wu_ref, wd_ref, acc_ref, j)

    @pl.when(j == pl.num_programs(1) - 1)
    def _():
        o_ref[...] = h_ref[...] + acc_ref[...]


def _ffn_dense(un, h, wg, wu, wd, tm=1024, tf=256):
    T, D = un.shape
    F = wg.shape[1]
    tm = min(tm, T)
    tf = min(tf, F)
    return pl.pallas_call(
        _ffn_dense_kernel,
        grid=(T // tm, F // tf),
        in_specs=[
            pl.BlockSpec((tm, D), lambda i, j: (i, 0)),
            pl.BlockSpec((tm, D), lambda i, j: (i, 0)),
            pl.BlockSpec((D, tf), lambda i, j: (0, j)),
            pl.BlockSpec((D, tf), lambda i, j: (0, j)),
            pl.BlockSpec((tf, D), lambda i, j: (j, 0)),
        ],
        out_specs=pl.BlockSpec((tm, D), lambda i, j: (i, 0)),
        out_shape=jax.ShapeDtypeStruct((T, D), F32),
        scratch_shapes=[pltpu.VMEM((tm, D), F32)],
        compiler_params=_cparams("parallel", "arbitrary"),
        name="ffn_dense",
    )(un, h, wg, wu, wd)


def _ffn_grouped_kernel(te_ref, nt_ref, x_ref, wg_ref, wu_ref, wd_ref, o_ref, xb_ref, acc_ref):
    i = pl.program_id(0)
    j = pl.program_id(1)

    @pl.when(i < nt_ref[0])
    def _():
        @pl.when(j == 0)
        def _():
            xb_ref[...] = x_ref[...].astype(BF16)

        _swiglu_step(xb_ref[...], wg_ref.at[0], wu_ref.at[0], wd_ref.at[0], acc_ref, j)

        @pl.when(j == pl.num_programs(1) - 1)
        def _():
            o_ref[...] = acc_ref[...]

    @pl.when(jnp.logical_and(i >= nt_ref[0], j == pl.num_programs(1) - 1))
    def _():
        o_ref[...] = jnp.zeros_like(o_ref)


def _ffn_grouped(tile_expert, n_tiles, xs, wg, wu, wd, tm, tf=512):
    P, D = xs.shape
    F = wg.shape[2]
    tf = min(tf, F)
    nj = F // tf

    def jj(i, j, nt):
        return jnp.where(i < nt[0], j, nj - 1)

    grid_spec = pltpu.PrefetchScalarGridSpec(
        num_scalar_prefetch=2,
        grid=(P // tm, nj),
        in_specs=[
            pl.BlockSpec((tm, D), lambda i, j, te, nt: (jnp.minimum(i, nt[0] - 1), 0)),
            pl.BlockSpec((1, D, tf), lambda i, j, te, nt: (te[i], 0, jj(i, j, nt))),
            pl.BlockSpec((1, D, tf), lambda i, j, te, nt: (te[i], 0, jj(i, j, nt))),
            pl.BlockSpec((1, tf, D), lambda i, j, te, nt: (te[i], jj(i, j, nt), 0)),
        ],
        out_specs=pl.BlockSpec((tm, D), lambda i, j, te, nt: (i, 0)),
        scratch_shapes=[pltpu.VMEM((tm, D), BF16), pltpu.VMEM((tm, D), F32)],
    )
    return pl.pallas_call(
        _ffn_grouped_kernel,
        grid_spec=grid_spec,
        out_shape=jax.ShapeDtypeStruct((P, D), F32),
        compiler_params=_cparams("arbitrary", "arbitrary"),
        name="ffn_grouped",
    )(tile_expert, n_tiles, xs, wg, wu, wd)


def _router_kernel(lg_ref, rt_ref, cnt_ref, carry_ref):
    tm = lg_ref.shape[0]

    @pl.when(pl.program_id(0) == 0)
    def _():
        carry_ref[...] = jnp.zeros_like(carry_ref)

    lane = lax.broadcasted_iota(jnp.int32, (tm, LANES), 1).astype(F32)
    lg = jnp.where(lane < N_EXPERTS, lg_ref[...], -jnp.inf)
    v1 = jnp.max(lg, axis=1, keepdims=True)
    e1 = jnp.min(jnp.where(lg == v1, lane, float(LANES)), axis=1, keepdims=True)
    lg2 = jnp.where(lane == e1, -jnp.inf, lg)
    v2 = jnp.max(lg2, axis=1, keepdims=True)
    e2 = jnp.min(jnp.where(lg2 == v2, lane, float(LANES)), axis=1, keepdims=True)
    ex = jnp.exp(v2 - v1)
    w1 = 1.0 / (1.0 + ex)
    w2 = ex / (1.0 + ex)

    oh1 = (lane == e1).astype(F32)
    oh2 = (lane == e2).astype(F32)
    r_i = lax.broadcasted_iota(jnp.int32, (tm, tm), 0)
    c_i = lax.broadcasted_iota(jnp.int32, (tm, tm), 1)
    strict = (c_i < r_i).astype(BF16)
    before = _dot(strict, (oh1 + oh2).astype(BF16)) + carry_ref[0:1, :]
    rank1 = jnp.sum(before * oh1, axis=1, keepdims=True)
    rank2 = jnp.sum(before * oh2, axis=1, keepdims=True)
    total = carry_ref[0:1, :] + jnp.sum(oh1 + oh2, axis=0, keepdims=True)
    carry_ref[...] = jnp.broadcast_to(total, carry_ref.shape)
    cnt_ref[...] = jnp.broadcast_to(total, cnt_ref.shape)

    out = jnp.where(lane == 0, e1, 0.0)
    out = jnp.where(lane == 1, e2, out)
    out = jnp.where(lane == 2, w1, out)
    out = jnp.where(lane == 3, w2, out)
    out = jnp.where(lane == 4, rank1, out)
    out = jnp.where(lane == 5, rank2, out)
    rt_ref[...] = out


def _router(logits, tm=512):
    T = logits.shape[0]
    tm = min(tm, T)
    return pl.pallas_call(
        _router_kernel,
        grid=(T // tm,),
        in_specs=[pl.BlockSpec((tm, LANES), lambda i: (i, 0))],
        out_specs=[
            pl.BlockSpec((tm, LANES), lambda i: (i, 0)),
            pl.BlockSpec((SUBLANES, LANES), lambda i: (0, 0)),
        ],
        out_shape=[
            jax.ShapeDtypeStruct((T, LANES), F32),
            jax.ShapeDtypeStruct((SUBLANES, LANES), F32),
        ],
        scratch_shapes=[pltpu.VMEM((SUBLANES, LANES), F32)],
        compiler_params=_cparams("arbitrary"),
        name="router",
    )(logits)


def _dispatch_kernel(p1_ref, p2_ref, u_ref, z_hbm, xs_hbm, sem):
    del z_hbm
    tm = u_ref.shape[0]

    def row_copy(r, pos):
        return pltpu.make_async_copy(u_ref.at[pl.ds(r, 1)], xs_hbm.at[pl.ds(pos, 1)], sem)

    def issue(r, carry):
        row_copy(r, p1_ref[r]).start()
        row_copy(r, p2_ref[r]).start()
        return carry

    lax.fori_loop(0, tm, issue, 0)
    for _ in range(2):
        pltpu.make_async_copy(u_ref, xs_hbm.at[pl.ds(0, tm)], sem).wait()


def _dispatch(pos1, pos2, un, P, tm=512):
    T, D = un.shape
    tm = min(tm, T)
    zeros = jnp.zeros((P, D), un.dtype)
    return pl.pallas_call(
        _dispatch_kernel,
        grid=(T // tm,),
        in_specs=[
            pl.BlockSpec((tm,), lambda i: (i,), memory_space=pltpu.SMEM),
            pl.BlockSpec((tm,), lambda i: (i,), memory_space=pltpu.SMEM),
            pl.BlockSpec((tm, D), lambda i: (i, 0)),
            pl.BlockSpec(memory_space=pl.ANY),
        ],
        out_specs=pl.BlockSpec(memory_space=pl.ANY),
        out_shape=jax.ShapeDtypeStruct((P, D), un.dtype),
        scratch_shapes=[pltpu.SemaphoreType.DMA(())],
        input_output_aliases={3: 0},
        compiler_params=_cparams("arbitrary"),
        name="dispatch",
    )(pos1, pos2, un, zeros)


def _combine_kernel(p1_ref, p2_ref, h_ref, rt_ref, g_ref, y_hbm, o_ref, buf_ref, sem, *, final_norm):
    tm = h_ref.shape[0]

    def row_copy(k, r, pos):
        return pltpu.make_async_copy(y_hbm.at[pl.ds(pos, 1)], buf_ref.at[k, pl.ds(r, 1)], sem)

    def issue(r, carry):
        row_copy(0, r, p1_ref[r]).start()
        row_copy(1, r, p2_ref[r]).start()
        return carry

    lax.fori_loop(0, tm, issue, 0)
    for k in range(2):
        pltpu.make_async_copy(y_hbm.at[pl.ds(0, tm)], buf_ref.at[k], sem).wait()

    rt = rt_ref[...]
    out = h_ref[...] + (rt[:, 2:3] * buf_ref[0] + rt[:, 3:4] * buf_ref[1])
    if final_norm:
        out = _rms_rows(out) * g_ref[...]
    o_ref[...] = out


def _combine(pos1, pos2, h, routing, g_final, y, final_norm, tm=256):
    T, D = h.shape
    tm = min(tm, T)
    kern = functools.partial(_combine_kernel, final_norm=final_norm)
    return pl.pallas_call(
        kern,
        grid=(T // tm,),
        in_specs=[
            pl.BlockSpec((tm,), lambda i: (i,), memory_space=pltpu.SMEM),
            pl.BlockSpec((tm,), lambda i: (i,), memory_space=pltpu.SMEM),
            pl.BlockSpec((tm, D), lambda i: (i, 0)),
            pl.BlockSpec((tm, LANES), lambda i: (i, 0)),
            pl.BlockSpec((1, D), lambda i: (0, 0)),
            pl.BlockSpec(memory_space=pl.ANY),
        ],
        out_specs=pl.BlockSpec((tm, D), lambda i: (i, 0)),
        out_shape=jax.ShapeDtypeStruct((T, D), F32),
        scratch_shapes=[pltpu.VMEM((2, tm, D), F32), pltpu.SemaphoreType.DMA(())],
        compiler_params=_cparams("arbitrary"),
        name="combine",
    )(pos1, pos2, h, routing, g_final, y)


def _final_norm_kernel(h_ref, g_ref, o_ref):
    o_ref[...] = _rms_rows(h_ref[...]) * g_ref[...]


def _final_norm(h, g, tm=1024):
    T, D = h.shape
    tm = min(tm, T)
    return pl.pallas_call(
        _final_norm_kernel,
        grid=(T // tm,),
        in_specs=[pl.BlockSpec((tm, D), lambda i: (i, 0)), pl.BlockSpec((1, D), lambda i: (0, 0))],
        out_specs=pl.BlockSpec((tm, D), lambda i: (i, 0)),
        out_shape=jax.ShapeDtypeStruct((T, D), F32),
        compiler_params=_cparams("parallel"),
        name="final_norm",
    )(h, g)


def _block_diag(w):
    G, n, _ = w.shape
    eye = jnp.eye(G, dtype=w.dtype)
    return (eye[:, None, :, None] * w[:, :, None, :]).reshape(G * n, G * n)


def _moe_layer(h, un, logits, w_gate, w_up, w_down, g_final, final_norm, tm):
    T, D = h.shape
    E = N_EXPERTS
    routing, counts = _router(logits)
    e1 = routing[:, 0].astype(jnp.int32)
    e2 = routing[:, 1].astype(jnp.int32)
    rank1 = routing[:, 4].astype(jnp.int32)
    rank2 = routing[:, 5].astype(jnp.int32)
    cnt = counts[0, :E].astype(jnp.int32)
    tiles_per = (cnt + tm - 1) // tm
    tile_end = jnp.cumsum(tiles_per)
    offs = (tile_end - tiles_per) * tm
    pos1 = offs[e1] + rank1
    pos2 = offs[e2] + rank2
    n_slots = (2 * T) // tm + E
    P = n_slots * tm
    tile_expert = jnp.minimum(
        jnp.sum(jnp.arange(n_slots, dtype=jnp.int32)[:, None] >= tile_end[None, :], axis=1), E - 1
    ).astype(jnp.int32)
    n_tiles = tile_end[E - 1:E].astype(jnp.int32)

    xs = _dispatch(pos1, pos2, un, P)
    y = _ffn_grouped(tile_expert, n_tiles, xs, w_gate, w_up, w_down, tm)
    return _combine(pos1, pos2, h, routing, g_final, y, final_norm)


def _forward(x, norm_mix_g, w_in, ml_b_if, ml_norm_g, rg_conv_w, rg_conv_b, rg_w_a, rg_b_a,
             rg_w_x, rg_b_x, rg_lam, rg_norm_g, w_out, norm_ffn_g, ffn_w_gate, ffn_w_up,
             ffn_w_down, moe_w_router, moe_w_gate, moe_w_up, moe_w_down, norm_final_g,
             moe_tile=512):
    B, S, D = x.shape
    T = B * S
    depth = w_in.shape[0]
    ml_w = ML_HEADS * ML_HEAD_DIM
    n_q = 4 * ml_w
    n_if = 2 * ML_HEADS
    rg_w = rg_lam.shape[1]
    n_r = 2 * rg_w

    h = x.reshape(T, D)
    g_final = norm_final_g.reshape(1, D)
    for l in range(depth):
        wl = w_in[l]
        w_all = jnp.concatenate(
            [wl[:, :n_q], wl[:, n_q + n_if:], wl[:, n_q:n_q + n_if],
             jnp.zeros((D, LANES - n_if), wl.dtype)], axis=1).astype(BF16)
        b_if = jnp.concatenate([ml_b_if[l], jnp.zeros((LANES - n_if,), F32)]).reshape(1, LANES)
        zq, zr, zg = _inproj(h, norm_mix_g[l].reshape(1, D), b_if, w_all, n_q, n_r)

        zgt = zg[:, :SUBLANES].T.reshape(SUBLANES, T // CHUNK, CHUNK).transpose(1, 0, 2)
        yml = _mlstm(zq, zg, zgt, ml_norm_g[l].reshape(1, ml_w), B, S)

        w_gates = jnp.concatenate([_block_diag(rg_w_a[l]), _block_diag(rg_w_x[l])], axis=1).astype(BF16)
        b_gates = jnp.concatenate([rg_b_a[l], rg_b_x[l]]).reshape(1, n_r)
        yrg = _rglru(zr, rg_conv_w[l], rg_conv_b[l].reshape(1, rg_w), w_gates, b_gates,
                     rg_lam[l].reshape(1, rg_w), rg_norm_g[l].reshape(1, rg_w), B, S)

        j = l // 2
        is_moe = (l % 2 == 1)
        w_router = None
        if is_moe:
            w_router = jnp.concatenate(
                [moe_w_router[j], jnp.zeros((D, LANES - N_EXPERTS), F32)], axis=1)
        res = _outproj(yml, yrg, h, w_out[l].astype(BF16), norm_ffn_g[l].reshape(1, D), w_router)
        h, un = res[0], res[1]
        last = (l == depth - 1)
        if is_moe:
            logits = res[2]
            h = _moe_layer(h, un, logits, moe_w_gate[j].astype(BF16), moe_w_up[j].astype(BF16),
                           moe_w_down[j].astype(BF16), g_final, last, moe_tile)
        else:
            h = _ffn_dense(un, h, ffn_w_gate[j].astype(BF16), ffn_w_up[j].astype(BF16),
                           ffn_w_down[j].astype(BF16))
            if last:
                h = _final_norm(h, g_final)
    return h.reshape(B, S, D)


def kernel(x, norm_mix_g, w_in, ml_b_if, ml_norm_g, rg_conv_w, rg_conv_b, rg_w_a, rg_b_a, rg_w_x,
           rg_b_x, rg_lam, rg_norm_g, w_out, norm_ffn_g, ffn_w_gate, ffn_w_up, ffn_w_down,
           moe_w_router, moe_w_gate, moe_w_up, moe_w_down, norm_final_g):
    return _forward(x, norm_mix_g, w_in, ml_b_if, ml_norm_g, rg_conv_w, rg_conv_b, rg_w_a, rg_b_a,
                    rg_w_x, rg_b_x, rg_lam, rg_norm_g, w_out, norm_ffn_g, ffn_w_gate, ffn_w_up,
                    ffn_w_down, moe_w_router, moe_w_gate, moe_w_up, moe_w_down, norm_final_g)
```

```python
import functools

import jax
import jax.numpy as jnp
from jax import lax
from jax.experimental import pallas as pl
from jax.experimental.pallas import tpu as pltpu

EPS = 1e-6
ML_HEADS = 4
ML_HEAD_DIM = 128
CHUNK = 128
RG_BLOCKS = 8
RG_C = 8.0
CONV_WIDTH = 4
N_EXPERTS = 8
LANES = 128
SUBLANES = 8
VMEM_LIMIT = 56 * 1024 * 1024

BF16 = jnp.bfloat16
F32 = jnp.float32


def _cparams(*sem):
    return pltpu.CompilerParams(dimension_semantics=sem, vmem_limit_bytes=VMEM_LIMIT)


def _dot(a, b):
    return jnp.dot(a, b, preferred_element_type=F32)


def _dot_nt(a, b):
    return lax.dot_general(a, b, (((1,), (1,)), ((), ())), preferred_element_type=F32)


def _dot_tn(a, b):
    return lax.dot_general(a, b, (((0,), (0,)), ((), ())), preferred_element_type=F32)


def _dot_f32(a, b):
    return jnp.dot(a, b, preferred_element_type=F32, precision=lax.Precision.HIGHEST)


def _sigmoid(x):
    return 1.0 / (1.0 + jnp.exp(-x))


def _log_sigmoid(x):
    return jnp.minimum(x, 0.0) - jnp.log(1.0 + jnp.exp(-jnp.abs(x)))


def _softplus(x):
    return jnp.maximum(x, 0.0) + jnp.log(1.0 + jnp.exp(-jnp.abs(x)))


def _gelu_tanh(x):
    return 0.5 * x * (1.0 + jnp.tanh(0.7978845608028654 * (x + 0.044715 * (x * x * x))))


def _rms_rows(x):
    return x * lax.rsqrt(jnp.mean(x * x, axis=-1, keepdims=True) + EPS)


def _inproj_kernel(h_ref, g_ref, b_ref, w_ref, wkt_ref, zq_ref, kt_ref, zr_ref, zg_ref,
                   *, n_q, n_r, n_col):
    xn = (_rms_rows(h_ref[...]) * g_ref[...]).astype(BF16)
    for c0 in range(0, n_q, n_col):
        zq_ref[:, c0:c0 + n_col] = _dot(xn, w_ref[:, c0:c0 + n_col]).astype(BF16)
    for c0 in range(0, n_r, n_col):
        zr_ref[:, c0:c0 + n_col] = _dot(xn, w_ref[:, n_q + c0:n_q + c0 + n_col]).astype(BF16)
    zg_ref[...] = _dot(xn, w_ref[:, n_q + n_r:]) + b_ref[...]
    kt = _dot_nt(wkt_ref[...], xn).astype(BF16)
    for c in range(kt_ref.shape[0]):
        kt_ref[c] = kt[:, c * CHUNK:(c + 1) * CHUNK]


def _inproj(h, g, b_if, w_all, wkt, n_q, n_r, tm=512):
    T, D = h.shape
    tm = min(tm, T)
    n_all = w_all.shape[1]
    n_g = n_all - n_q - n_r
    n_k = wkt.shape[0]
    kern = functools.partial(_inproj_kernel, n_q=n_q, n_r=n_r, n_col=512)
    return pl.pallas_call(
        kern,
        grid=(T // tm,),
        in_specs=[
            pl.BlockSpec((tm, D), lambda i: (i, 0)),
            pl.BlockSpec((1, D), lambda i: (0, 0)),
            pl.BlockSpec((1, n_g), lambda i: (0, 0)),
            pl.BlockSpec((D, n_all), lambda i: (0, 0)),
            pl.BlockSpec((n_k, D), lambda i: (0, 0)),
        ],
        out_specs=[
            pl.BlockSpec((tm, n_q), lambda i: (i, 0)),
            pl.BlockSpec((tm // CHUNK, n_k, CHUNK), lambda i: (i, 0, 0)),
            pl.BlockSpec((tm, n_r), lambda i: (i, 0)),
            pl.BlockSpec((tm, n_g), lambda i: (i, 0)),
        ],
        out_shape=[
            jax.ShapeDtypeStruct((T, n_q), BF16),
            jax.ShapeDtypeStruct((T // CHUNK, n_k, CHUNK), BF16),
            jax.ShapeDtypeStruct((T, n_r), BF16),
            jax.ShapeDtypeStruct((T, n_g), F32),
        ],
        compiler_params=_cparams("parallel"),
        name="inproj",
    )(h, g, b_if, w_all, wkt)


def _mlstm_kernel(zq_ref, kt_ref, zg_ref, zgt_ref, gain_ref, y_ref, c_ref, m_ref, *, n_chunks):
    H, Dh, L = ML_HEADS, ML_HEAD_DIM, CHUNK
    W = H * Dh
    scale = Dh ** -0.5

    @pl.when(pl.program_id(1) == 0)
    def _():
        c_ref[...] = jnp.zeros_like(c_ref)
        m_ref[...] = jnp.zeros_like(m_ref)

    row = lax.broadcasted_iota(jnp.int32, (L, L), 0)
    col = lax.broadcasted_iota(jnp.int32, (L, L), 1)
    causal = col <= row
    tri_l = causal.astype(F32)
    tri_u = (row <= col).astype(F32)
    ones_blk = jnp.ones((L, Dh), BF16)

    def chunk_body(c, carry):
        r0 = pl.multiple_of(c * L, L)
        gcol = zg_ref[pl.ds(r0, L), :]
        grow = zgt_ref[c]
        bcol_all = _dot_f32(tri_l, _log_sigmoid(gcol))
        brow_all = _dot_f32(_log_sigmoid(grow), tri_u)
        cm = gcol - pltpu.roll(bcol_all, LANES - H, 1)
        d = 1
        while d < L:
            cm = jnp.maximum(cm, jnp.where(row >= d, pltpu.roll(cm, d, 0), -jnp.inf))
            d *= 2

        for hd in range(H):
            q = zq_ref[pl.ds(r0, L), hd * Dh:(hd + 1) * Dh]
            v = zq_ref[pl.ds(r0, L), W + hd * Dh:W + (hd + 1) * Dh]
            o = zq_ref[pl.ds(r0, L), 2 * W + hd * Dh:2 * W + (hd + 1) * Dh]
            kt = kt_ref[c, hd * Dh:(hd + 1) * Dh, :]
            v_aug = jnp.concatenate([v, ones_blk], axis=1)

            li_row = grow[hd:hd + 1, :]
            b_row = brow_all[H + hd:H + hd + 1, :]
            g_tot = b_row[:, L - 1:L]
            r_row = li_row - b_row

            c_prev = c_ref[hd]
            m_prev = m_ref[hd]

            mx = jnp.maximum(jnp.broadcast_to(cm[:, hd:hd + 1], (L, L)), m_prev)
            bb = jnp.broadcast_to(bcol_all[:, H + hd:H + hd + 1], (L, L))
            p = jnp.exp(jnp.where(causal, r_row - mx, -jnp.inf))
            s_w = p * (_dot(q, kt) * scale)
            inter = jnp.exp(m_prev - mx)
            intra = _dot(s_w.astype(BF16), v_aug)
            cross = _dot(q, c_prev.astype(BF16))
            num = intra[:, :Dh] + inter * cross[:, :Dh]
            den = intra[:, Dh:] + inter * cross[:, Dh:]
            hh = num / jnp.maximum(jnp.abs(den), jnp.exp(-(bb + mx)))
            ht = _rms_rows(hh) * gain_ref[:, hd * Dh:(hd + 1) * Dh]
            y_ref[pl.ds(r0, L), hd * Dh:(hd + 1) * Dh] = (_sigmoid(o.astype(F32)) * ht).astype(BF16)

            a = g_tot + r_row
            m_loc = jnp.max(a, axis=1, keepdims=True)
            w = jnp.exp(a - m_loc) * scale
            c_loc = _dot((kt.astype(F32) * w).astype(BF16), v_aug)
            m_new = jnp.maximum(g_tot + m_prev, m_loc)
            s_old = jnp.exp(g_tot + m_prev - m_new)
            s_loc = jnp.exp(m_loc - m_new)
            c_ref[hd] = s_old * c_prev + s_loc * c_loc
            m_ref[hd] = m_new
        return carry

    lax.fori_loop(0, n_chunks, chunk_body, 0)


def _mlstm(zq, kt, zg, zgt, gain, B, S, rows=1024):
    T = B * S
    rows = min(rows, S)
    n_chunks = rows // CHUNK
    steps = S // rows
    W = ML_HEADS * ML_HEAD_DIM
    kern = functools.partial(_mlstm_kernel, n_chunks=n_chunks)
    return pl.pallas_call(
        kern,
        grid=(B, steps),
        in_specs=[
            pl.BlockSpec((rows, 3 * W), lambda b, s: (b * steps + s, 0)),
            pl.BlockSpec((n_chunks, W, CHUNK), lambda b, s: (b * steps + s, 0, 0)),
            pl.BlockSpec((rows, LANES), lambda b, s: (b * steps + s, 0)),
            pl.BlockSpec((n_chunks, SUBLANES, CHUNK), lambda b, s: (b * steps + s, 0, 0)),
            pl.BlockSpec((1, W), lambda b, s: (0, 0)),
        ],
        out_specs=pl.BlockSpec((rows, W), lambda b, s: (b * steps + s, 0)),
        out_shape=jax.ShapeDtypeStruct((T, W), BF16),
        scratch_shapes=[
            pltpu.VMEM((ML_HEADS, ML_HEAD_DIM, 2 * ML_HEAD_DIM), F32),
            pltpu.VMEM((ML_HEADS, 1, 1), F32),
        ],
        compiler_params=_cparams("parallel", "arbitrary"),
        name="mlstm",
    )(zq, kt, zg, zgt, gain)


def _rglru_kernel(zr_ref, cw_ref, cb_ref, wg_ref, bg_ref, lam_ref, gain_ref, y_ref,
                  xpad_ref, a_ref, u_ref, h_ref, *, rows):
    Wd = y_ref.shape[1]
    PAD = SUBLANES

    @pl.when(pl.program_id(1) == 0)
    def _():
        xpad_ref[0:PAD, :] = jnp.zeros((PAD, Wd), F32)
        h_ref[...] = jnp.zeros_like(h_ref)

    xpad_ref[PAD:PAD + rows, :] = zr_ref[:, :Wd].astype(F32)
    xc = cb_ref[...] + jnp.zeros((rows, Wd), F32)
    for j in range(CONV_WIDTH):
        off = PAD - (CONV_WIDTH - 1) + j
        xc = xc + cw_ref[j:j + 1, :] * xpad_ref[off:off + rows, :]
    tail = xpad_ref[rows:rows + PAD, :]
    xpad_ref[0:PAD, :] = tail

    gates = _dot(xc.astype(BF16), wg_ref[...]) + bg_ref[...]
    r = _sigmoid(gates[:, :Wd])
    ig = _sigmoid(gates[:, Wd:])
    log_a = (-RG_C) * r * _softplus(-lam_ref[...])
    a = jnp.exp(log_a)
    a_ref[...] = a
    u_ref[...] = jnp.sqrt(1.0 - a * a) * (ig * xc)

    sub = lax.broadcasted_iota(jnp.int32, (SUBLANES, Wd), 0)

    def scan_body(i, hprev):
        r0 = pl.multiple_of(i * SUBLANES, SUBLANES)
        av = a_ref[pl.ds(r0, SUBLANES), :]
        uv = u_ref[pl.ds(r0, SUBLANES), :]
        for d in (1, 2, 4):
            keep = sub >= d
            a_s = jnp.where(keep, pltpu.roll(av, d, 0), 1.0)
            u_s = jnp.where(keep, pltpu.roll(uv, d, 0), 0.0)
            uv = av * u_s + uv
            av = av * a_s
        hv = av * hprev + uv
        u_ref[pl.ds(r0, SUBLANES), :] = hv
        return jnp.broadcast_to(hv[SUBLANES - 1:SUBLANES, :], (SUBLANES, Wd))

    hlast = lax.fori_loop(0, rows // SUBLANES, scan_body, h_ref[...])
    h_ref[...] = hlast

    yv = u_ref[...] * _gelu_tanh(zr_ref[:, Wd:].astype(F32))
    y_ref[...] = (_rms_rows(yv) * gain_ref[...]).astype(BF16)


def _rglru(zr, conv_w, conv_b, w_gates, b_gates, lam, gain, B, S, rows=512):
    T = B * S
    Wd = zr.shape[1] // 2
    rows = min(rows, S)
    steps = S // rows
    kern = functools.partial(_rglru_kernel, rows=rows)
    const = lambda b, s: (0, 0)
    return pl.pallas_call(
        kern,
        grid=(B, steps),
        in_specs=[
            pl.BlockSpec((rows, 2 * Wd), lambda b, s: (b * steps + s, 0)),
            pl.BlockSpec((CONV_WIDTH, Wd), const),
            pl.BlockSpec((1, Wd), const),
            pl.BlockSpec((Wd, 2 * Wd), const),
            pl.BlockSpec((1, 2 * Wd), const),
            pl.BlockSpec((1, Wd), const),
            pl.BlockSpec((1, Wd), const),
        ],
        out_specs=pl.BlockSpec((rows, Wd), lambda b, s: (b * steps + s, 0)),
        out_shape=jax.ShapeDtypeStruct((T, Wd), BF16),
        scratch_shapes=[
            pltpu.VMEM((rows + SUBLANES, Wd), F32),
            pltpu.VMEM((rows, Wd), F32),
            pltpu.VMEM((rows, Wd), F32),
            pltpu.VMEM((SUBLANES, Wd), F32),
        ],
        compiler_params=_cparams("parallel", "arbitrary"),
        name="rglru",
    )(zr, conv_w, conv_b, w_gates, b_gates, lam, gain)


def _outproj_kernel(yml_ref, yrg_ref, h_ref, w_ref, g_ref, *rest):
    half = yml_ref.shape[1]
    hn = (h_ref[...] + _dot(yml_ref[...], w_ref[:half, :]) + _dot(yrg_ref[...], w_ref[half:, :]))
    un = _rms_rows(hn) * g_ref[...]
    if len(rest) == 4:
        wr_ref, hn_ref, un_ref, lg_ref = rest
        wr = wr_ref[...]
        w_hi = wr.astype(BF16)
        w_lo = (wr - w_hi.astype(F32)).astype(BF16)
        u_hi = un.astype(BF16)
        u_lo = (un - u_hi.astype(F32)).astype(BF16)
        lg_ref[...] = _dot(u_hi, w_hi) + (_dot(u_hi, w_lo) + _dot(u_lo, w_hi))
    else:
        hn_ref, un_ref = rest
    hn_ref[...] = hn
    un_ref[...] = un.astype(un_ref.dtype)


def _outproj(yml, yrg, h, w_out, g_ffn, w_router, tm=512):
    T, D = h.shape
    half = yml.shape[1]
    tm = min(tm, T)
    with_router = w_router is not None
    row = lambda i: (i, 0)
    const = lambda i: (0, 0)
    in_specs = [
        pl.BlockSpec((tm, half), row),
        pl.BlockSpec((tm, half), row),
        pl.BlockSpec((tm, D), row),
        pl.BlockSpec((2 * half, D), const),
        pl.BlockSpec((1, D), const),
    ]
    out_specs = [pl.BlockSpec((tm, D), row), pl.BlockSpec((tm, D), row)]
    out_shape = [jax.ShapeDtypeStruct((T, D), F32),
                 jax.ShapeDtypeStruct((T, D), F32 if with_router else BF16)]
    args = [yml, yrg, h, w_out, g_ffn]
    if with_router:
        in_specs.append(pl.BlockSpec((D, LANES), const))
        out_specs.append(pl.BlockSpec((tm, LANES), row))
        out_shape.append(jax.ShapeDtypeStruct((T, LANES), F32))
        args.append(w_router)
    return pl.pallas_call(
        _outproj_kernel,
        grid=(T // tm,),
        in_specs=in_specs,
        out_specs=out_specs,
        out_shape=out_shape,
        compiler_params=_cparams("parallel"),
        name="outproj_router" if with_router else "outproj",
    )(*args)


def _swiglu_up(x, wg_ref, wu_ref, act_ref):
    g = _dot(x, wg_ref[...].astype(BF16))
    u = _dot(x, wu_ref[...].astype(BF16))
    act_ref[...] = (g * _sigmoid(g) * u).astype(BF16)


def _swiglu_down(act_ref, wd_ref, o_ref):
    o_ref[...] += _dot(act_ref[...], wd_ref[...].astype(BF16))


def _swiglu_steps(j, nj, x_fn, init_fn, wg_ref, wu_ref, wd_ref, act_ref, o_ref):
    @pl.when(j == 0)
    def _():
        o_ref[...] = init_fn()
        _swiglu_up(x_fn(), wg_ref, wu_ref, act_ref)

    @pl.when(jnp.logical_and(j > 0, j < nj))
    def _():
        _swiglu_down(act_ref, wd_ref, o_ref)
        _swiglu_up(x_fn(), wg_ref, wu_ref, act_ref)

    @pl.when(j == nj)
    def _():
        _swiglu_down(act_ref, wd_ref, o_ref)


def _ffn_dense_kernel(u_ref, h_ref, wg_ref, wu_ref, wd_ref, o_ref, act_ref):
    j = pl.program_id(1)
    nj = pl.num_programs(1) - 1
    _swiglu_steps(j, nj, lambda: u_ref[...], lambda: h_ref[...], wg_ref, wu_ref, wd_ref, act_ref, o_ref)


def _ffn_dense(un, h, wg, wu, wd, tm=1024, tf=256):
    T, D = un.shape
    F = wg.shape[1]
    tm = min(tm, T)
    tf = min(tf, F)
    nj = F // tf
    up = lambda i, j: (0, jnp.minimum(j, nj - 1))
    down = lambda i, j: (jnp.maximum(j - 1, 0), 0)
    row = lambda i, j: (i, 0)
    return pl.pallas_call(
        _ffn_dense_kernel,
        grid=(T // tm, nj + 1),
        in_specs=[
            pl.BlockSpec((tm, D), row),
            pl.BlockSpec((tm, D), row),
            pl.BlockSpec((D, tf), up),
            pl.BlockSpec((D, tf), up),
            pl.BlockSpec((tf, D), down),
        ],
        out_specs=pl.BlockSpec((tm, D), row),
        out_shape=jax.ShapeDtypeStruct((T, D), F32),
        scratch_shapes=[pltpu.VMEM((tm, tf), BF16)],
        compiler_params=_cparams("parallel", "arbitrary"),
        name="ffn_dense",
    )(un, h, wg, wu, wd)


def _ffn_grouped_kernel(te_ref, nt_ref, x_ref, wg_ref, wu_ref, wd_ref, o_ref, xb_ref, act_ref):
    i = pl.program_id(0)
    j = pl.program_id(1)
    nj = pl.num_programs(1) - 1
    used = i < nt_ref[0]

    @pl.when(jnp.logical_and(used, j == 0))
    def _():
        xb_ref[...] = x_ref[...].astype(BF16)

    @pl.when(used)
    def _():
        _swiglu_steps(j, nj, lambda: xb_ref[...], lambda: jnp.zeros(o_ref.shape, F32),
                      wg_ref.at[0], wu_ref.at[0], wd_ref.at[0], act_ref, o_ref)

    @pl.when(jnp.logical_and(jnp.logical_not(used), j == nj))
    def _():
        o_ref[...] = jnp.zeros_like(o_ref)


def _ffn_grouped(tile_expert, n_tiles, xs, wg, wu, wd, tm, tf=512):
    P, D = xs.shape
    F = wg.shape[2]
    tf = min(tf, F)
    nj = F // tf

    def up(i, j, te, nt):
        return (te[i], 0, jnp.where(i < nt[0], jnp.minimum(j, nj - 1), nj - 1))

    def down(i, j, te, nt):
        return (te[i], jnp.where(i < nt[0], jnp.maximum(j - 1, 0), nj - 1), 0)

    grid_spec = pltpu.PrefetchScalarGridSpec(
        num_scalar_prefetch=2,
        grid=(P // tm, nj + 1),
        in_specs=[
            pl.BlockSpec((tm, D), lambda i, j, te, nt: (jnp.minimum(i, nt[0] - 1), 0)),
            pl.BlockSpec((1, D, tf), up),
            pl.BlockSpec((1, D, tf), up),
            pl.BlockSpec((1, tf, D), down),
        ],
        out_specs=pl.BlockSpec((tm, D), lambda i, j, te, nt: (i, 0)),
        scratch_shapes=[pltpu.VMEM((tm, D), BF16), pltpu.VMEM((tm, tf), BF16)],
    )
    return pl.pallas_call(
        _ffn_grouped_kernel,
        grid_spec=grid_spec,
        out_shape=jax.ShapeDtypeStruct((P, D), F32),
        compiler_params=_cparams("arbitrary", "arbitrary"),
        name="ffn_grouped",
    )(tile_expert, n_tiles, xs, wg, wu, wd)


def _router_kernel(lg_ref, rt_ref, cnt_ref, carry_ref):
    tm = lg_ref.shape[0]

    @pl.when(pl.program_id(0) == 0)
    def _():
        carry_ref[...] = jnp.zeros_like(carry_ref)

    lane = lax.broadcasted_iota(jnp.int32, (tm, LANES), 1).astype(F32)
    lg = jnp.where(lane < N_EXPERTS, lg_ref[...], -jnp.inf)
    v1 = jnp.max(lg, axis=1, keepdims=True)
    e1 = jnp.min(jnp.where(lg == v1, lane, float(LANES)), axis=1, keepdims=True)
    lg2 = jnp.where(lane == e1, -jnp.inf, lg)
    v2 = jnp.max(lg2, axis=1, keepdims=True)
    e2 = jnp.min(jnp.where(lg2 == v2, lane, float(LANES)), axis=1, keepdims=True)
    ex = jnp.exp(v2 - v1)
    w1 = 1.0 / (1.0 + ex)
    w2 = ex / (1.0 + ex)

    oh1 = (lane == e1).astype(F32)
    oh2 = (lane == e2).astype(F32)
    r_i = lax.broadcasted_iota(jnp.int32, (tm, tm), 0)
    c_i = lax.broadcasted_iota(jnp.int32, (tm, tm), 1)
    strict = (c_i < r_i).astype(BF16)
    before = _dot(strict, (oh1 + oh2).astype(BF16)) + carry_ref[0:1, :]
    rank1 = jnp.sum(before * oh1, axis=1, keepdims=True)
    rank2 = jnp.sum(before * oh2, axis=1, keepdims=True)
    total = carry_ref[0:1, :] + jnp.sum(oh1 + oh2, axis=0, keepdims=True)
    carry_ref[...] = jnp.broadcast_to(total, carry_ref.shape)
    cnt_ref[...] = jnp.broadcast_to(total, cnt_ref.shape)

    out = jnp.where(lane == 0, e1, 0.0)
    out = jnp.where(lane == 1, e2, out)
    out = jnp.where(lane == 2, w1, out)
    out = jnp.where(lane == 3, w2, out)
    out = jnp.where(lane == 4, rank1, out)
    out = jnp.where(lane == 5, rank2, out)
    rt_ref[...] = out


def _router(logits, tm=512):
    T = logits.shape[0]
    tm = min(tm, T)
    return pl.pallas_call(
        _router_kernel,
        grid=(T // tm,),
        in_specs=[pl.BlockSpec((tm, LANES), lambda i: (i, 0))],
        out_specs=[
            pl.BlockSpec((tm, LANES), lambda i: (i, 0)),
            pl.BlockSpec((SUBLANES, LANES), lambda i: (0, 0)),
        ],
        out_shape=[
            jax.ShapeDtypeStruct((T, LANES), F32),
            jax.ShapeDtypeStruct((SUBLANES, LANES), F32),
        ],
        scratch_shapes=[pltpu.VMEM((SUBLANES, LANES), F32)],
        compiler_params=_cparams("arbitrary"),
        name="router",
    )(logits)


def _dispatch_kernel(p1_ref, p2_ref, pad0_ref, padn_ref, nt_ref, u_ref, xs_hbm, zbuf_ref, sem, zsem,
                     *, pad_bits, tail_per_tile):
    tm = u_ref.shape[0]

    @pl.when(pl.program_id(0) == 0)
    def _():
        zbuf_ref[...] = jnp.zeros_like(zbuf_ref)

        def pad_copies(e, b):
            n = padn_ref[e]
            off = pad0_ref[e] + (n & ((1 << b) - 1))
            if (1 << b) >= SUBLANES:
                off = pl.multiple_of(off, SUBLANES)
                return [pltpu.make_async_copy(zbuf_ref.at[pl.ds(0, 1 << b)],
                                              xs_hbm.at[pl.ds(off, 1 << b)], zsem)]
            return [pltpu.make_async_copy(zbuf_ref.at[pl.ds(0, 1)], xs_hbm.at[pl.ds(off + r, 1)], zsem)
                    for r in range(1 << b)]

        for wait in (False, True):
            for e in range(N_EXPERTS):
                for b in range(pad_bits):
                    @pl.when(((padn_ref[e] >> b) & 1) == 1)
                    def _():
                        for cp in pad_copies(e, b):
                            if wait:
                                cp.wait()
                            else:
                                cp.start()

        zrows = zbuf_ref.shape[0]

        def tail_copy(t):
            off = pl.multiple_of(t * zrows, SUBLANES)
            return pltpu.make_async_copy(zbuf_ref, xs_hbm.at[pl.ds(off, zrows)], zsem)

        def tail_start(t, carry):
            tail_copy(t).start()
            return carry

        def tail_wait(t, carry):
            tail_copy(t).wait()
            return carry

        first, last = nt_ref[0] * tail_per_tile, xs_hbm.shape[0] // zrows
        lax.fori_loop(first, last, tail_start, 0)
        lax.fori_loop(first, last, tail_wait, 0)

    def row_copy(r, pos):
        return pltpu.make_async_copy(u_ref.at[pl.ds(r, 1)], xs_hbm.at[pl.ds(pos, 1)], sem)

    def issue(r, carry):
        row_copy(r, p1_ref[r]).start()
        row_copy(r, p2_ref[r]).start()
        return carry

    lax.fori_loop(0, tm, issue, 0)
    for _ in range(2):
        pltpu.make_async_copy(u_ref, xs_hbm.at[pl.ds(0, tm)], sem).wait()


def _dispatch(pos1, pos2, pad_start, pad_len, n_tiles, un, P, group_tile, tm=512):
    T, D = un.shape
    tm = min(tm, T)
    pad_bits = group_tile.bit_length() - 1
    assert group_tile == 1 << pad_bits and pad_bits >= 1
    zrows = group_tile // 2
    kern = functools.partial(_dispatch_kernel, pad_bits=pad_bits, tail_per_tile=group_tile // zrows)
    smem = pl.BlockSpec(memory_space=pltpu.SMEM)
    return pl.pallas_call(
        kern,
        grid=(T // tm,),
        in_specs=[
            pl.BlockSpec((tm,), lambda i: (i,), memory_space=pltpu.SMEM),
            pl.BlockSpec((tm,), lambda i: (i,), memory_space=pltpu.SMEM),
            smem, smem, smem,
            pl.BlockSpec((tm, D), lambda i: (i, 0)),
        ],
        out_specs=pl.BlockSpec(memory_space=pl.ANY),
        out_shape=jax.ShapeDtypeStruct((P, D), un.dtype),
        scratch_shapes=[pltpu.VMEM((zrows, D), un.dtype),
                        pltpu.SemaphoreType.DMA(()), pltpu.SemaphoreType.DMA(())],
        compiler_params=_cparams("arbitrary"),
        name="dispatch",
    )(pos1, pos2, pad_start, pad_len, n_tiles, un)


def _combine_kernel(p1_ref, p2_ref, h_ref, rt_ref, g_ref, y_hbm, o_ref, buf_ref, sem, *, final_norm):
    tm = h_ref.shape[0]

    def row_copy(k, r, pos):
        return pltpu.make_async_copy(y_hbm.at[pl.ds(pos, 1)], buf_ref.at[k, pl.ds(r, 1)], sem)

    def issue(r, carry):
        row_copy(0, r, p1_ref[r]).start()
        row_copy(1, r, p2_ref[r]).start()
        return carry

    lax.fori_loop(0, tm, issue, 0)
    for k in range(2):
        pltpu.make_async_copy(y_hbm.at[pl.ds(0, tm)], buf_ref.at[k], sem).wait()

    rt = rt_ref[...]
    out = h_ref[...] + (rt[:, 2:3] * buf_ref[0] + rt[:, 3:4] * buf_ref[1])
    if final_norm:
        out = _rms_rows(out) * g_ref[...]
    o_ref[...] = out


def _combine(pos1, pos2, h, routing, g_final, y, final_norm, tm=256):
    T, D = h.shape
    tm = min(tm, T)
    kern = functools.partial(_combine_kernel, final_norm=final_norm)
    return pl.pallas_call(
        kern,
        grid=(T // tm,),
        in_specs=[
            pl.BlockSpec((tm,), lambda i: (i,), memory_space=pltpu.SMEM),
            pl.BlockSpec((tm,), lambda i: (i,), memory_space=pltpu.SMEM),
            pl.BlockSpec((tm, D), lambda i: (i, 0)),
            pl.BlockSpec((tm, LANES), lambda i: (i, 0)),
            pl.BlockSpec((1, D), lambda i: (0, 0)),
            pl.BlockSpec(memory_space=pl.ANY),
        ],
        out_specs=pl.BlockSpec((tm, D), lambda i: (i, 0)),
        out_shape=jax.ShapeDtypeStruct((T, D), F32),
        scratch_shapes=[pltpu.VMEM((2, tm, D), F32), pltpu.SemaphoreType.DMA(())],
        compiler_params=_cparams("arbitrary"),
        name="combine",
    )(pos1, pos2, h, routing, g_final, y)


def _final_norm_kernel(h_ref, g_ref, o_ref):
    o_ref[...] = _rms_rows(h_ref[...]) * g_ref[...]


def _final_norm(h, g, tm=1024):
    T, D = h.shape
    tm = min(tm, T)
    return pl.pallas_call(
        _final_norm_kernel,
        grid=(T // tm,),
        in_specs=[pl.BlockSpec((tm, D), lambda i: (i, 0)), pl.BlockSpec((1, D), lambda i: (0, 0))],
        out_specs=pl.BlockSpec((tm, D), lambda i: (i, 0)),
        out_shape=jax.ShapeDtypeStruct((T, D), F32),
        compiler_params=_cparams("parallel"),
        name="final_norm",
    )(h, g)


def _block_diag(w):
    G, n, _ = w.shape
    eye = jnp.eye(G, dtype=w.dtype)
    return (eye[:, None, :, None] * w[:, :, None, :]).reshape(G * n, G * n)


def _moe_layer(h, un, logits, w_gate, w_up, w_down, g_final, final_norm, tm):
    T, D = h.shape
    E = N_EXPERTS
    routing, counts = _router(logits)
    e1 = routing[:, 0].astype(jnp.int32)
    e2 = routing[:, 1].astype(jnp.int32)
    rank1 = routing[:, 4].astype(jnp.int32)
    rank2 = routing[:, 5].astype(jnp.int32)
    cnt = counts[0, :E].astype(jnp.int32)
    tiles_per = (cnt + tm - 1) // tm
    tile_end = jnp.cumsum(tiles_per)
    offs = (tile_end - tiles_per) * tm
    pos1 = offs[e1] + rank1
    pos2 = offs[e2] + rank2
    n_slots = (2 * T) // tm + E
    P = n_slots * tm
    tile_expert = jnp.minimum(
        jnp.sum(jnp.arange(n_slots, dtype=jnp.int32)[:, None] >= tile_end[None, :], axis=1), E - 1
    ).astype(jnp.int32)
    n_tiles = tile_end[E - 1:E].astype(jnp.int32)

    xs = _dispatch(pos1, pos2, offs + cnt, tiles_per * tm - cnt, n_tiles, un, P, tm)
    y = _ffn_grouped(tile_expert, n_tiles, xs, w_gate, w_up, w_down, tm)
    return _combine(pos1, pos2, h, routing, g_final, y, final_norm)


def _forward(x, norm_mix_g, w_in, ml_b_if, ml_norm_g, rg_conv_w, rg_conv_b, rg_w_a, rg_b_a,
             rg_w_x, rg_b_x, rg_lam, rg_norm_g, w_out, norm_ffn_g, ffn_w_gate, ffn_w_up,
             ffn_w_down, moe_w_router, moe_w_gate, moe_w_up, moe_w_down, norm_final_g,
             moe_tile=1024):
    B, S, D = x.shape
    T = B * S
    depth = w_in.shape[0]
    ml_w = ML_HEADS * ML_HEAD_DIM
    n_q = 4 * ml_w
    n_if = 2 * ML_HEADS
    rg_w = rg_lam.shape[1]
    n_r = 2 * rg_w

    h = x.reshape(T, D)
    g_final = norm_final_g.reshape(1, D)
    for l in range(depth):
        wl = w_in[l]
        w_all = jnp.concatenate(
            [wl[:, :ml_w], wl[:, 2 * ml_w:n_q], wl[:, n_q + n_if:], wl[:, n_q:n_q + n_if],
             jnp.zeros((D, LANES - n_if), wl.dtype)], axis=1).astype(BF16)
        wkt = wl[:, ml_w:2 * ml_w].T.astype(BF16)
        b_if = jnp.concatenate([ml_b_if[l], jnp.zeros((LANES - n_if,), F32)]).reshape(1, LANES)
        zq, kt, zr, zg = _inproj(h, norm_mix_g[l].reshape(1, D), b_if, w_all, wkt, 3 * ml_w, n_r)

        zgt = zg[:, :SUBLANES].T.reshape(SUBLANES, T // CHUNK, CHUNK).transpose(1, 0, 2)
        yml = _mlstm(zq, kt, zg, zgt, ml_norm_g[l].reshape(1, ml_w), B, S)

        w_gates = jnp.concatenate([_block_diag(rg_w_a[l]), _block_diag(rg_w_x[l])], axis=1).astype(BF16)
        b_gates = jnp.concatenate([rg_b_a[l], rg_b_x[l]]).reshape(1, n_r)
        yrg = _rglru(zr, rg_conv_w[l], rg_conv_b[l].reshape(1, rg_w), w_gates, b_gates,
                     rg_lam[l].reshape(1, rg_w), rg_norm_g[l].reshape(1, rg_w), B, S)

        j = l // 2
        is_moe = (l % 2 == 1)
        w_router = None
        if is_moe:
            w_router = jnp.concatenate(
                [moe_w_router[j], jnp.zeros((D, LANES - N_EXPERTS), F32)], axis=1)
        res = _outproj(yml, yrg, h, w_out[l].astype(BF16), norm_ffn_g[l].reshape(1, D), w_router)
        h, un = res[0], res[1]
        last = (l == depth - 1)
        if is_moe:
            logits = res[2]
            h = _moe_layer(h, un, logits, moe_w_gate[j], moe_w_up[j], moe_w_down[j],
                           g_final, last, moe_tile)
        else:
            h = _ffn_dense(un, h, ffn_w_gate[j], ffn_w_up[j], ffn_w_down[j])
            if last:
                h = _final_norm(h, g_final)
    return h.reshape(B, S, D)


def kernel(x, norm_mix_g, w_in, ml_b_if, ml_norm_g, rg_conv_w, rg_conv_b, rg_w_a, rg_b_a, rg_w_x,
           rg_b_x, rg_lam, rg_norm_g, w_out, norm_ffn_g, ffn_w_gate, ffn_w_up, ffn_w_down,
           moe_w_router, moe_w_gate, moe_w_up, moe_w_down, norm_final_g):
    return _forward(x, norm_mix_g, w_in, ml_b_if, ml_norm_g, rg_conv_w, rg_conv_b, rg_w_a, rg_b_a,
                    rg_w_x, rg_b_x, rg_lam, rg_norm_g, w_out, norm_ffn_g, ffn_w_gate, ffn_w_up,
                    ffn_w_down, moe_w_router, moe_w_gate, moe_w_up, moe_w_down, norm_final_g)
```

```python
import functools

import jax
import jax.numpy as jnp
from jax import lax
from jax.experimental import pallas as pl
from jax.experimental.pallas import tpu as pltpu

EPS = 1e-6
ML_HEADS = 4
ML_HEAD_DIM = 128
CHUNK = 128
RG_BLOCKS = 8
RG_C = 8.0
CONV_WIDTH = 4
N_EXPERTS = 8
LANES = 128
SUBLANES = 8
VMEM_LIMIT = 56 * 1024 * 1024
ISSUE_UNROLL = 8

BF16 = jnp.bfloat16
F32 = jnp.float32


def _cparams(*sem):
    return pltpu.CompilerParams(dimension_semantics=sem, vmem_limit_bytes=VMEM_LIMIT)


def _dot(a, b):
    return jnp.dot(a, b, preferred_element_type=F32)


def _dot_nt(a, b):
    return lax.dot_general(a, b, (((1,), (1,)), ((), ())), preferred_element_type=F32)


def _dot_tn(a, b):
    return lax.dot_general(a, b, (((0,), (0,)), ((), ())), preferred_element_type=F32)


def _dot_f32(a, b):
    return jnp.dot(a, b, preferred_element_type=F32, precision=lax.Precision.HIGHEST)


def _sigmoid(x):
    return 1.0 / (1.0 + jnp.exp(-x))


def _log_sigmoid(x):
    return jnp.minimum(x, 0.0) - jnp.log(1.0 + jnp.exp(-jnp.abs(x)))


def _softplus(x):
    return jnp.maximum(x, 0.0) + jnp.log(1.0 + jnp.exp(-jnp.abs(x)))


def _gelu_tanh(x):
    return 0.5 * x * (1.0 + jnp.tanh(0.7978845608028654 * (x + 0.044715 * (x * x * x))))


def _rms_rows(x):
    return x * lax.rsqrt(jnp.mean(x * x, axis=-1, keepdims=True) + EPS)


def _inproj_kernel(h_ref, g_ref, b_ref, w_ref, wkt_ref, zq_ref, kt_ref, zr_ref, zg_ref,
                   *, n_q, n_r, n_col):
    xn = (_rms_rows(h_ref[...]) * g_ref[...]).astype(BF16)
    for c0 in range(0, n_q, n_col):
        zq_ref[:, c0:c0 + n_col] = _dot(xn, w_ref[:, c0:c0 + n_col]).astype(BF16)
    for c0 in range(0, n_r, n_col):
        zr_ref[:, c0:c0 + n_col] = _dot(xn, w_ref[:, n_q + c0:n_q + c0 + n_col]).astype(BF16)
    zg_ref[...] = _dot(xn, w_ref[:, n_q + n_r:]) + b_ref[...]
    kt = _dot_nt(wkt_ref[...], xn).astype(BF16)
    for c in range(kt_ref.shape[0]):
        kt_ref[c] = kt[:, c * CHUNK:(c + 1) * CHUNK]


def _inproj(h, g, b_if, w_all, wkt, n_q, n_r, tm=512):
    T, D = h.shape
    tm = min(tm, T)
    n_all = w_all.shape[1]
    n_g = n_all - n_q - n_r
    n_k = wkt.shape[0]
    kern = functools.partial(_inproj_kernel, n_q=n_q, n_r=n_r, n_col=512)
    return pl.pallas_call(
        kern,
        grid=(T // tm,),
        in_specs=[
            pl.BlockSpec((tm, D), lambda i: (i, 0)),
            pl.BlockSpec((1, D), lambda i: (0, 0)),
            pl.BlockSpec((1, n_g), lambda i: (0, 0)),
            pl.BlockSpec((D, n_all), lambda i: (0, 0)),
            pl.BlockSpec((n_k, D), lambda i: (0, 0)),
        ],
        out_specs=[
            pl.BlockSpec((tm, n_q), lambda i: (i, 0)),
            pl.BlockSpec((tm // CHUNK, n_k, CHUNK), lambda i: (i, 0, 0)),
            pl.BlockSpec((tm, n_r), lambda i: (i, 0)),
            pl.BlockSpec((tm, n_g), lambda i: (i, 0)),
        ],
        out_shape=[
            jax.ShapeDtypeStruct((T, n_q), BF16),
            jax.ShapeDtypeStruct((T // CHUNK, n_k, CHUNK), BF16),
            jax.ShapeDtypeStruct((T, n_r), BF16),
            jax.ShapeDtypeStruct((T, n_g), F32),
        ],
        compiler_params=_cparams("parallel"),
        name="inproj",
    )(h, g, b_if, w_all, wkt)


def _mlstm_kernel(zq_ref, kt_ref, zg_ref, zgt_ref, gain_ref, y_ref, c_ref, m_ref, *, n_chunks):
    H, Dh, L = ML_HEADS, ML_HEAD_DIM, CHUNK
    W = H * Dh
    scale = Dh ** -0.5

    @pl.when(pl.program_id(1) == 0)
    def _():
        c_ref[...] = jnp.zeros_like(c_ref)
        m_ref[...] = jnp.zeros_like(m_ref)

    row = lax.broadcasted_iota(jnp.int32, (L, L), 0)
    col = lax.broadcasted_iota(jnp.int32, (L, L), 1)
    causal = col <= row
    tri_l = causal.astype(F32)
    tri_u = (row <= col).astype(F32)
    ones_blk = jnp.ones((L, Dh), BF16)

    def chunk_body(c, carry):
        r0 = pl.multiple_of(c * L, L)
        gcol = zg_ref[pl.ds(r0, L), :]
        grow = zgt_ref[c]
        bcol_all = _dot_f32(tri_l, _log_sigmoid(gcol))
        brow_all = _dot_f32(_log_sigmoid(grow), tri_u)
        cm = gcol - pltpu.roll(bcol_all, LANES - H, 1)
        d = 1
        while d < L:
            cm = jnp.maximum(cm, jnp.where(row >= d, pltpu.roll(cm, d, 0), -jnp.inf))
            d *= 2

        for hd in range(H):
            q = zq_ref[pl.ds(r0, L), hd * Dh:(hd + 1) * Dh]
            v = zq_ref[pl.ds(r0, L), W + hd * Dh:W + (hd + 1) * Dh]
            o = zq_ref[pl.ds(r0, L), 2 * W + hd * Dh:2 * W + (hd + 1) * Dh]
            kt = kt_ref[c, hd * Dh:(hd + 1) * Dh, :]
            v_aug = jnp.concatenate([v, ones_blk], axis=1)

            li_row = grow[hd:hd + 1, :]
            b_row = brow_all[H + hd:H + hd + 1, :]
            g_tot = b_row[:, L - 1:L]
            r_row = li_row - b_row

            c_prev = c_ref[hd]
            m_prev = m_ref[hd]

            mx = jnp.maximum(jnp.broadcast_to(cm[:, hd:hd + 1], (L, L)), m_prev)
            bb = jnp.broadcast_to(bcol_all[:, H + hd:H + hd + 1], (L, L))
            p = jnp.exp(jnp.where(causal, r_row - mx, -jnp.inf))
            s_w = p * (_dot(q, kt) * scale)
            inter = jnp.exp(m_prev - mx)
            intra = _dot(s_w.astype(BF16), v_aug)
            cross = _dot(q, c_prev.astype(BF16))
            num = intra[:, :Dh] + inter * cross[:, :Dh]
            den = intra[:, Dh:] + inter * cross[:, Dh:]
            hh = num / jnp.maximum(jnp.abs(den), jnp.exp(-(bb + mx)))
            ht = _rms_rows(hh) * gain_ref[:, hd * Dh:(hd + 1) * Dh]
            y_ref[pl.ds(r0, L), hd * Dh:(hd + 1) * Dh] = (_sigmoid(o.astype(F32)) * ht).astype(BF16)

            a = g_tot + r_row
            m_loc = jnp.max(a, axis=1, keepdims=True)
            w = jnp.exp(a - m_loc) * scale
            c_loc = _dot((kt.astype(F32) * w).astype(BF16), v_aug)
            m_new = jnp.maximum(g_tot + m_prev, m_loc)
            s_old = jnp.exp(g_tot + m_prev - m_new)
            s_loc = jnp.exp(m_loc - m_new)
            c_ref[hd] = s_old * c_prev + s_loc * c_loc
            m_ref[hd] = m_new
        return carry

    lax.fori_loop(0, n_chunks, chunk_body, 0)


def _mlstm(zq, kt, zg, zgt, gain, B, S, rows=1024):
    T = B * S
    rows = min(rows, S)
    n_chunks = rows // CHUNK
    steps = S // rows
    W = ML_HEADS * ML_HEAD_DIM
    kern = functools.partial(_mlstm_kernel, n_chunks=n_chunks)
    return pl.pallas_call(
        kern,
        grid=(B, steps),
        in_specs=[
            pl.BlockSpec((rows, 3 * W), lambda b, s: (b * steps + s, 0)),
            pl.BlockSpec((n_chunks, W, CHUNK), lambda b, s: (b * steps + s, 0, 0)),
            pl.BlockSpec((rows, LANES), lambda b, s: (b * steps + s, 0)),
            pl.BlockSpec((n_chunks, SUBLANES, CHUNK), lambda b, s: (b * steps + s, 0, 0)),
            pl.BlockSpec((1, W), lambda b, s: (0, 0)),
        ],
        out_specs=pl.BlockSpec((rows, W), lambda b, s: (b * steps + s, 0)),
        out_shape=jax.ShapeDtypeStruct((T, W), BF16),
        scratch_shapes=[
            pltpu.VMEM((ML_HEADS, ML_HEAD_DIM, 2 * ML_HEAD_DIM), F32),
            pltpu.VMEM((ML_HEADS, 1, 1), F32),
        ],
        compiler_params=_cparams("parallel", "arbitrary"),
        name="mlstm",
    )(zq, kt, zg, zgt, gain)


def _rglru_kernel(zr_ref, cw_ref, cb_ref, wg_ref, bg_ref, lam_ref, gain_ref, y_ref,
                  xpad_ref, a_ref, u_ref, h_ref, *, rows):
    Wd = y_ref.shape[1]
    PAD = SUBLANES

    @pl.when(pl.program_id(1) == 0)
    def _():
        xpad_ref[0:PAD, :] = jnp.zeros((PAD, Wd), F32)
        h_ref[...] = jnp.zeros_like(h_ref)

    xpad_ref[PAD:PAD + rows, :] = zr_ref[:, :Wd].astype(F32)
    xc = cb_ref[...] + jnp.zeros((rows, Wd), F32)
    for j in range(CONV_WIDTH):
        off = PAD - (CONV_WIDTH - 1) + j
        xc = xc + cw_ref[j:j + 1, :] * xpad_ref[off:off + rows, :]
    tail = xpad_ref[rows:rows + PAD, :]
    xpad_ref[0:PAD, :] = tail

    gates = _dot(xc.astype(BF16), wg_ref[...]) + bg_ref[...]
    r = _sigmoid(gates[:, :Wd])
    ig = _sigmoid(gates[:, Wd:])
    log_a = (-RG_C) * r * _softplus(-lam_ref[...])
    a = jnp.exp(log_a)
    a_ref[...] = a
    u_ref[...] = jnp.sqrt(1.0 - a * a) * (ig * xc)

    sub = lax.broadcasted_iota(jnp.int32, (SUBLANES, Wd), 0)

    def scan_body(i, hprev):
        r0 = pl.multiple_of(i * SUBLANES, SUBLANES)
        av = a_ref[pl.ds(r0, SUBLANES), :]
        uv = u_ref[pl.ds(r0, SUBLANES), :]
        for d in (1, 2, 4):
            keep = sub >= d
            a_s = jnp.where(keep, pltpu.roll(av, d, 0), 1.0)
            u_s = jnp.where(keep, pltpu.roll(uv, d, 0), 0.0)
            uv = av * u_s + uv
            av = av * a_s
        hv = av * hprev + uv
        u_ref[pl.ds(r0, SUBLANES), :] = hv
        return jnp.broadcast_to(hv[SUBLANES - 1:SUBLANES, :], (SUBLANES, Wd))

    hlast = lax.fori_loop(0, rows // SUBLANES, scan_body, h_ref[...])
    h_ref[...] = hlast

    yv = u_ref[...] * _gelu_tanh(zr_ref[:, Wd:].astype(F32))
    y_ref[...] = (_rms_rows(yv) * gain_ref[...]).astype(BF16)


def _rglru(zr, conv_w, conv_b, w_gates, b_gates, lam, gain, B, S, rows=512):
    T = B * S
    Wd = zr.shape[1] // 2
    rows = min(rows, S)
    steps = S // rows
    kern = functools.partial(_rglru_kernel, rows=rows)
    const = lambda b, s: (0, 0)
    return pl.pallas_call(
        kern,
        grid=(B, steps),
        in_specs=[
            pl.BlockSpec((rows, 2 * Wd), lambda b, s: (b * steps + s, 0)),
            pl.BlockSpec((CONV_WIDTH, Wd), const),
            pl.BlockSpec((1, Wd), const),
            pl.BlockSpec((Wd, 2 * Wd), const),
            pl.BlockSpec((1, 2 * Wd), const),
            pl.BlockSpec((1, Wd), const),
            pl.BlockSpec((1, Wd), const),
        ],
        out_specs=pl.BlockSpec((rows, Wd), lambda b, s: (b * steps + s, 0)),
        out_shape=jax.ShapeDtypeStruct((T, Wd), BF16),
        scratch_shapes=[
            pltpu.VMEM((rows + SUBLANES, Wd), F32),
            pltpu.VMEM((rows, Wd), F32),
            pltpu.VMEM((rows, Wd), F32),
            pltpu.VMEM((SUBLANES, Wd), F32),
        ],
        compiler_params=_cparams("parallel", "arbitrary"),
        name="rglru",
    )(zr, conv_w, conv_b, w_gates, b_gates, lam, gain)


def _outproj_kernel(yml_ref, yrg_ref, h_ref, w_ref, g_ref, *rest):
    half = yml_ref.shape[1]
    hn = (h_ref[...] + _dot(yml_ref[...], w_ref[:half, :]) + _dot(yrg_ref[...], w_ref[half:, :]))
    un = _rms_rows(hn) * g_ref[...]
    if len(rest) == 4:
        wr_ref, hn_ref, un_ref, lg_ref = rest
        wr = wr_ref[...]
        w_hi = wr.astype(BF16)
        w_lo = (wr - w_hi.astype(F32)).astype(BF16)
        u_hi = un.astype(BF16)
        u_lo = (un - u_hi.astype(F32)).astype(BF16)
        lg_ref[...] = _dot(u_hi, w_hi) + (_dot(u_hi, w_lo) + _dot(u_lo, w_hi))
    else:
        hn_ref, un_ref = rest
    hn_ref[...] = hn
    un_ref[...] = un.astype(un_ref.dtype)


def _outproj(yml, yrg, h, w_out, g_ffn, w_router, tm=512):
    T, D = h.shape
    half = yml.shape[1]
    tm = min(tm, T)
    with_router = w_router is not None
    row = lambda i: (i, 0)
    const = lambda i: (0, 0)
    in_specs = [
        pl.BlockSpec((tm, half), row),
        pl.BlockSpec((tm, half), row),
        pl.BlockSpec((tm, D), row),
        pl.BlockSpec((2 * half, D), const),
        pl.BlockSpec((1, D), const),
    ]
    out_specs = [pl.BlockSpec((tm, D), row), pl.BlockSpec((tm, D), row)]
    out_shape = [jax.ShapeDtypeStruct((T, D), F32),
                 jax.ShapeDtypeStruct((T, D), F32 if with_router else BF16)]
    args = [yml, yrg, h, w_out, g_ffn]
    if with_router:
        in_specs.append(pl.BlockSpec((D, LANES), const))
        out_specs.append(pl.BlockSpec((tm, LANES), row))
        out_shape.append(jax.ShapeDtypeStruct((T, LANES), F32))
        args.append(w_router)
    return pl.pallas_call(
        _outproj_kernel,
        grid=(T // tm,),
        in_specs=in_specs,
        out_specs=out_specs,
        out_shape=out_shape,
        compiler_params=_cparams("parallel"),
        name="outproj_router" if with_router else "outproj",
    )(*args)


W_SPLIT = 4


def _cast_slabs(parts, dst_ref):
    rows = dst_ref.shape[0] // len(parts)
    for q, part in enumerate(parts):
        dst_ref[q * rows:(q + 1) * rows, :] = part[0].astype(BF16)


def _swiglu_up(x, wg_parts, wu_parts, wgb_ref, wub_ref, act_ref):
    _cast_slabs(wg_parts, wgb_ref)
    _cast_slabs(wu_parts, wub_ref)
    g = _dot(x, wgb_ref[...])
    u = _dot(x, wub_ref[...])
    act_ref[...] = (g * _sigmoid(g) * u).astype(BF16)


def _swiglu_down(act_ref, wd_parts, wdb_ref, o_ref):
    _cast_slabs(wd_parts, wdb_ref)
    o_ref[...] += _dot(act_ref[...], wdb_ref[...])


def _swiglu_steps(j, nj, x_fn, init_fn, weights, scratch, o_ref):
    wg_parts, wu_parts, wd_parts = weights
    act_ref, wgb_ref, wub_ref, wdb_ref = scratch

    @pl.when(j == 0)
    def _():
        o_ref[...] = init_fn()
        _swiglu_up(x_fn(), wg_parts, wu_parts, wgb_ref, wub_ref, act_ref)

    @pl.when(jnp.logical_and(j > 0, j < nj))
    def _():
        _swiglu_down(act_ref, wd_parts, wdb_ref, o_ref)
        _swiglu_up(x_fn(), wg_parts, wu_parts, wgb_ref, wub_ref, act_ref)

    @pl.when(j == nj)
    def _():
        _swiglu_down(act_ref, wd_parts, wdb_ref, o_ref)


def _split_weights(refs):
    q = W_SPLIT
    return refs[:q], refs[q:2 * q], refs[2 * q:3 * q]


def _weight_specs(D, tf, up_map, down_map):
    q = W_SPLIT
    ups = [pl.BlockSpec((1, D // q, tf), functools.partial(up_map, s)) for s in range(q)]
    downs = [pl.BlockSpec((1, tf // q, D), functools.partial(down_map, s)) for s in range(q)]
    return ups + ups + downs


def _swiglu_scratch(tm, D, tf):
    return [pltpu.VMEM((tm, tf), BF16), pltpu.VMEM((D, tf), BF16), pltpu.VMEM((D, tf), BF16),
            pltpu.VMEM((tf, D), BF16)]


def _ffn_dense_kernel(u_ref, h_ref, *refs):
    weights = _split_weights(refs)
    o_ref = refs[3 * W_SPLIT]
    scratch = refs[3 * W_SPLIT + 1:]
    j = pl.program_id(1)
    nj = pl.num_programs(1) - 1
    _swiglu_steps(j, nj, lambda: u_ref[...], lambda: h_ref[...], weights, scratch, o_ref)


def _ffn_dense(un, h, wg, wu, wd, layer, tm=1024, tf=256):
    T, D = un.shape
    F = wg.shape[2]
    tm = min(tm, T)
    tf = min(tf, F)
    nj = F // tf
    q = W_SPLIT
    up = lambda s, i, j: (layer, s, jnp.minimum(j, nj - 1))
    down = lambda s, i, j: (layer, jnp.maximum(j - 1, 0) * q + s, 0)
    row = lambda i, j: (i, 0)
    return pl.pallas_call(
        _ffn_dense_kernel,
        grid=(T // tm, nj + 1),
        in_specs=[pl.BlockSpec((tm, D), row), pl.BlockSpec((tm, D), row)] + _weight_specs(D, tf, up, down),
        out_specs=pl.BlockSpec((tm, D), row),
        out_shape=jax.ShapeDtypeStruct((T, D), F32),
        scratch_shapes=_swiglu_scratch(tm, D, tf),
        compiler_params=_cparams("parallel", "arbitrary"),
        name="ffn_dense",
    )(un, h, *([wg] * q + [wu] * q + [wd] * q))


def _ffn_grouped_kernel(te_ref, nt_ref, x_ref, *refs):
    weights = _split_weights(refs)
    o_ref, xb_ref = refs[3 * W_SPLIT], refs[3 * W_SPLIT + 1]
    scratch = refs[3 * W_SPLIT + 2:]
    i = pl.program_id(0)
    j = pl.program_id(1)
    nj = pl.num_programs(1) - 1
    used = i < nt_ref[0]

    @pl.when(jnp.logical_and(used, j == 0))
    def _():
        xb_ref[...] = x_ref[...].astype(BF16)

    @pl.when(used)
    def _():
        _swiglu_steps(j, nj, lambda: xb_ref[...], lambda: jnp.zeros(o_ref.shape, F32),
                      weights, scratch, o_ref)

    @pl.when(jnp.logical_and(jnp.logical_not(used), j == nj))
    def _():
        o_ref[...] = jnp.zeros_like(o_ref)


def _ffn_grouped(tile_expert, n_tiles, xs, wg, wu, wd, first_expert, tm, tf=512):
    P, D = xs.shape
    F = wg.shape[2]
    tf = min(tf, F)
    nj = F // tf
    q = W_SPLIT

    def up(s, i, j, te, nt):
        return (first_expert + te[i], s, jnp.where(i < nt[0], jnp.minimum(j, nj - 1), nj - 1))

    def down(s, i, j, te, nt):
        return (first_expert + te[i], jnp.where(i < nt[0], jnp.maximum(j - 1, 0), nj - 1) * q + s, 0)

    grid_spec = pltpu.PrefetchScalarGridSpec(
        num_scalar_prefetch=2,
        grid=(P // tm, nj + 1),
        in_specs=[pl.BlockSpec((tm, D), lambda i, j, te, nt: (jnp.minimum(i, nt[0] - 1), 0))]
        + _weight_specs(D, tf, up, down),
        out_specs=pl.BlockSpec((tm, D), lambda i, j, te, nt: (i, 0)),
        scratch_shapes=[pltpu.VMEM((tm, D), BF16)] + _swiglu_scratch(tm, D, tf),
    )
    return pl.pallas_call(
        _ffn_grouped_kernel,
        grid_spec=grid_spec,
        out_shape=jax.ShapeDtypeStruct((P, D), F32),
        compiler_params=_cparams("arbitrary", "arbitrary"),
        name="ffn_grouped",
    )(tile_expert, n_tiles, xs, *([wg] * q + [wu] * q + [wd] * q))


def _router_kernel(lg_ref, rt_ref, cnt_ref, carry_ref):
    tm = lg_ref.shape[0]

    @pl.when(pl.program_id(0) == 0)
    def _():
        carry_ref[...] = jnp.zeros_like(carry_ref)

    lane = lax.broadcasted_iota(jnp.int32, (tm, LANES), 1).astype(F32)
    lg = jnp.where(lane < N_EXPERTS, lg_ref[...], -jnp.inf)
    v1 = jnp.max(lg, axis=1, keepdims=True)
    e1 = jnp.min(jnp.where(lg == v1, lane, float(LANES)), axis=1, keepdims=True)
    lg2 = jnp.where(lane == e1, -jnp.inf, lg)
    v2 = jnp.max(lg2, axis=1, keepdims=True)
    e2 = jnp.min(jnp.where(lg2 == v2, lane, float(LANES)), axis=1, keepdims=True)
    ex = jnp.exp(v2 - v1)
    w1 = 1.0 / (1.0 + ex)
    w2 = ex / (1.0 + ex)

    oh1 = (lane == e1).astype(F32)
    oh2 = (lane == e2).astype(F32)
    r_i = lax.broadcasted_iota(jnp.int32, (tm, tm), 0)
    c_i = lax.broadcasted_iota(jnp.int32, (tm, tm), 1)
    strict = (c_i < r_i).astype(BF16)
    before = _dot(strict, (oh1 + oh2).astype(BF16)) + carry_ref[0:1, :]
    rank1 = jnp.sum(before * oh1, axis=1, keepdims=True)
    rank2 = jnp.sum(before * oh2, axis=1, keepdims=True)
    total = carry_ref[0:1, :] + jnp.sum(oh1 + oh2, axis=0, keepdims=True)
    carry_ref[...] = jnp.broadcast_to(total, carry_ref.shape)
    cnt_ref[...] = jnp.broadcast_to(total, cnt_ref.shape)

    out = jnp.where(lane == 0, e1, 0.0)
    out = jnp.where(lane == 1, e2, out)
    out = jnp.where(lane == 2, w1, out)
    out = jnp.where(lane == 3, w2, out)
    out = jnp.where(lane == 4, rank1, out)
    out = jnp.where(lane == 5, rank2, out)
    rt_ref[...] = out


def _router(logits, tm=512):
    T = logits.shape[0]
    tm = min(tm, T)
    return pl.pallas_call(
        _router_kernel,
        grid=(T // tm,),
        in_specs=[pl.BlockSpec((tm, LANES), lambda i: (i, 0))],
        out_specs=[
            pl.BlockSpec((tm, LANES), lambda i: (i, 0)),
            pl.BlockSpec((SUBLANES, LANES), lambda i: (0, 0)),
        ],
        out_shape=[
            jax.ShapeDtypeStruct((T, LANES), F32),
            jax.ShapeDtypeStruct((SUBLANES, LANES), F32),
        ],
        scratch_shapes=[pltpu.VMEM((SUBLANES, LANES), F32)],
        compiler_params=_cparams("arbitrary"),
        name="router",
    )(logits)


def _dispatch_kernel(p1_ref, p2_ref, pad0_ref, padn_ref, nt_ref, u_ref, xs_hbm, zbuf_ref, sem, zsem,
                     *, pad_bits, tail_per_tile):
    tm = u_ref.shape[0]

    @pl.when(pl.program_id(0) == 0)
    def _():
        zbuf_ref[...] = jnp.zeros_like(zbuf_ref)

        def pad_copies(e, b):
            n = padn_ref[e]
            off = pad0_ref[e] + (n & ((1 << b) - 1))
            if (1 << b) >= SUBLANES:
                off = pl.multiple_of(off, SUBLANES)
                return [pltpu.make_async_copy(zbuf_ref.at[pl.ds(0, 1 << b)],
                                              xs_hbm.at[pl.ds(off, 1 << b)], zsem)]
            return [pltpu.make_async_copy(zbuf_ref.at[pl.ds(0, 1)], xs_hbm.at[pl.ds(off + r, 1)], zsem)
                    for r in range(1 << b)]

        for wait in (False, True):
            for e in range(N_EXPERTS):
                for b in range(pad_bits):
                    @pl.when(((padn_ref[e] >> b) & 1) == 1)
                    def _():
                        for cp in pad_copies(e, b):
                            if wait:
                                cp.wait()
                            else:
                                cp.start()

        zrows = zbuf_ref.shape[0]

        def tail_copy(t):
            off = pl.multiple_of(t * zrows, SUBLANES)
            return pltpu.make_async_copy(zbuf_ref, xs_hbm.at[pl.ds(off, zrows)], zsem)

        def tail_start(t, carry):
            tail_copy(t).start()
            return carry

        def tail_wait(t, carry):
            tail_copy(t).wait()
            return carry

        first, last = nt_ref[0] * tail_per_tile, xs_hbm.shape[0] // zrows
        lax.fori_loop(first, last, tail_start, 0)
        lax.fori_loop(first, last, tail_wait, 0)

    def row_copy(r, pos):
        return pltpu.make_async_copy(u_ref.at[pl.ds(r, 1)], xs_hbm.at[pl.ds(pos, 1)], sem)

    def issue(b, carry):
        for rr in range(ISSUE_UNROLL):
            r = b * ISSUE_UNROLL + rr
            row_copy(r, p1_ref[r]).start()
            row_copy(r, p2_ref[r]).start()
        return carry

    lax.fori_loop(0, tm // ISSUE_UNROLL, issue, 0)
    for _ in range(2):
        pltpu.make_async_copy(u_ref, xs_hbm.at[pl.ds(0, tm)], sem).wait()


def _dispatch(pos1, pos2, pad_start, pad_len, n_tiles, un, P, group_tile, tm=512):
    T, D = un.shape
    tm = min(tm, T)
    pad_bits = group_tile.bit_length() - 1
    assert group_tile == 1 << pad_bits and pad_bits >= 1
    zrows = group_tile // 2
    kern = functools.partial(_dispatch_kernel, pad_bits=pad_bits, tail_per_tile=group_tile // zrows)
    smem = pl.BlockSpec(memory_space=pltpu.SMEM)
    return pl.pallas_call(
        kern,
        grid=(T // tm,),
        in_specs=[
            pl.BlockSpec((tm,), lambda i: (i,), memory_space=pltpu.SMEM),
            pl.BlockSpec((tm,), lambda i: (i,), memory_space=pltpu.SMEM),
            smem, smem, smem,
            pl.BlockSpec((tm, D), lambda i: (i, 0)),
        ],
        out_specs=pl.BlockSpec(memory_space=pl.ANY),
        out_shape=jax.ShapeDtypeStruct((P, D), un.dtype),
        scratch_shapes=[pltpu.VMEM((zrows, D), un.dtype),
                        pltpu.SemaphoreType.DMA(()), pltpu.SemaphoreType.DMA(())],
        compiler_params=_cparams("arbitrary"),
        name="dispatch",
    )(pos1, pos2, pad_start, pad_len, n_tiles, un)


def _combine_kernel(p1_ref, p2_ref, q1_ref, q2_ref, h_ref, rt_ref, g_ref, y_hbm, o_ref, buf_ref, sem,
                    *, final_norm):
    tm = h_ref.shape[0]
    i = pl.program_id(0)
    n = pl.num_programs(0)
    slot = i % 2

    def issue_tile(i1_ref, i2_ref, s):
        def row_copy(k, r, pos):
            return pltpu.make_async_copy(y_hbm.at[pl.ds(pos, 1)], buf_ref.at[s, k, pl.ds(r, 1)], sem.at[s])

        def issue(b, carry):
            for rr in range(ISSUE_UNROLL):
                r = b * ISSUE_UNROLL + rr
                row_copy(0, r, i1_ref[r]).start()
                row_copy(1, r, i2_ref[r]).start()
            return carry

        lax.fori_loop(0, tm // ISSUE_UNROLL, issue, 0)

    @pl.when(i == 0)
    def _():
        issue_tile(p1_ref, p2_ref, 0)

    @pl.when(i + 1 < n)
    def _():
        issue_tile(q1_ref, q2_ref, 1 - slot)

    for k in range(2):
        pltpu.make_async_copy(y_hbm.at[pl.ds(0, tm)], buf_ref.at[slot, k], sem.at[slot]).wait()

    rt = rt_ref[...]
    out = h_ref[...] + (rt[:, 2:3] * buf_ref[slot, 0] + rt[:, 3:4] * buf_ref[slot, 1])
    if final_norm:
        out = _rms_rows(out) * g_ref[...]
    o_ref[...] = out


def _combine(pos1, pos2, h, routing, g_final, y, final_norm, tm=256):
    T, D = h.shape
    tm = min(tm, T)
    n = T // tm
    kern = functools.partial(_combine_kernel, final_norm=final_norm)
    cur = pl.BlockSpec((tm,), lambda i: (i,), memory_space=pltpu.SMEM)
    nxt = pl.BlockSpec((tm,), lambda i: (jnp.minimum(i + 1, n - 1),), memory_space=pltpu.SMEM)
    return pl.pallas_call(
        kern,
        grid=(n,),
        in_specs=[
            cur, cur, nxt, nxt,
            pl.BlockSpec((tm, D), lambda i: (i, 0)),
            pl.BlockSpec((tm, LANES), lambda i: (i, 0)),
            pl.BlockSpec((1, D), lambda i: (0, 0)),
            pl.BlockSpec(memory_space=pl.ANY),
        ],
        out_specs=pl.BlockSpec((tm, D), lambda i: (i, 0)),
        out_shape=jax.ShapeDtypeStruct((T, D), F32),
        scratch_shapes=[pltpu.VMEM((2, 2, tm, D), F32), pltpu.SemaphoreType.DMA((2,))],
        compiler_params=_cparams("arbitrary"),
        name="combine",
    )(pos1, pos2, pos1, pos2, h, routing, g_final, y)


def _final_norm_kernel(h_ref, g_ref, o_ref):
    o_ref[...] = _rms_rows(h_ref[...]) * g_ref[...]


def _final_norm(h, g, tm=1024):
    T, D = h.shape
    tm = min(tm, T)
    return pl.pallas_call(
        _final_norm_kernel,
        grid=(T // tm,),
        in_specs=[pl.BlockSpec((tm, D), lambda i: (i, 0)), pl.BlockSpec((1, D), lambda i: (0, 0))],
        out_specs=pl.BlockSpec((tm, D), lambda i: (i, 0)),
        out_shape=jax.ShapeDtypeStruct((T, D), F32),
        compiler_params=_cparams("parallel"),
        name="final_norm",
    )(h, g)


def _block_diag(w):
    G, n, _ = w.shape
    eye = jnp.eye(G, dtype=w.dtype)
    return (eye[:, None, :, None] * w[:, :, None, :]).reshape(G * n, G * n)


def _moe_layer(h, un, logits, w_gate, w_up, w_down, first_expert, g_final, final_norm, tm):
    T, D = h.shape
    E = N_EXPERTS
    routing, counts = _router(logits)
    e1 = routing[:, 0].astype(jnp.int32)
    e2 = routing[:, 1].astype(jnp.int32)
    rank1 = routing[:, 4].astype(jnp.int32)
    rank2 = routing[:, 5].astype(jnp.int32)
    cnt = counts[0, :E].astype(jnp.int32)
    tiles_per = (cnt + tm - 1) // tm
    tile_end = jnp.cumsum(tiles_per)
    offs = (tile_end - tiles_per) * tm
    pos1 = offs[e1] + rank1
    pos2 = offs[e2] + rank2
    n_slots = (2 * T) // tm + E
    P = n_slots * tm
    tile_expert = jnp.minimum(
        jnp.sum(jnp.arange(n_slots, dtype=jnp.int32)[:, None] >= tile_end[None, :], axis=1), E - 1
    ).astype(jnp.int32)
    n_tiles = tile_end[E - 1:E].astype(jnp.int32)

    xs = _dispatch(pos1, pos2, offs + cnt, tiles_per * tm - cnt, n_tiles, un, P, tm)
    y = _ffn_grouped(tile_expert, n_tiles, xs, w_gate, w_up, w_down, first_expert, tm)
    return _combine(pos1, pos2, h, routing, g_final, y, final_norm)


def _forward(x, norm_mix_g, w_in, ml_b_if, ml_norm_g, rg_conv_w, rg_conv_b, rg_w_a, rg_b_a,
             rg_w_x, rg_b_x, rg_lam, rg_norm_g, w_out, norm_ffn_g, ffn_w_gate, ffn_w_up,
             ffn_w_down, moe_w_router, moe_w_gate, moe_w_up, moe_w_down, norm_final_g,
             moe_tile=1024):
    B, S, D = x.shape
    T = B * S
    depth = w_in.shape[0]
    ml_w = ML_HEADS * ML_HEAD_DIM
    n_q = 4 * ml_w
    n_if = 2 * ML_HEADS
    rg_w = rg_lam.shape[1]
    n_r = 2 * rg_w

    h = x.reshape(T, D)
    g_final = norm_final_g.reshape(1, D)
    moe_gate_all = moe_w_gate.reshape((-1,) + moe_w_gate.shape[2:])
    moe_up_all = moe_w_up.reshape((-1,) + moe_w_up.shape[2:])
    moe_down_all = moe_w_down.reshape((-1,) + moe_w_down.shape[2:])
    for l in range(depth):
        wl = w_in[l]
        w_all = jnp.concatenate(
            [wl[:, :ml_w], wl[:, 2 * ml_w:n_q], wl[:, n_q + n_if:], wl[:, n_q:n_q + n_if],
             jnp.zeros((D, LANES - n_if), wl.dtype)], axis=1).astype(BF16)
        wkt = wl[:, ml_w:2 * ml_w].T.astype(BF16)
        b_if = jnp.concatenate([ml_b_if[l], jnp.zeros((LANES - n_if,), F32)]).reshape(1, LANES)
        zq, kt, zr, zg = _inproj(h, norm_mix_g[l].reshape(1, D), b_if, w_all, wkt, 3 * ml_w, n_r)

        zgt = zg[:, :SUBLANES].T.reshape(SUBLANES, T // CHUNK, CHUNK).transpose(1, 0, 2)
        yml = _mlstm(zq, kt, zg, zgt, ml_norm_g[l].reshape(1, ml_w), B, S)

        w_gates = jnp.concatenate([_block_diag(rg_w_a[l]), _block_diag(rg_w_x[l])], axis=1).astype(BF16)
        b_gates = jnp.concatenate([rg_b_a[l], rg_b_x[l]]).reshape(1, n_r)
        yrg = _rglru(zr, rg_conv_w[l], rg_conv_b[l].reshape(1, rg_w), w_gates, b_gates,
                     rg_lam[l].reshape(1, rg_w), rg_norm_g[l].reshape(1, rg_w), B, S)

        j = l // 2
        is_moe = (l % 2 == 1)
        w_router = None
        if is_moe:
            w_router = jnp.concatenate(
                [moe_w_router[j], jnp.zeros((D, LANES - N_EXPERTS), F32)], axis=1)
        res = _outproj(yml, yrg, h, w_out[l].astype(BF16), norm_ffn_g[l].reshape(1, D), w_router)
        h, un = res[0], res[1]
        last = (l == depth - 1)
        if is_moe:
            logits = res[2]
            h = _moe_layer(h, un, logits, moe_gate_all, moe_up_all, moe_down_all, j * N_EXPERTS,
                           g_final, last, moe_tile)
        else:
            h = _ffn_dense(un, h, ffn_w_gate, ffn_w_up, ffn_w_down, j)
            if last:
                h = _final_norm(h, g_final)
    return h.reshape(B, S, D)


def kernel(x, norm_mix_g, w_in, ml_b_if, ml_norm_g, rg_conv_w, rg_conv_b, rg_w_a, rg_b_a, rg_w_x,
           rg_b_x, rg_lam, rg_norm_g, w_out, norm_ffn_g, ffn_w_gate, ffn_w_up, ffn_w_down,
           moe_w_router, moe_w_gate, moe_w_up, moe_w_down, norm_final_g):
    return _forward(x, norm_mix_g, w_in, ml_b_if, ml_norm_g, rg_conv_w, rg_conv_b, rg_w_a, rg_b_a,
                    rg_w_x, rg_b_x, rg_lam, rg_norm_g, w_out, norm_ffn_g, ffn_w_gate, ffn_w_up,
                    ffn_w_down, moe_w_router, moe_w_gate, moe_w_up, moe_w_down, norm_final_g)
```

```python
import functools

import jax
import jax.numpy as jnp
from jax import lax
from jax.experimental import pallas as pl
from jax.experimental.pallas import tpu as pltpu

EPS = 1e-6
ML_HEADS = 4
ML_HEAD_DIM = 128
CHUNK = 128
RG_BLOCKS = 8
RG_C = 8.0
CONV_WIDTH = 4
N_EXPERTS = 8
LANES = 128
SUBLANES = 8
VMEM_LIMIT = 56 * 1024 * 1024
ISSUE_UNROLL = 8

BF16 = jnp.bfloat16
F32 = jnp.float32


def _cparams(*sem):
    return pltpu.CompilerParams(dimension_semantics=sem, vmem_limit_bytes=VMEM_LIMIT)


def _dot(a, b):
    return jnp.dot(a, b, preferred_element_type=F32)


def _dot_nt(a, b):
    return lax.dot_general(a, b, (((1,), (1,)), ((), ())), preferred_element_type=F32)


def _dot_tn(a, b):
    return lax.dot_general(a, b, (((0,), (0,)), ((), ())), preferred_element_type=F32)


def _dot_f32(a, b):
    return jnp.dot(a, b, preferred_element_type=F32, precision=lax.Precision.HIGHEST)


def _sigmoid(x):
    return 1.0 / (1.0 + jnp.exp(-x))


def _log_sigmoid(x):
    return jnp.minimum(x, 0.0) - jnp.log(1.0 + jnp.exp(-jnp.abs(x)))


def _softplus(x):
    return jnp.maximum(x, 0.0) + jnp.log(1.0 + jnp.exp(-jnp.abs(x)))


def _gelu_tanh(x):
    return 0.5 * x * (1.0 + jnp.tanh(0.7978845608028654 * (x + 0.044715 * (x * x * x))))


def _rms_rows(x):
    return x * lax.rsqrt(jnp.mean(x * x, axis=-1, keepdims=True) + EPS)


def _inproj_kernel(h_ref, g_ref, b_ref, w_ref, wkt_ref, zq_ref, kt_ref, zr_ref, zg_ref,
                   *, n_q, n_r, n_col):
    xn = (_rms_rows(h_ref[...]) * g_ref[...]).astype(BF16)
    for c0 in range(0, n_q, n_col):
        zq_ref[:, c0:c0 + n_col] = _dot(xn, w_ref[:, c0:c0 + n_col]).astype(BF16)
    for c0 in range(0, n_r, n_col):
        zr_ref[:, c0:c0 + n_col] = _dot(xn, w_ref[:, n_q + c0:n_q + c0 + n_col]).astype(BF16)
    zg_ref[...] = _dot(xn, w_ref[:, n_q + n_r:]) + b_ref[...]
    kt = _dot_nt(wkt_ref[...], xn).astype(BF16)
    for c in range(kt_ref.shape[0]):
        kt_ref[c] = kt[:, c * CHUNK:(c + 1) * CHUNK]


def _inproj(h, g, b_if, w_all, wkt, n_q, n_r, tm=512):
    T, D = h.shape
    tm = min(tm, T)
    n_all = w_all.shape[1]
    n_g = n_all - n_q - n_r
    n_k = wkt.shape[0]
    kern = functools.partial(_inproj_kernel, n_q=n_q, n_r=n_r, n_col=512)
    return pl.pallas_call(
        kern,
        grid=(T // tm,),
        in_specs=[
            pl.BlockSpec((tm, D), lambda i: (i, 0)),
            pl.BlockSpec((1, D), lambda i: (0, 0)),
            pl.BlockSpec((1, n_g), lambda i: (0, 0)),
            pl.BlockSpec((D, n_all), lambda i: (0, 0)),
            pl.BlockSpec((n_k, D), lambda i: (0, 0)),
        ],
        out_specs=[
            pl.BlockSpec((tm, n_q), lambda i: (i, 0)),
            pl.BlockSpec((tm // CHUNK, n_k, CHUNK), lambda i: (i, 0, 0)),
            pl.BlockSpec((tm, n_r), lambda i: (i, 0)),
            pl.BlockSpec((tm, n_g), lambda i: (i, 0)),
        ],
        out_shape=[
            jax.ShapeDtypeStruct((T, n_q), BF16),
            jax.ShapeDtypeStruct((T // CHUNK, n_k, CHUNK), BF16),
            jax.ShapeDtypeStruct((T, n_r), BF16),
            jax.ShapeDtypeStruct((T, n_g), F32),
        ],
        compiler_params=_cparams("parallel"),
        name="inproj",
    )(h, g, b_if, w_all, wkt)


def _mlstm_kernel(zq_ref, kt_ref, zg_ref, zgt_ref, gain_ref, y_ref, c_ref, m_ref, *, n_chunks):
    H, Dh, L = ML_HEADS, ML_HEAD_DIM, CHUNK
    W = H * Dh
    scale = Dh ** -0.5

    @pl.when(pl.program_id(1) == 0)
    def _():
        c_ref[...] = jnp.zeros_like(c_ref)
        m_ref[...] = jnp.zeros_like(m_ref)

    row = lax.broadcasted_iota(jnp.int32, (L, L), 0)
    col = lax.broadcasted_iota(jnp.int32, (L, L), 1)
    causal = col <= row
    tri_l = causal.astype(F32)
    tri_u = (row <= col).astype(F32)
    ones_blk = jnp.ones((L, Dh), BF16)

    def chunk_body(c, carry):
        r0 = pl.multiple_of(c * L, L)
        gcol = zg_ref[pl.ds(r0, L), :]
        grow = zgt_ref[c]
        bcol_all = _dot_f32(tri_l, _log_sigmoid(gcol))
        brow_all = _dot_f32(_log_sigmoid(grow), tri_u)
        cm = gcol - pltpu.roll(bcol_all, LANES - H, 1)
        d = 1
        while d < L:
            cm = jnp.maximum(cm, jnp.where(row >= d, pltpu.roll(cm, d, 0), -jnp.inf))
            d *= 2

        for hd in range(H):
            q = zq_ref[pl.ds(r0, L), hd * Dh:(hd + 1) * Dh]
            v = zq_ref[pl.ds(r0, L), W + hd * Dh:W + (hd + 1) * Dh]
            o = zq_ref[pl.ds(r0, L), 2 * W + hd * Dh:2 * W + (hd + 1) * Dh]
            kt = kt_ref[c, hd * Dh:(hd + 1) * Dh, :]
            v_aug = jnp.concatenate([v, ones_blk], axis=1)

            li_row = grow[hd:hd + 1, :]
            b_row = brow_all[H + hd:H + hd + 1, :]
            g_tot = b_row[:, L - 1:L]
            r_row = li_row - b_row

            c_prev = c_ref[hd]
            m_prev = m_ref[hd]

            mx = jnp.maximum(jnp.broadcast_to(cm[:, hd:hd + 1], (L, L)), m_prev)
            bb = jnp.broadcast_to(bcol_all[:, H + hd:H + hd + 1], (L, L))
            p = jnp.exp(jnp.where(causal, r_row - mx, -jnp.inf))
            s_w = p * (_dot(q, kt) * scale)
            inter = jnp.exp(m_prev - mx)
            intra = _dot(s_w.astype(BF16), v_aug)
            cross = _dot(q, c_prev.astype(BF16))
            num = intra[:, :Dh] + inter * cross[:, :Dh]
            den = intra[:, Dh:] + inter * cross[:, Dh:]
            hh = num / jnp.maximum(jnp.abs(den), jnp.exp(-(bb + mx)))
            ht = _rms_rows(hh) * gain_ref[:, hd * Dh:(hd + 1) * Dh]
            y_ref[pl.ds(r0, L), hd * Dh:(hd + 1) * Dh] = (_sigmoid(o.astype(F32)) * ht).astype(BF16)

            a = g_tot + r_row
            m_loc = jnp.max(a, axis=1, keepdims=True)
            w = jnp.exp(a - m_loc) * scale
            c_loc = _dot((kt.astype(F32) * w).astype(BF16), v_aug)
            m_new = jnp.maximum(g_tot + m_prev, m_loc)
            s_old = jnp.exp(g_tot + m_prev - m_new)
            s_loc = jnp.exp(m_loc - m_new)
            c_ref[hd] = s_old * c_prev + s_loc * c_loc
            m_ref[hd] = m_new
        return carry

    lax.fori_loop(0, n_chunks, chunk_body, 0)


def _mlstm(zq, kt, zg, zgt, gain, B, S, rows=1024):
    T = B * S
    rows = min(rows, S)
    n_chunks = rows // CHUNK
    steps = S // rows
    W = ML_HEADS * ML_HEAD_DIM
    kern = functools.partial(_mlstm_kernel, n_chunks=n_chunks)
    return pl.pallas_call(
        kern,
        grid=(B, steps),
        in_specs=[
            pl.BlockSpec((rows, 3 * W), lambda b, s: (b * steps + s, 0)),
            pl.BlockSpec((n_chunks, W, CHUNK), lambda b, s: (b * steps + s, 0, 0)),
            pl.BlockSpec((rows, LANES), lambda b, s: (b * steps + s, 0)),
            pl.BlockSpec((n_chunks, SUBLANES, CHUNK), lambda b, s: (b * steps + s, 0, 0)),
            pl.BlockSpec((1, W), lambda b, s: (0, 0)),
        ],
        out_specs=pl.BlockSpec((rows, W), lambda b, s: (b * steps + s, 0)),
        out_shape=jax.ShapeDtypeStruct((T, W), BF16),
        scratch_shapes=[
            pltpu.VMEM((ML_HEADS, ML_HEAD_DIM, 2 * ML_HEAD_DIM), F32),
            pltpu.VMEM((ML_HEADS, 1, 1), F32),
        ],
        compiler_params=_cparams("parallel", "arbitrary"),
        name="mlstm",
    )(zq, kt, zg, zgt, gain)


def _rglru_kernel(zr_ref, cw_ref, cb_ref, wg_ref, bg_ref, lam_ref, gain_ref, y_ref,
                  xpad_ref, a_ref, u_ref, h_ref, *, rows):
    Wd = y_ref.shape[1]
    PAD = SUBLANES

    @pl.when(pl.program_id(1) == 0)
    def _():
        xpad_ref[0:PAD, :] = jnp.zeros((PAD, Wd), F32)
        h_ref[...] = jnp.zeros_like(h_ref)

    xpad_ref[PAD:PAD + rows, :] = zr_ref[:, :Wd].astype(F32)
    xc = cb_ref[...] + jnp.zeros((rows, Wd), F32)
    for j in range(CONV_WIDTH):
        off = PAD - (CONV_WIDTH - 1) + j
        xc = xc + cw_ref[j:j + 1, :] * xpad_ref[off:off + rows, :]
    tail = xpad_ref[rows:rows + PAD, :]
    xpad_ref[0:PAD, :] = tail

    gates = _dot(xc.astype(BF16), wg_ref[...]) + bg_ref[...]
    r = _sigmoid(gates[:, :Wd])
    ig = _sigmoid(gates[:, Wd:])
    log_a = (-RG_C) * r * _softplus(-lam_ref[...])
    a = jnp.exp(log_a)
    a_ref[...] = a
    u_ref[...] = jnp.sqrt(1.0 - a * a) * (ig * xc)

    sub = lax.broadcasted_iota(jnp.int32, (SUBLANES, Wd), 0)

    def scan_body(i, hprev):
        r0 = pl.multiple_of(i * SUBLANES, SUBLANES)
        av = a_ref[pl.ds(r0, SUBLANES), :]
        uv = u_ref[pl.ds(r0, SUBLANES), :]
        for d in (1, 2, 4):
            keep = sub >= d
            a_s = jnp.where(keep, pltpu.roll(av, d, 0), 1.0)
            u_s = jnp.where(keep, pltpu.roll(uv, d, 0), 0.0)
            uv = av * u_s + uv
            av = av * a_s
        hv = av * hprev + uv
        u_ref[pl.ds(r0, SUBLANES), :] = hv
        return jnp.broadcast_to(hv[SUBLANES - 1:SUBLANES, :], (SUBLANES, Wd))

    hlast = lax.fori_loop(0, rows // SUBLANES, scan_body, h_ref[...])
    h_ref[...] = hlast

    yv = u_ref[...] * _gelu_tanh(zr_ref[:, Wd:].astype(F32))
    y_ref[...] = (_rms_rows(yv) * gain_ref[...]).astype(BF16)


def _rglru(zr, conv_w, conv_b, w_gates, b_gates, lam, gain, B, S, rows=512):
    T = B * S
    Wd = zr.shape[1] // 2
    rows = min(rows, S)
    steps = S // rows
    kern = functools.partial(_rglru_kernel, rows=rows)
    const = lambda b, s: (0, 0)
    return pl.pallas_call(
        kern,
        grid=(B, steps),
        in_specs=[
            pl.BlockSpec((rows, 2 * Wd), lambda b, s: (b * steps + s, 0)),
            pl.BlockSpec((CONV_WIDTH, Wd), const),
            pl.BlockSpec((1, Wd), const),
            pl.BlockSpec((Wd, 2 * Wd), const),
            pl.BlockSpec((1, 2 * Wd), const),
            pl.BlockSpec((1, Wd), const),
            pl.BlockSpec((1, Wd), const),
        ],
        out_specs=pl.BlockSpec((rows, Wd), lambda b, s: (b * steps + s, 0)),
        out_shape=jax.ShapeDtypeStruct((T, Wd), BF16),
        scratch_shapes=[
            pltpu.VMEM((rows + SUBLANES, Wd), F32),
            pltpu.VMEM((rows, Wd), F32),
            pltpu.VMEM((rows, Wd), F32),
            pltpu.VMEM((SUBLANES, Wd), F32),
        ],
        compiler_params=_cparams("parallel", "arbitrary"),
        name="rglru",
    )(zr, conv_w, conv_b, w_gates, b_gates, lam, gain)


def _outproj_kernel(yml_ref, yrg_ref, h_ref, w_ref, g_ref, *rest):
    half = yml_ref.shape[1]
    hn = (h_ref[...] + _dot(yml_ref[...], w_ref[:half, :]) + _dot(yrg_ref[...], w_ref[half:, :]))
    un = _rms_rows(hn) * g_ref[...]
    if len(rest) == 4:
        wr_ref, hn_ref, un_ref, lg_ref = rest
        wr = wr_ref[...]
        w_hi = wr.astype(BF16)
        w_lo = (wr - w_hi.astype(F32)).astype(BF16)
        u_hi = un.astype(BF16)
        u_lo = (un - u_hi.astype(F32)).astype(BF16)
        lg_ref[...] = _dot(u_hi, w_hi) + (_dot(u_hi, w_lo) + _dot(u_lo, w_hi))
    else:
        hn_ref, un_ref = rest
    hn_ref[...] = hn
    un_ref[...] = un.astype(un_ref.dtype)


def _outproj(yml, yrg, h, w_out, g_ffn, w_router, tm=512):
    T, D = h.shape
    half = yml.shape[1]
    tm = min(tm, T)
    with_router = w_router is not None
    row = lambda i: (i, 0)
    const = lambda i: (0, 0)
    in_specs = [
        pl.BlockSpec((tm, half), row),
        pl.BlockSpec((tm, half), row),
        pl.BlockSpec((tm, D), row),
        pl.BlockSpec((2 * half, D), const),
        pl.BlockSpec((1, D), const),
    ]
    out_specs = [pl.BlockSpec((tm, D), row), pl.BlockSpec((tm, D), row)]
    out_shape = [jax.ShapeDtypeStruct((T, D), F32),
                 jax.ShapeDtypeStruct((T, D), F32 if with_router else BF16)]
    args = [yml, yrg, h, w_out, g_ffn]
    if with_router:
        in_specs.append(pl.BlockSpec((D, LANES), const))
        out_specs.append(pl.BlockSpec((tm, LANES), row))
        out_shape.append(jax.ShapeDtypeStruct((T, LANES), F32))
        args.append(w_router)
    return pl.pallas_call(
        _outproj_kernel,
        grid=(T // tm,),
        in_specs=in_specs,
        out_specs=out_specs,
        out_shape=out_shape,
        compiler_params=_cparams("parallel"),
        name="outproj_router" if with_router else "outproj",
    )(*args)


W_SPLIT = 1
SUB_ROWS = 1024


def _cast_slabs(parts, dst_ref):
    rows = dst_ref.shape[0] // len(parts)
    for q, part in enumerate(parts):
        dst_ref[q * rows:(q + 1) * rows, :] = part[0].astype(BF16)


def _swiglu_steps(j, nj, x_ref, init_fn, weights, scratch, o_ref, sub_on):
    wg_parts, wu_parts, wd_parts = weights
    act_ref, wgb_ref, wub_ref, wdb_ref = scratch
    tm = o_ref.shape[0]
    sub = min(SUB_ROWS, tm)

    for s in range(tm // sub):
        rows = pl.ds(s * sub, sub)

        def up(s=s, rows=rows):
            if s == 0:
                _cast_slabs(wg_parts, wgb_ref)
                _cast_slabs(wu_parts, wub_ref)
            x = x_ref[rows, :]
            g = _dot(x, wgb_ref[...])
            u = _dot(x, wub_ref[...])
            act_ref[rows, :] = (g * _sigmoid(g) * u).astype(BF16)

        def down(s=s, rows=rows):
            if s == 0:
                _cast_slabs(wd_parts, wdb_ref)
            o_ref[rows, :] += _dot(act_ref[rows, :], wdb_ref[...])

        on = sub_on(s)
        cond = (lambda c: c) if on is None else (lambda c, on=on: jnp.logical_and(on, c))

        @pl.when(cond(j == 0))
        def _():
            o_ref[rows, :] = init_fn(rows)
            up()

        @pl.when(cond(jnp.logical_and(j > 0, j < nj)))
        def _():
            down()
            up()

        @pl.when(cond(j == nj))
        def _():
            down()

        if on is not None:
            @pl.when(jnp.logical_and(jnp.logical_not(on), j == nj))
            def _():
                o_ref[rows, :] = jnp.zeros((sub, o_ref.shape[1]), o_ref.dtype)


def _split_weights(refs):
    q = W_SPLIT
    return refs[:q], refs[q:2 * q], refs[2 * q:3 * q]


def _weight_specs(D, tf, up_map, down_map):
    q = W_SPLIT
    ups = [pl.BlockSpec((1, D // q, tf), functools.partial(up_map, s)) for s in range(q)]
    downs = [pl.BlockSpec((1, tf // q, D), functools.partial(down_map, s)) for s in range(q)]
    return ups + ups + downs


def _swiglu_scratch(tm, D, tf):
    return [pltpu.VMEM((tm, tf), BF16), pltpu.VMEM((D, tf), BF16), pltpu.VMEM((D, tf), BF16),
            pltpu.VMEM((tf, D), BF16)]


def _ffn_dense_kernel(u_ref, h_ref, *refs):
    weights = _split_weights(refs)
    o_ref = refs[3 * W_SPLIT]
    scratch = refs[3 * W_SPLIT + 1:]
    j = pl.program_id(1)
    nj = pl.num_programs(1) - 1
    _swiglu_steps(j, nj, u_ref, lambda rows: h_ref[rows, :], weights, scratch, o_ref, lambda s: None)


def _ffn_dense(un, h, wg, wu, wd, layer, tm=2048, tf=256):
    T, D = un.shape
    F = wg.shape[2]
    tm = min(tm, T)
    tf = min(tf, F)
    nj = F // tf
    q = W_SPLIT
    up = lambda s, i, j: (layer, s, jnp.minimum(j, nj - 1))
    down = lambda s, i, j: (layer, jnp.maximum(j - 1, 0) * q + s, 0)
    row = lambda i, j: (i, 0)
    return pl.pallas_call(
        _ffn_dense_kernel,
        grid=(T // tm, nj + 1),
        in_specs=[pl.BlockSpec((tm, D), row), pl.BlockSpec((tm, D), row)] + _weight_specs(D, tf, up, down),
        out_specs=pl.BlockSpec((tm, D), row),
        out_shape=jax.ShapeDtypeStruct((T, D), F32),
        scratch_shapes=_swiglu_scratch(tm, D, tf),
        compiler_params=_cparams("parallel", "arbitrary"),
        name="ffn_dense",
    )(un, h, *([wg] * q + [wu] * q + [wd] * q))


def _ffn_grouped_kernel(te_ref, tr_ref, nt_ref, x_ref, *refs):
    weights = _split_weights(refs)
    o_ref, xb_ref = refs[3 * W_SPLIT], refs[3 * W_SPLIT + 1]
    scratch = refs[3 * W_SPLIT + 2:]
    i = pl.program_id(0)
    j = pl.program_id(1)
    nj = pl.num_programs(1) - 1
    sub = min(SUB_ROWS, o_ref.shape[0])
    used = i < nt_ref[0]

    @pl.when(jnp.logical_and(used, j == 0))
    def _():
        xb_ref[...] = x_ref[...].astype(BF16)

    _swiglu_steps(j, nj, xb_ref, lambda rows: jnp.zeros((sub, o_ref.shape[1]), F32), weights, scratch,
                  o_ref, lambda s: jnp.logical_and(used, tr_ref[i] > s * sub))


def _ffn_grouped(tile_expert, tile_rows, n_tiles, xs, wg, wu, wd, first_expert, tm, tf=256):
    P, D = xs.shape
    F = wg.shape[2]
    tf = min(tf, F)
    nj = F // tf
    q = W_SPLIT

    def up(s, i, j, te, tr, nt):
        return (first_expert + te[i], s, jnp.where(i < nt[0], jnp.minimum(j, nj - 1), nj - 1))

    def down(s, i, j, te, tr, nt):
        return (first_expert + te[i], jnp.where(i < nt[0], jnp.maximum(j - 1, 0), nj - 1) * q + s, 0)

    grid_spec = pltpu.PrefetchScalarGridSpec(
        num_scalar_prefetch=3,
        grid=(P // tm, nj + 1),
        in_specs=[pl.BlockSpec((tm, D), lambda i, j, te, tr, nt: (jnp.minimum(i, nt[0] - 1), 0))]
        + _weight_specs(D, tf, up, down),
        out_specs=pl.BlockSpec((tm, D), lambda i, j, te, tr, nt: (i, 0)),
        scratch_shapes=[pltpu.VMEM((tm, D), BF16)] + _swiglu_scratch(tm, D, tf),
    )
    return pl.pallas_call(
        _ffn_grouped_kernel,
        grid_spec=grid_spec,
        out_shape=jax.ShapeDtypeStruct((P, D), F32),
        compiler_params=_cparams("arbitrary", "arbitrary"),
        name="ffn_grouped",
    )(tile_expert, tile_rows, n_tiles, xs, *([wg] * q + [wu] * q + [wd] * q))


def _router_kernel(lg_ref, rt_ref, cnt_ref, carry_ref):
    tm = lg_ref.shape[0]

    @pl.when(pl.program_id(0) == 0)
    def _():
        carry_ref[...] = jnp.zeros_like(carry_ref)

    lane = lax.broadcasted_iota(jnp.int32, (tm, LANES), 1).astype(F32)
    lg = jnp.where(lane < N_EXPERTS, lg_ref[...], -jnp.inf)
    v1 = jnp.max(lg, axis=1, keepdims=True)
    e1 = jnp.min(jnp.where(lg == v1, lane, float(LANES)), axis=1, keepdims=True)
    lg2 = jnp.where(lane == e1, -jnp.inf, lg)
    v2 = jnp.max(lg2, axis=1, keepdims=True)
    e2 = jnp.min(jnp.where(lg2 == v2, lane, float(LANES)), axis=1, keepdims=True)
    ex = jnp.exp(v2 - v1)
    w1 = 1.0 / (1.0 + ex)
    w2 = ex / (1.0 + ex)

    oh1 = (lane == e1).astype(F32)
    oh2 = (lane == e2).astype(F32)
    r_i = lax.broadcasted_iota(jnp.int32, (tm, tm), 0)
    c_i = lax.broadcasted_iota(jnp.int32, (tm, tm), 1)
    strict = (c_i < r_i).astype(BF16)
    before = _dot(strict, (oh1 + oh2).astype(BF16)) + carry_ref[0:1, :]
    rank1 = jnp.sum(before * oh1, axis=1, keepdims=True)
    rank2 = jnp.sum(before * oh2, axis=1, keepdims=True)
    total = carry_ref[0:1, :] + jnp.sum(oh1 + oh2, axis=0, keepdims=True)
    carry_ref[...] = jnp.broadcast_to(total, carry_ref.shape)
    cnt_ref[...] = jnp.broadcast_to(total, cnt_ref.shape)

    out = jnp.where(lane == 0, e1, 0.0)
    out = jnp.where(lane == 1, e2, out)
    out = jnp.where(lane == 2, w1, out)
    out = jnp.where(lane == 3, w2, out)
    out = jnp.where(lane == 4, rank1, out)
    out = jnp.where(lane == 5, rank2, out)
    rt_ref[...] = out


def _router(logits, tm=512):
    T = logits.shape[0]
    tm = min(tm, T)
    return pl.pallas_call(
        _router_kernel,
        grid=(T // tm,),
        in_specs=[pl.BlockSpec((tm, LANES), lambda i: (i, 0))],
        out_specs=[
            pl.BlockSpec((tm, LANES), lambda i: (i, 0)),
            pl.BlockSpec((SUBLANES, LANES), lambda i: (0, 0)),
        ],
        out_shape=[
            jax.ShapeDtypeStruct((T, LANES), F32),
            jax.ShapeDtypeStruct((SUBLANES, LANES), F32),
        ],
        scratch_shapes=[pltpu.VMEM((SUBLANES, LANES), F32)],
        compiler_params=_cparams("arbitrary"),
        name="router",
    )(logits)


def _dispatch_kernel(p1_ref, p2_ref, pad0_ref, padn_ref, nt_ref, u_ref, xs_hbm, zbuf_ref, sem, zsem,
                     *, pad_bits, tail_per_tile):
    tm = u_ref.shape[0]

    @pl.when(pl.program_id(0) == 0)
    def _():
        zbuf_ref[...] = jnp.zeros_like(zbuf_ref)

        def pad_copies(e, b):
            n = padn_ref[e]
            off = pad0_ref[e] + (n & ((1 << b) - 1))
            if (1 << b) >= SUBLANES:
                off = pl.multiple_of(off, SUBLANES)
                return [pltpu.make_async_copy(zbuf_ref.at[pl.ds(0, 1 << b)],
                                              xs_hbm.at[pl.ds(off, 1 << b)], zsem)]
            return [pltpu.make_async_copy(zbuf_ref.at[pl.ds(0, 1)], xs_hbm.at[pl.ds(off + r, 1)], zsem)
                    for r in range(1 << b)]

        for wait in (False, True):
            for e in range(N_EXPERTS):
                for b in range(pad_bits):
                    @pl.when(((padn_ref[e] >> b) & 1) == 1)
                    def _():
                        for cp in pad_copies(e, b):
                            if wait:
                                cp.wait()
                            else:
                                cp.start()

        zrows = zbuf_ref.shape[0]

        def tail_copy(t):
            off = pl.multiple_of(t * zrows, SUBLANES)
            return pltpu.make_async_copy(zbuf_ref, xs_hbm.at[pl.ds(off, zrows)], zsem)

        def tail_start(t, carry):
            tail_copy(t).start()
            return carry

        def tail_wait(t, carry):
            tail_copy(t).wait()
            return carry

        first, last = nt_ref[0] * tail_per_tile, xs_hbm.shape[0] // zrows
        lax.fori_loop(first, last, tail_start, 0)
        lax.fori_loop(first, last, tail_wait, 0)

    def row_copy(r, pos):
        return pltpu.make_async_copy(u_ref.at[pl.ds(r, 1)], xs_hbm.at[pl.ds(pos, 1)], sem)

    def issue(b, carry):
        for rr in range(ISSUE_UNROLL):
            r = b * ISSUE_UNROLL + rr
            row_copy(r, p1_ref[r]).start()
            row_copy(r, p2_ref[r]).start()
        return carry

    lax.fori_loop(0, tm // ISSUE_UNROLL, issue, 0)
    for _ in range(2):
        pltpu.make_async_copy(u_ref, xs_hbm.at[pl.ds(0, tm)], sem).wait()


def _dispatch(pos1, pos2, pad_start, pad_len, n_tiles, un, P, group_tile, tm=512):
    T, D = un.shape
    tm = min(tm, T)
    pad_bits = group_tile.bit_length() - 1
    assert group_tile == 1 << pad_bits and pad_bits >= 1
    zrows = group_tile // 2
    kern = functools.partial(_dispatch_kernel, pad_bits=pad_bits, tail_per_tile=group_tile // zrows)
    smem = pl.BlockSpec(memory_space=pltpu.SMEM)
    return pl.pallas_call(
        kern,
        grid=(T // tm,),
        in_specs=[
            pl.BlockSpec((tm,), lambda i: (i,), memory_space=pltpu.SMEM),
            pl.BlockSpec((tm,), lambda i: (i,), memory_space=pltpu.SMEM),
            smem, smem, smem,
            pl.BlockSpec((tm, D), lambda i: (i, 0)),
        ],
        out_specs=pl.BlockSpec(memory_space=pl.ANY),
        out_shape=jax.ShapeDtypeStruct((P, D), un.dtype),
        scratch_shapes=[pltpu.VMEM((zrows, D), un.dtype),
                        pltpu.SemaphoreType.DMA(()), pltpu.SemaphoreType.DMA(())],
        compiler_params=_cparams("arbitrary"),
        name="dispatch",
    )(pos1, pos2, pad_start, pad_len, n_tiles, un)


def _combine_kernel(p1_ref, p2_ref, q1_ref, q2_ref, h_ref, rt_ref, g_ref, y_hbm, o_ref, buf_ref, sem,
                    *, final_norm):
    tm = h_ref.shape[0]
    i = pl.program_id(0)
    n = pl.num_programs(0)
    slot = i % 2

    def issue_tile(i1_ref, i2_ref, s):
        def row_copy(k, r, pos):
            return pltpu.make_async_copy(y_hbm.at[pl.ds(pos, 1)], buf_ref.at[s, k, pl.ds(r, 1)], sem.at[s])

        def issue(b, carry):
            for rr in range(ISSUE_UNROLL):
                r = b * ISSUE_UNROLL + rr
                row_copy(0, r, i1_ref[r]).start()
                row_copy(1, r, i2_ref[r]).start()
            return carry

        lax.fori_loop(0, tm // ISSUE_UNROLL, issue, 0)

    @pl.when(i == 0)
    def _():
        issue_tile(p1_ref, p2_ref, 0)

    @pl.when(i + 1 < n)
    def _():
        issue_tile(q1_ref, q2_ref, 1 - slot)

    for k in range(2):
        pltpu.make_async_copy(y_hbm.at[pl.ds(0, tm)], buf_ref.at[slot, k], sem.at[slot]).wait()

    rt = rt_ref[...]
    out = h_ref[...] + (rt[:, 2:3] * buf_ref[slot, 0] + rt[:, 3:4] * buf_ref[slot, 1])
    if final_norm:
        out = _rms_rows(out) * g_ref[...]
    o_ref[...] = out


def _combine(pos1, pos2, h, routing, g_final, y, final_norm, tm=256):
    T, D = h.shape
    tm = min(tm, T)
    n = T // tm
    kern = functools.partial(_combine_kernel, final_norm=final_norm)
    cur = pl.BlockSpec((tm,), lambda i: (i,), memory_space=pltpu.SMEM)
    nxt = pl.BlockSpec((tm,), lambda i: (jnp.minimum(i + 1, n - 1),), memory_space=pltpu.SMEM)
    return pl.pallas_call(
        kern,
        grid=(n,),
        in_specs=[
            cur, cur, nxt, nxt,
            pl.BlockSpec((tm, D), lambda i: (i, 0)),
            pl.BlockSpec((tm, LANES), lambda i: (i, 0)),
            pl.BlockSpec((1, D), lambda i: (0, 0)),
            pl.BlockSpec(memory_space=pl.ANY),
        ],
        out_specs=pl.BlockSpec((tm, D), lambda i: (i, 0)),
        out_shape=jax.ShapeDtypeStruct((T, D), F32),
        scratch_shapes=[pltpu.VMEM((2, 2, tm, D), F32), pltpu.SemaphoreType.DMA((2,))],
        compiler_params=_cparams("arbitrary"),
        name="combine",
    )(pos1, pos2, pos1, pos2, h, routing, g_final, y)


def _final_norm_kernel(h_ref, g_ref, o_ref):
    o_ref[...] = _rms_rows(h_ref[...]) * g_ref[...]


def _final_norm(h, g, tm=1024):
    T, D = h.shape
    tm = min(tm, T)
    return pl.pallas_call(
        _final_norm_kernel,
        grid=(T // tm,),
        in_specs=[pl.BlockSpec((tm, D), lambda i: (i, 0)), pl.BlockSpec((1, D), lambda i: (0, 0))],
        out_specs=pl.BlockSpec((tm, D), lambda i: (i, 0)),
        out_shape=jax.ShapeDtypeStruct((T, D), F32),
        compiler_params=_cparams("parallel"),
        name="final_norm",
    )(h, g)


def _block_diag(w):
    G, n, _ = w.shape
    eye = jnp.eye(G, dtype=w.dtype)
    return (eye[:, None, :, None] * w[:, :, None, :]).reshape(G * n, G * n)


def _moe_layer(h, un, logits, w_gate, w_up, w_down, first_expert, g_final, final_norm, tm):
    T, D = h.shape
    E = N_EXPERTS
    routing, counts = _router(logits)
    e1 = routing[:, 0].astype(jnp.int32)
    e2 = routing[:, 1].astype(jnp.int32)
    rank1 = routing[:, 4].astype(jnp.int32)
    rank2 = routing[:, 5].astype(jnp.int32)
    cnt = counts[0, :E].astype(jnp.int32)
    tiles_per = (cnt + tm - 1) // tm
    tile_end = jnp.cumsum(tiles_per)
    offs = (tile_end - tiles_per) * tm
    pos1 = offs[e1] + rank1
    pos2 = offs[e2] + rank2
    n_slots = (2 * T) // tm + E
    P = n_slots * tm
    tile_expert = jnp.minimum(
        jnp.sum(jnp.arange(n_slots, dtype=jnp.int32)[:, None] >= tile_end[None, :], axis=1), E - 1
    ).astype(jnp.int32)
    n_tiles = tile_end[E - 1:E].astype(jnp.int32)
    tile_in_group = jnp.arange(n_slots, dtype=jnp.int32) - (tile_end - tiles_per)[tile_expert]
    tile_rows = jnp.clip(cnt[tile_expert] - tile_in_group * tm, 0, tm).astype(jnp.int32)

    xs = _dispatch(pos1, pos2, offs + cnt, tiles_per * tm - cnt, n_tiles, un, P, tm)
    y = _ffn_grouped(tile_expert, tile_rows, n_tiles, xs, w_gate, w_up, w_down, first_expert, tm)
    return _combine(pos1, pos2, h, routing, g_final, y, final_norm)


def _forward(x, norm_mix_g, w_in, ml_b_if, ml_norm_g, rg_conv_w, rg_conv_b, rg_w_a, rg_b_a,
             rg_w_x, rg_b_x, rg_lam, rg_norm_g, w_out, norm_ffn_g, ffn_w_gate, ffn_w_up,
             ffn_w_down, moe_w_router, moe_w_gate, moe_w_up, moe_w_down, norm_final_g,
             moe_tile=2048):
    B, S, D = x.shape
    T = B * S
    depth = w_in.shape[0]
    ml_w = ML_HEADS * ML_HEAD_DIM
    n_q = 4 * ml_w
    n_if = 2 * ML_HEADS
    rg_w = rg_lam.shape[1]
    n_r = 2 * rg_w

    h = x.reshape(T, D)
    g_final = norm_final_g.reshape(1, D)
    moe_gate_all = moe_w_gate.reshape((-1,) + moe_w_gate.shape[2:])
    moe_up_all = moe_w_up.reshape((-1,) + moe_w_up.shape[2:])
    moe_down_all = moe_w_down.reshape((-1,) + moe_w_down.shape[2:])
    for l in range(depth):
        wl = w_in[l]
        w_all = jnp.concatenate(
            [wl[:, :ml_w], wl[:, 2 * ml_w:n_q], wl[:, n_q + n_if:], wl[:, n_q:n_q + n_if],
             jnp.zeros((D, LANES - n_if), wl.dtype)], axis=1).astype(BF16)
        wkt = wl[:, ml_w:2 * ml_w].T.astype(BF16)
        b_if = jnp.concatenate([ml_b_if[l], jnp.zeros((LANES - n_if,), F32)]).reshape(1, LANES)
        zq, kt, zr, zg = _inproj(h, norm_mix_g[l].reshape(1, D), b_if, w_all, wkt, 3 * ml_w, n_r)

        zgt = zg[:, :SUBLANES].T.reshape(SUBLANES, T // CHUNK, CHUNK).transpose(1, 0, 2)
        yml = _mlstm(zq, kt, zg, zgt, ml_norm_g[l].reshape(1, ml_w), B, S)

        w_gates = jnp.concatenate([_block_diag(rg_w_a[l]), _block_diag(rg_w_x[l])], axis=1).astype(BF16)
        b_gates = jnp.concatenate([rg_b_a[l], rg_b_x[l]]).reshape(1, n_r)
        yrg = _rglru(zr, rg_conv_w[l], rg_conv_b[l].reshape(1, rg_w), w_gates, b_gates,
                     rg_lam[l].reshape(1, rg_w), rg_norm_g[l].reshape(1, rg_w), B, S)

        j = l // 2
        is_moe = (l % 2 == 1)
        w_router = None
        if is_moe:
            w_router = jnp.concatenate(
                [moe_w_router[j], jnp.zeros((D, LANES - N_EXPERTS), F32)], axis=1)
        res = _outproj(yml, yrg, h, w_out[l].astype(BF16), norm_ffn_g[l].reshape(1, D), w_router)
        h, un = res[0], res[1]
        last = (l == depth - 1)
        if is_moe:
            logits = res[2]
            h = _moe_layer(h, un, logits, moe_gate_all, moe_up_all, moe_down_all, j * N_EXPERTS,
                           g_final, last, moe_tile)
        else:
            h = _ffn_dense(un, h, ffn_w_gate, ffn_w_up, ffn_w_down, j)
            if last:
                h = _final_norm(h, g_final)
    return h.reshape(B, S, D)


def kernel(x, norm_mix_g, w_in, ml_b_if, ml_norm_g, rg_conv_w, rg_conv_b, rg_w_a, rg_b_a, rg_w_x,
           rg_b_x, rg_lam, rg_norm_g, w_out, norm_ffn_g, ffn_w_gate, ffn_w_up, ffn_w_down,
           moe_w_router, moe_w_gate, moe_w_up, moe_w_down, norm_final_g):
    return _forward(x, norm_mix_g, w_in, ml_b_if, ml_norm_g, rg_conv_w, rg_conv_b, rg_w_a, rg_b_a,
                    rg_w_x, rg_b_x, rg_lam, rg_norm_g, w_out, norm_ffn_g, ffn_w_gate, ffn_w_up,
                    ffn_w_down, moe_w_router, moe_w_gate, moe_w_up, moe_w_down, norm_final_g)
```

```python
import functools

import jax
import jax.numpy as jnp
from jax import lax
from jax.experimental import pallas as pl
from jax.experimental.pallas import tpu as pltpu

EPS = 1e-6
ML_HEADS = 4
ML_HEAD_DIM = 128
CHUNK = 128
RG_BLOCKS = 8
RG_C = 8.0
CONV_WIDTH = 4
N_EXPERTS = 8
LANES = 128
SUBLANES = 8
VMEM_LIMIT = 56 * 1024 * 1024
ISSUE_UNROLL = 8

BF16 = jnp.bfloat16
F32 = jnp.float32


def _cparams(*sem):
    return pltpu.CompilerParams(dimension_semantics=sem, vmem_limit_bytes=VMEM_LIMIT)


def _dot(a, b):
    return jnp.dot(a, b, preferred_element_type=F32)


def _dot_nt(a, b):
    return lax.dot_general(a, b, (((1,), (1,)), ((), ())), preferred_element_type=F32)


def _dot_tn(a, b):
    return lax.dot_general(a, b, (((0,), (0,)), ((), ())), preferred_element_type=F32)


def _dot_f32(a, b):
    return jnp.dot(a, b, preferred_element_type=F32, precision=lax.Precision.HIGHEST)


def _sigmoid(x):
    return 1.0 / (1.0 + jnp.exp(-x))


def _log_sigmoid(x):
    return jnp.minimum(x, 0.0) - jnp.log(1.0 + jnp.exp(-jnp.abs(x)))


def _softplus(x):
    return jnp.maximum(x, 0.0) + jnp.log(1.0 + jnp.exp(-jnp.abs(x)))


def _gelu_tanh(x):
    return 0.5 * x * (1.0 + jnp.tanh(0.7978845608028654 * (x + 0.044715 * (x * x * x))))


def _rms_rows(x):
    return x * lax.rsqrt(jnp.mean(x * x, axis=-1, keepdims=True) + EPS)


def _inproj_kernel(h_ref, g_ref, b_ref, bt_ref, w_ref, wt_ref, zq_ref, kt_ref, zr_ref, zg_ref, zgt_ref,
                   *, n_q, n_r, n_col):
    xn = (_rms_rows(h_ref[...]) * g_ref[...]).astype(BF16)
    for c0 in range(0, n_q, n_col):
        zq_ref[:, c0:c0 + n_col] = _dot(xn, w_ref[:, c0:c0 + n_col]).astype(BF16)
    for c0 in range(0, n_r, n_col):
        zr_ref[:, c0:c0 + n_col] = _dot(xn, w_ref[:, n_q + c0:n_q + c0 + n_col]).astype(BF16)
    zg_ref[...] = _dot(xn, w_ref[:, n_q + n_r:]) + b_ref[...]
    n_k = kt_ref.shape[1]
    t = _dot_nt(wt_ref[...], xn)
    for c in range(kt_ref.shape[0]):
        kt_ref[c] = t[:n_k, c * CHUNK:(c + 1) * CHUNK].astype(BF16)
        zgt_ref[c] = t[n_k:, c * CHUNK:(c + 1) * CHUNK] + bt_ref[...]


def _inproj(h, g, b_if, b_if_t, w_all, wt, n_q, n_r, tm=512):
    T, D = h.shape
    tm = min(tm, T)
    n_all = w_all.shape[1]
    n_g = n_all - n_q - n_r
    n_k = wt.shape[0] - SUBLANES
    kern = functools.partial(_inproj_kernel, n_q=n_q, n_r=n_r, n_col=512)
    const = lambda i: (0, 0)
    return pl.pallas_call(
        kern,
        grid=(T // tm,),
        in_specs=[
            pl.BlockSpec((tm, D), lambda i: (i, 0)),
            pl.BlockSpec((1, D), const),
            pl.BlockSpec((1, n_g), const),
            pl.BlockSpec((SUBLANES, 1), const),
            pl.BlockSpec((D, n_all), const),
            pl.BlockSpec((n_k + SUBLANES, D), const),
        ],
        out_specs=[
            pl.BlockSpec((tm, n_q), lambda i: (i, 0)),
            pl.BlockSpec((tm // CHUNK, n_k, CHUNK), lambda i: (i, 0, 0)),
            pl.BlockSpec((tm, n_r), lambda i: (i, 0)),
            pl.BlockSpec((tm, n_g), lambda i: (i, 0)),
            pl.BlockSpec((tm // CHUNK, SUBLANES, CHUNK), lambda i: (i, 0, 0)),
        ],
        out_shape=[
            jax.ShapeDtypeStruct((T, n_q), BF16),
            jax.ShapeDtypeStruct((T // CHUNK, n_k, CHUNK), BF16),
            jax.ShapeDtypeStruct((T, n_r), BF16),
            jax.ShapeDtypeStruct((T, n_g), F32),
            jax.ShapeDtypeStruct((T // CHUNK, SUBLANES, CHUNK), F32),
        ],
        compiler_params=_cparams("parallel"),
        name="inproj",
    )(h, g, b_if, b_if_t, w_all, wt)


def _mlstm_kernel(zq_ref, kt_ref, zg_ref, zgt_ref, gain_ref, y_ref, c_ref, m_ref, pcol_ref, prow_ref,
                  *, n_chunks):
    H, Dh, L = ML_HEADS, ML_HEAD_DIM, CHUNK
    W = H * Dh
    scale = Dh ** -0.5

    @pl.when(pl.program_id(1) == 0)
    def _():
        c_ref[...] = jnp.zeros_like(c_ref)
        m_ref[...] = jnp.zeros_like(m_ref)

    row = lax.broadcasted_iota(jnp.int32, (L, L), 0)
    col = lax.broadcasted_iota(jnp.int32, (L, L), 1)
    causal = col <= row
    tri_l = causal.astype(F32)
    tri_u = (row <= col).astype(F32)
    ones_blk = jnp.ones((L, Dh), BF16)

    def gate_prefix(c, slot):
        r0 = pl.multiple_of(c * L, L)
        gcol = zg_ref[pl.ds(r0, L), :]
        grow = zgt_ref[c]
        bcol_all = _dot_f32(tri_l, _log_sigmoid(gcol))
        brow_all = _dot_f32(_log_sigmoid(grow), tri_u)
        cm = gcol - pltpu.roll(bcol_all, LANES - H, 1)
        d = 1
        while d < L:
            cm = jnp.maximum(cm, jnp.where(row >= d, pltpu.roll(cm, d, 0), -jnp.inf))
            d *= 2
        pcol_ref[slot, 0] = bcol_all
        pcol_ref[slot, 1] = cm
        prow_ref[slot] = brow_all

    gate_prefix(0, 0)

    def chunk_body(c, carry):
        r0 = pl.multiple_of(c * L, L)
        slot = c % 2
        grow = zgt_ref[c]
        bcol_all = pcol_ref[slot, 0]
        cm = pcol_ref[slot, 1]
        brow_all = prow_ref[slot]

        for hd in range(H):
            q = zq_ref[pl.ds(r0, L), hd * Dh:(hd + 1) * Dh]
            v = zq_ref[pl.ds(r0, L), W + hd * Dh:W + (hd + 1) * Dh]
            o = zq_ref[pl.ds(r0, L), 2 * W + hd * Dh:2 * W + (hd + 1) * Dh]
            kt = kt_ref[c, hd * Dh:(hd + 1) * Dh, :]
            v_aug = jnp.concatenate([v, ones_blk], axis=1)

            li_row = grow[hd:hd + 1, :]
            b_row = brow_all[H + hd:H + hd + 1, :]
            g_tot = b_row[:, L - 1:L]
            r_row = li_row - b_row

            c_prev = c_ref[hd]
            m_prev = m_ref[hd]

            mx = jnp.maximum(jnp.broadcast_to(cm[:, hd:hd + 1], (L, L)), m_prev)
            bb = jnp.broadcast_to(bcol_all[:, H + hd:H + hd + 1], (L, L))
            p = jnp.exp(jnp.where(causal, r_row - mx, -jnp.inf))
            s_w = p * (_dot(q, kt) * scale)
            inter = jnp.exp(m_prev - mx)
            intra = _dot(s_w.astype(BF16), v_aug)
            cross = _dot(q, c_prev.astype(BF16))
            num = intra[:, :Dh] + inter * cross[:, :Dh]
            den = intra[:, Dh:] + inter * cross[:, Dh:]
            hh = num / jnp.maximum(jnp.abs(den), jnp.exp(-(bb + mx)))
            ht = _rms_rows(hh) * gain_ref[:, hd * Dh:(hd + 1) * Dh]
            y_ref[pl.ds(r0, L), hd * Dh:(hd + 1) * Dh] = (_sigmoid(o.astype(F32)) * ht).astype(BF16)

            a = g_tot + r_row
            m_loc = jnp.max(a, axis=1, keepdims=True)
            w = jnp.exp(a - m_loc) * scale
            c_loc = _dot((kt.astype(F32) * w).astype(BF16), v_aug)
            m_new = jnp.maximum(g_tot + m_prev, m_loc)
            s_old = jnp.exp(g_tot + m_prev - m_new)
            s_loc = jnp.exp(m_loc - m_new)
            c_ref[hd] = s_old * c_prev + s_loc * c_loc
            m_ref[hd] = m_new
        gate_prefix(jnp.minimum(c + 1, n_chunks - 1), 1 - slot)
        return carry

    lax.fori_loop(0, n_chunks, chunk_body, 0)


def _mlstm(zq, kt, zg, zgt, gain, B, S, rows=1024):
    T = B * S
    rows = min(rows, S)
    n_chunks = rows // CHUNK
    steps = S // rows
    W = ML_HEADS * ML_HEAD_DIM
    kern = functools.partial(_mlstm_kernel, n_chunks=n_chunks)
    return pl.pallas_call(
        kern,
        grid=(B, steps),
        in_specs=[
            pl.BlockSpec((rows, 3 * W), lambda b, s: (b * steps + s, 0)),
            pl.BlockSpec((n_chunks, W, CHUNK), lambda b, s: (b * steps + s, 0, 0)),
            pl.BlockSpec((rows, LANES), lambda b, s: (b * steps + s, 0)),
            pl.BlockSpec((n_chunks, SUBLANES, CHUNK), lambda b, s: (b * steps + s, 0, 0)),
            pl.BlockSpec((1, W), lambda b, s: (0, 0)),
        ],
        out_specs=pl.BlockSpec((rows, W), lambda b, s: (b * steps + s, 0)),
        out_shape=jax.ShapeDtypeStruct((T, W), BF16),
        scratch_shapes=[
            pltpu.VMEM((ML_HEADS, ML_HEAD_DIM, 2 * ML_HEAD_DIM), F32),
            pltpu.VMEM((ML_HEADS, 1, 1), F32),
            pltpu.VMEM((2, 2, CHUNK, LANES), F32),
            pltpu.VMEM((2, SUBLANES, CHUNK), F32),
        ],
        compiler_params=_cparams("parallel", "arbitrary"),
        name="mlstm",
    )(zq, kt, zg, zgt, gain)


def _rglru_kernel(zr_ref, cw_ref, cb_ref, wg_ref, bg_ref, lam_ref, gain_ref, y_ref,
                  nat_ref, ost_ref, a_ref, u_ref, p_ref, cx_ref, hc_ref, *, rows):
    Wd = y_ref.shape[1]
    NB = SUBLANES
    BL = rows // NB
    PITCH = BL + SUBLANES
    n_slab = Wd // LANES
    taps = CONV_WIDTH - 1

    @pl.when(pl.program_id(1) == 0)
    def _():
        cx_ref[...] = jnp.zeros_like(cx_ref)
        hc_ref[...] = jnp.zeros_like(hc_ref)

    for s in range(NB):
        blk = zr_ref[s * BL:(s + 1) * BL, :].astype(F32)
        for c in range(2 * n_slab):
            nat_ref[c, s * PITCH:s * PITCH + BL, :] = blk[:, c * LANES:(c + 1) * LANES]

    def interleaved(c):
        return jnp.concatenate([nat_ref[c, pl.ds(i, NB, stride=PITCH), :] for i in range(BL)], axis=0)

    first_sub = lax.broadcasted_iota(jnp.int32, (NB, LANES), 0) == 0
    xcs = []
    for c in range(n_slab):
        lanes = slice(c * LANES, (c + 1) * LANES)
        x = interleaved(c)
        head = []
        for k in range(taps, 0, -1):
            tail = x[(BL - k) * NB:(BL - k + 1) * NB, :]
            prev = cx_ref[c * taps + k - 1]
            head.append(jnp.where(first_sub, pltpu.roll(prev, 1, 0), pltpu.roll(tail, 1, 0)))
            cx_ref[c * taps + k - 1] = tail
        ext = jnp.concatenate(head + [x], axis=0)
        xc = cb_ref[:, lanes] + jnp.zeros((rows, LANES), F32)
        for j in range(CONV_WIDTH):
            xc = xc + cw_ref[j:j + 1, lanes] * ext[j * NB:j * NB + rows, :]
        xcs.append(xc)
    xc = jnp.concatenate(xcs, axis=1)

    gates = _dot(xc.astype(BF16), wg_ref[...]) + bg_ref[...]
    r = _sigmoid(gates[:, :Wd])
    ig = _sigmoid(gates[:, Wd:])
    a = jnp.exp(r * ((-RG_C) * _softplus(-lam_ref[...])))
    a_ref[...] = a
    d = 1.0 - a * a
    u_ref[...] = jnp.where(d > 0.0, d * lax.rsqrt(d), 0.0) * (ig * xc)

    for c in range(n_slab):
        lanes = slice(c * LANES, (c + 1) * LANES)
        h = jnp.zeros((NB, LANES), F32)
        p = jnp.ones((NB, LANES), F32)
        for i in range(BL):
            ai = a_ref[i * NB:(i + 1) * NB, lanes]
            h = ai * h + u_ref[i * NB:(i + 1) * NB, lanes]
            p = ai * p
            u_ref[i * NB:(i + 1) * NB, lanes] = h
            p_ref[i * NB:(i + 1) * NB, lanes] = p
    h_end = u_ref[(BL - 1) * NB:BL * NB, :]
    p_end = p_ref[(BL - 1) * NB:BL * NB, :]
    h_in = [hc_ref[0:1, :]]
    for s in range(NB):
        h_in.append(h_end[s:s + 1, :] + p_end[s:s + 1, :] * h_in[s])
    hc_ref[...] = jnp.broadcast_to(h_in[NB], hc_ref.shape)
    h_enter = jnp.concatenate(h_in[:NB], axis=0)
    h_all = u_ref[...] + p_ref[...] * jnp.concatenate([h_enter] * BL, axis=0)

    gate = jnp.concatenate([interleaved(n_slab + c) for c in range(n_slab)], axis=1)
    yv = h_all * _gelu_tanh(gate)
    yn = _rms_rows(yv) * gain_ref[...]
    for c in range(n_slab):
        for i in range(BL):
            ost_ref[c, pl.ds(i, NB, stride=PITCH), :] = yn[i * NB:(i + 1) * NB, c * LANES:(c + 1) * LANES]
    for s in range(NB):
        y_ref[s * BL:(s + 1) * BL, :] = jnp.concatenate(
            [ost_ref[c, s * PITCH:s * PITCH + BL, :] for c in range(n_slab)], axis=1).astype(BF16)


def _rglru(zr, conv_w, conv_b, w_gates, b_gates, lam, gain, B, S, rows=512):
    T = B * S
    Wd = zr.shape[1] // 2
    rows = min(rows, S)
    steps = S // rows
    pitch_rows = SUBLANES * (rows // SUBLANES + SUBLANES)
    n_slab = Wd // LANES
    kern = functools.partial(_rglru_kernel, rows=rows)
    const = lambda b, s: (0, 0)
    return pl.pallas_call(
        kern,
        grid=(B, steps),
        in_specs=[
            pl.BlockSpec((rows, 2 * Wd), lambda b, s: (b * steps + s, 0)),
            pl.BlockSpec((CONV_WIDTH, Wd), const),
            pl.BlockSpec((1, Wd), const),
            pl.BlockSpec((Wd, 2 * Wd), const),
            pl.BlockSpec((1, 2 * Wd), const),
            pl.BlockSpec((1, Wd), const),
            pl.BlockSpec((1, Wd), const),
        ],
        out_specs=pl.BlockSpec((rows, Wd), lambda b, s: (b * steps + s, 0)),
        out_shape=jax.ShapeDtypeStruct((T, Wd), BF16),
        scratch_shapes=[
            pltpu.VMEM((2 * n_slab, pitch_rows, LANES), F32),
            pltpu.VMEM((n_slab, pitch_rows, LANES), F32),
            pltpu.VMEM((rows, Wd), F32),
            pltpu.VMEM((rows, Wd), F32),
            pltpu.VMEM((rows, Wd), F32),
            pltpu.VMEM((n_slab * (CONV_WIDTH - 1), SUBLANES, LANES), F32),
            pltpu.VMEM((SUBLANES, Wd), F32),
        ],
        compiler_params=_cparams("parallel", "arbitrary"),
        name="rglru",
    )(zr, conv_w, conv_b, w_gates, b_gates, lam, gain)


def _outproj_kernel(yml_ref, yrg_ref, h_ref, w_ref, g_ref, *rest):
    half = yml_ref.shape[1]
    hn = (h_ref[...] + _dot(yml_ref[...], w_ref[:half, :]) + _dot(yrg_ref[...], w_ref[half:, :]))
    un = _rms_rows(hn) * g_ref[...]
    if len(rest) == 4:
        wr_ref, hn_ref, un_ref, lg_ref = rest
        wr = wr_ref[...]
        w_hi = wr.astype(BF16)
        w_lo = (wr - w_hi.astype(F32)).astype(BF16)
        u_hi = un.astype(BF16)
        u_lo = (un - u_hi.astype(F32)).astype(BF16)
        lg_ref[...] = _dot(u_hi, w_hi) + (_dot(u_hi, w_lo) + _dot(u_lo, w_hi))
    else:
        hn_ref, un_ref = rest
    hn_ref[...] = hn
    un_ref[...] = un.astype(un_ref.dtype)


def _outproj(yml, yrg, h, w_out, g_ffn, w_router, tm=512):
    T, D = h.shape
    half = yml.shape[1]
    tm = min(tm, T)
    with_router = w_router is not None
    row = lambda i: (i, 0)
    const = lambda i: (0, 0)
    in_specs = [
        pl.BlockSpec((tm, half), row),
        pl.BlockSpec((tm, half), row),
        pl.BlockSpec((tm, D), row),
        pl.BlockSpec((2 * half, D), const),
        pl.BlockSpec((1, D), const),
    ]
    out_specs = [pl.BlockSpec((tm, D), row), pl.BlockSpec((tm, D), row)]
    out_shape = [jax.ShapeDtypeStruct((T, D), F32),
                 jax.ShapeDtypeStruct((T, D), F32 if with_router else BF16)]
    args = [yml, yrg, h, w_out, g_ffn]
    if with_router:
        in_specs.append(pl.BlockSpec((D, LANES), const))
        out_specs.append(pl.BlockSpec((tm, LANES), row))
        out_shape.append(jax.ShapeDtypeStruct((T, LANES), F32))
        args.append(w_router)
    return pl.pallas_call(
        _outproj_kernel,
        grid=(T // tm,),
        in_specs=in_specs,
        out_specs=out_specs,
        out_shape=out_shape,
        compiler_params=_cparams("parallel"),
        name="outproj_router" if with_router else "outproj",
    )(*args)


W_SPLIT = 1
SUB_ROWS = 1024


def _cast_slabs(parts, dst_ref):
    rows = dst_ref.shape[0] // len(parts)
    for q, part in enumerate(parts):
        dst_ref[q * rows:(q + 1) * rows, :] = part[0].astype(BF16)


def _swiglu_steps(j, nj, x_ref, init_fn, weights, scratch, o_ref, sub_on):
    wg_parts, wu_parts, wd_parts = weights
    act_ref, wgb_ref, wub_ref, wdb_ref = scratch
    tm = o_ref.shape[0]
    sub = min(SUB_ROWS, tm)

    for s in range(tm // sub):
        rows = pl.ds(s * sub, sub)

        def up(s=s, rows=rows):
            if s == 0:
                _cast_slabs(wg_parts, wgb_ref)
                _cast_slabs(wu_parts, wub_ref)
            x = x_ref[rows, :]
            g = _dot(x, wgb_ref[...])
            u = _dot(x, wub_ref[...])
            act_ref[rows, :] = (g * _sigmoid(g) * u).astype(BF16)

        def down(s=s, rows=rows):
            if s == 0:
                _cast_slabs(wd_parts, wdb_ref)
            o_ref[rows, :] += _dot(act_ref[rows, :], wdb_ref[...])

        on = sub_on(s)
        cond = (lambda c: c) if on is None else (lambda c, on=on: jnp.logical_and(on, c))

        @pl.when(cond(j == 0))
        def _():
            up()
            o_ref[rows, :] = init_fn(rows)

        @pl.when(cond(jnp.logical_and(j > 0, j < nj)))
        def _():
            down()
            up()

        @pl.when(cond(j == nj))
        def _():
            down()

        if on is not None:
            @pl.when(jnp.logical_and(jnp.logical_not(on), j == nj))
            def _():
                o_ref[rows, :] = jnp.zeros((sub, o_ref.shape[1]), o_ref.dtype)


def _split_weights(refs):
    q = W_SPLIT
    return refs[:q], refs[q:2 * q], refs[2 * q:3 * q]


def _weight_specs(D, tf, up_map, down_map):
    q = W_SPLIT
    ups = [pl.BlockSpec((1, D // q, tf), functools.partial(up_map, s)) for s in range(q)]
    downs = [pl.BlockSpec((1, tf // q, D), functools.partial(down_map, s)) for s in range(q)]
    return ups + ups + downs


def _swiglu_scratch(tm, D, tf):
    return [pltpu.VMEM((tm, tf), BF16), pltpu.VMEM((D, tf), BF16), pltpu.VMEM((D, tf), BF16),
            pltpu.VMEM((tf, D), BF16)]


def _ffn_dense_kernel(u_ref, h_ref, *refs):
    weights = _split_weights(refs)
    o_ref = refs[3 * W_SPLIT]
    scratch = refs[3 * W_SPLIT + 1:]
    j = pl.program_id(1)
    nj = pl.num_programs(1) - 1
    _swiglu_steps(j, nj, u_ref, lambda rows: h_ref[rows, :], weights, scratch, o_ref, lambda s: None)


def _ffn_dense(un, h, wg, wu, wd, layer, tm=2048, tf=256):
    T, D = un.shape
    F = wg.shape[2]
    tm = min(tm, T)
    tf = min(tf, F)
    nj = F // tf
    q = W_SPLIT
    up = lambda s, i, j: (layer, s, jnp.minimum(j, nj - 1))
    down = lambda s, i, j: (layer, jnp.maximum(j - 1, 0) * q + s, 0)
    row = lambda i, j: (i, 0)
    return pl.pallas_call(
        _ffn_dense_kernel,
        grid=(T // tm, nj + 1),
        in_specs=[pl.BlockSpec((tm, D), row), pl.BlockSpec((tm, D), row)] + _weight_specs(D, tf, up, down),
        out_specs=pl.BlockSpec((tm, D), row),
        out_shape=jax.ShapeDtypeStruct((T, D), F32),
        scratch_shapes=_swiglu_scratch(tm, D, tf),
        compiler_params=_cparams("parallel", "arbitrary"),
        name="ffn_dense",
    )(un, h, *([wg] * q + [wu] * q + [wd] * q))


def _ffn_grouped_kernel(te_ref, tr_ref, nt_ref, x_ref, *refs):
    weights = _split_weights(refs)
    o_ref, xb_ref = refs[3 * W_SPLIT], refs[3 * W_SPLIT + 1]
    scratch = refs[3 * W_SPLIT + 2:]
    i = pl.program_id(0)
    j = pl.program_id(1)
    nj = pl.num_programs(1) - 1
    sub = min(SUB_ROWS, o_ref.shape[0])
    used = i < nt_ref[0]

    @pl.when(jnp.logical_and(used, j == 0))
    def _():
        xb_ref[...] = x_ref[...].astype(BF16)

    _swiglu_steps(j, nj, xb_ref, lambda rows: jnp.zeros((sub, o_ref.shape[1]), F32), weights, scratch,
                  o_ref, lambda s: jnp.logical_and(used, tr_ref[i] > s * sub))


def _ffn_grouped(tile_expert, tile_rows, n_tiles, xs, wg, wu, wd, first_expert, tm, tf=256):
    P, D = xs.shape
    F = wg.shape[2]
    tf = min(tf, F)
    nj = F // tf
    q = W_SPLIT

    def up(s, i, j, te, tr, nt):
        return (first_expert + te[i], s, jnp.where(i < nt[0], jnp.minimum(j, nj - 1), nj - 1))

    def down(s, i, j, te, tr, nt):
        return (first_expert + te[i], jnp.where(i < nt[0], jnp.maximum(j - 1, 0), nj - 1) * q + s, 0)

    grid_spec = pltpu.PrefetchScalarGridSpec(
        num_scalar_prefetch=3,
        grid=(P // tm, nj + 1),
        in_specs=[pl.BlockSpec((tm, D), lambda i, j, te, tr, nt: (jnp.minimum(i, nt[0] - 1), 0))]
        + _weight_specs(D, tf, up, down),
        out_specs=pl.BlockSpec((tm, D), lambda i, j, te, tr, nt: (i, 0)),
        scratch_shapes=[pltpu.VMEM((tm, D), BF16)] + _swiglu_scratch(tm, D, tf),
    )
    return pl.pallas_call(
        _ffn_grouped_kernel,
        grid_spec=grid_spec,
        out_shape=jax.ShapeDtypeStruct((P, D), F32),
        compiler_params=_cparams("arbitrary", "arbitrary"),
        name="ffn_grouped",
    )(tile_expert, tile_rows, n_tiles, xs, *([wg] * q + [wu] * q + [wd] * q))


def _router_kernel(lg_ref, rt_ref, rtt_ref, cnt_ref, carry_ref):
    tm = lg_ref.shape[0]

    @pl.when(pl.program_id(0) == 0)
    def _():
        carry_ref[...] = jnp.zeros_like(carry_ref)

    lane = lax.broadcasted_iota(jnp.int32, (tm, LANES), 1).astype(F32)
    lg = jnp.where(lane < N_EXPERTS, lg_ref[...], -jnp.inf)
    v1 = jnp.max(lg, axis=1, keepdims=True)
    e1 = jnp.min(jnp.where(lg == v1, lane, float(LANES)), axis=1, keepdims=True)
    lg2 = jnp.where(lane == e1, -jnp.inf, lg)
    v2 = jnp.max(lg2, axis=1, keepdims=True)
    e2 = jnp.min(jnp.where(lg2 == v2, lane, float(LANES)), axis=1, keepdims=True)
    ex = jnp.exp(v2 - v1)
    w1 = 1.0 / (1.0 + ex)
    w2 = ex / (1.0 + ex)

    oh1 = (lane == e1).astype(F32)
    oh2 = (lane == e2).astype(F32)
    r_i = lax.broadcasted_iota(jnp.int32, (tm, tm), 0)
    c_i = lax.broadcasted_iota(jnp.int32, (tm, tm), 1)
    strict = (c_i < r_i).astype(BF16)
    before = _dot(strict, (oh1 + oh2).astype(BF16)) + carry_ref[0:1, :]
    rank1 = jnp.sum(before * oh1, axis=1, keepdims=True)
    rank2 = jnp.sum(before * oh2, axis=1, keepdims=True)
    total = carry_ref[0:1, :] + jnp.sum(oh1 + oh2, axis=0, keepdims=True)
    carry_ref[...] = jnp.broadcast_to(total, carry_ref.shape)
    cnt_ref[...] = jnp.broadcast_to(total, cnt_ref.shape)

    out = jnp.where(lane == 0, e1, 0.0)
    out = jnp.where(lane == 1, e2, out)
    out = jnp.where(lane == 2, w1, out)
    out = jnp.where(lane == 3, w2, out)
    out = jnp.where(lane == 4, rank1, out)
    out = jnp.where(lane == 5, rank2, out)
    rt_ref[...] = out
    rtt_ref[...] = jnp.transpose(out)[:SUBLANES, :]


def _router(logits, tm=512):
    T = logits.shape[0]
    tm = min(tm, T)
    return pl.pallas_call(
        _router_kernel,
        grid=(T // tm,),
        in_specs=[pl.BlockSpec((tm, LANES), lambda i: (i, 0))],
        out_specs=[
            pl.BlockSpec((tm, LANES), lambda i: (i, 0)),
            pl.BlockSpec((SUBLANES, tm), lambda i: (0, i)),
            pl.BlockSpec((SUBLANES, LANES), lambda i: (0, 0)),
        ],
        out_shape=[
            jax.ShapeDtypeStruct((T, LANES), F32),
            jax.ShapeDtypeStruct((SUBLANES, T), F32),
            jax.ShapeDtypeStruct((SUBLANES, LANES), F32),
        ],
        scratch_shapes=[pltpu.VMEM((SUBLANES, LANES), F32)],
        compiler_params=_cparams("arbitrary"),
        name="router",
    )(logits)


def _dispatch_kernel(p1_ref, p2_ref, pad0_ref, padn_ref, nt_ref, u_ref, xs_hbm, zbuf_ref, sem, zsem,
                     *, pad_bits, tail_per_tile):
    tm = u_ref.shape[0]

    @pl.when(pl.program_id(0) == 0)
    def _():
        zbuf_ref[...] = jnp.zeros_like(zbuf_ref)

        def pad_copies(e, b):
            n = padn_ref[e]
            off = pad0_ref[e] + (n & ((1 << b) - 1))
            if (1 << b) >= SUBLANES:
                off = pl.multiple_of(off, SUBLANES)
                return [pltpu.make_async_copy(zbuf_ref.at[pl.ds(0, 1 << b)],
                                              xs_hbm.at[pl.ds(off, 1 << b)], zsem)]
            return [pltpu.make_async_copy(zbuf_ref.at[pl.ds(0, 1)], xs_hbm.at[pl.ds(off + r, 1)], zsem)
                    for r in range(1 << b)]

        for wait in (False, True):
            for e in range(N_EXPERTS):
                for b in range(pad_bits):
                    @pl.when(((padn_ref[e] >> b) & 1) == 1)
                    def _():
                        for cp in pad_copies(e, b):
                            if wait:
                                cp.wait()
                            else:
                                cp.start()

        zrows = zbuf_ref.shape[0]

        def tail_copy(t):
            off = pl.multiple_of(t * zrows, SUBLANES)
            return pltpu.make_async_copy(zbuf_ref, xs_hbm.at[pl.ds(off, zrows)], zsem)

        def tail_start(t, carry):
            tail_copy(t).start()
            return carry

        def tail_wait(t, carry):
            tail_copy(t).wait()
            return carry

        first, last = nt_ref[0] * tail_per_tile, xs_hbm.shape[0] // zrows
        lax.fori_loop(first, last, tail_start, 0)
        lax.fori_loop(first, last, tail_wait, 0)

    def row_copy(r, pos):
        return pltpu.make_async_copy(u_ref.at[pl.ds(r, 1)], xs_hbm.at[pl.ds(pos, 1)], sem)

    def issue(b, carry):
        for rr in range(ISSUE_UNROLL):
            r = b * ISSUE_UNROLL + rr
            row_copy(r, p1_ref[r]).start()
            row_copy(r, p2_ref[r]).start()
        return carry

    lax.fori_loop(0, tm // ISSUE_UNROLL, issue, 0)
    for _ in range(2):
        pltpu.make_async_copy(u_ref, xs_hbm.at[pl.ds(0, tm)], sem).wait()


def _dispatch(pos1, pos2, pad_start, pad_len, n_tiles, un, P, group_tile, tm=512):
    T, D = un.shape
    tm = min(tm, T)
    pad_bits = group_tile.bit_length() - 1
    assert group_tile == 1 << pad_bits and pad_bits >= 1
    zrows = group_tile // 2
    kern = functools.partial(_dispatch_kernel, pad_bits=pad_bits, tail_per_tile=group_tile // zrows)
    smem = pl.BlockSpec(memory_space=pltpu.SMEM)
    return pl.pallas_call(
        kern,
        grid=(T // tm,),
        in_specs=[
            pl.BlockSpec((tm,), lambda i: (i,), memory_space=pltpu.SMEM),
            pl.BlockSpec((tm,), lambda i: (i,), memory_space=pltpu.SMEM),
            smem, smem, smem,
            pl.BlockSpec((tm, D), lambda i: (i, 0)),
        ],
        out_specs=pl.BlockSpec(memory_space=pl.ANY),
        out_shape=jax.ShapeDtypeStruct((P, D), un.dtype),
        scratch_shapes=[pltpu.VMEM((zrows, D), un.dtype),
                        pltpu.SemaphoreType.DMA(()), pltpu.SemaphoreType.DMA(())],
        compiler_params=_cparams("arbitrary"),
        name="dispatch",
    )(pos1, pos2, pad_start, pad_len, n_tiles, un)


def _combine_kernel(p1_ref, p2_ref, q1_ref, q2_ref, h_ref, rt_ref, g_ref, y_hbm, o_ref, buf_ref, sem,
                    *, final_norm):
    tm = h_ref.shape[0]
    i = pl.program_id(0)
    n = pl.num_programs(0)
    slot = i % 2

    def issue_tile(i1_ref, i2_ref, s):
        def row_copy(k, r, pos):
            return pltpu.make_async_copy(y_hbm.at[pl.ds(pos, 1)], buf_ref.at[s, k, pl.ds(r, 1)], sem.at[s])

        def issue(b, carry):
            for rr in range(ISSUE_UNROLL):
                r = b * ISSUE_UNROLL + rr
                row_copy(0, r, i1_ref[r]).start()
                row_copy(1, r, i2_ref[r]).start()
            return carry

        lax.fori_loop(0, tm // ISSUE_UNROLL, issue, 0)

    @pl.when(i == 0)
    def _():
        issue_tile(p1_ref, p2_ref, 0)

    @pl.when(i + 1 < n)
    def _():
        issue_tile(q1_ref, q2_ref, 1 - slot)

    for k in range(2):
        pltpu.make_async_copy(y_hbm.at[pl.ds(0, tm)], buf_ref.at[slot, k], sem.at[slot]).wait()

    rt = rt_ref[...]
    out = h_ref[...] + (rt[:, 2:3] * buf_ref[slot, 0] + rt[:, 3:4] * buf_ref[slot, 1])
    if final_norm:
        out = _rms_rows(out) * g_ref[...]
    o_ref[...] = out


def _combine(pos1, pos2, h, routing, g_final, y, final_norm, tm=256):
    T, D = h.shape
    tm = min(tm, T)
    n = T // tm
    kern = functools.partial(_combine_kernel, final_norm=final_norm)
    cur = pl.BlockSpec((tm,), lambda i: (i,), memory_space=pltpu.SMEM)
    nxt = pl.BlockSpec((tm,), lambda i: (jnp.minimum(i + 1, n - 1),), memory_space=pltpu.SMEM)
    return pl.pallas_call(
        kern,
        grid=(n,),
        in_specs=[
            cur, cur, nxt, nxt,
            pl.BlockSpec((tm, D), lambda i: (i, 0)),
            pl.BlockSpec((tm, LANES), lambda i: (i, 0)),
            pl.BlockSpec((1, D), lambda i: (0, 0)),
            pl.BlockSpec(memory_space=pl.ANY),
        ],
        out_specs=pl.BlockSpec((tm, D), lambda i: (i, 0)),
        out_shape=jax.ShapeDtypeStruct((T, D), F32),
        scratch_shapes=[pltpu.VMEM((2, 2, tm, D), F32), pltpu.SemaphoreType.DMA((2,))],
        compiler_params=_cparams("arbitrary"),
        name="combine",
    )(pos1, pos2, pos1, pos2, h, routing, g_final, y)


def _final_norm_kernel(h_ref, g_ref, o_ref):
    o_ref[...] = _rms_rows(h_ref[...]) * g_ref[...]


def _final_norm(h, g, tm=1024):
    T, D = h.shape
    tm = min(tm, T)
    return pl.pallas_call(
        _final_norm_kernel,
        grid=(T // tm,),
        in_specs=[pl.BlockSpec((tm, D), lambda i: (i, 0)), pl.BlockSpec((1, D), lambda i: (0, 0))],
        out_specs=pl.BlockSpec((tm, D), lambda i: (i, 0)),
        out_shape=jax.ShapeDtypeStruct((T, D), F32),
        compiler_params=_cparams("parallel"),
        name="final_norm",
    )(h, g)


def _block_diag(w):
    G, n, _ = w.shape
    eye = jnp.eye(G, dtype=w.dtype)
    return (eye[:, None, :, None] * w[:, :, None, :]).reshape(G * n, G * n)


def _moe_layer(h, un, logits, w_gate, w_up, w_down, first_expert, g_final, final_norm, tm):
    T, D = h.shape
    E = N_EXPERTS
    routing, routing_t, counts = _router(logits)
    e1 = routing_t[0].astype(jnp.int32)
    e2 = routing_t[1].astype(jnp.int32)
    rank1 = routing_t[4].astype(jnp.int32)
    rank2 = routing_t[5].astype(jnp.int32)
    cnt = counts[0, :E].astype(jnp.int32)
    tiles_per = (cnt + tm - 1) // tm
    tile_end = jnp.cumsum(tiles_per)
    offs = (tile_end - tiles_per) * tm
    pos1 = offs[e1] + rank1
    pos2 = offs[e2] + rank2
    n_slots = (2 * T) // tm + E
    P = n_slots * tm
    tile_expert = jnp.minimum(
        jnp.sum(jnp.arange(n_slots, dtype=jnp.int32)[:, None] >= tile_end[None, :], axis=1), E - 1
    ).astype(jnp.int32)
    n_tiles = tile_end[E - 1:E].astype(jnp.int32)
    tile_in_group = jnp.arange(n_slots, dtype=jnp.int32) - (tile_end - tiles_per)[tile_expert]
    tile_rows = jnp.clip(cnt[tile_expert] - tile_in_group * tm, 0, tm).astype(jnp.int32)

    xs = _dispatch(pos1, pos2, offs + cnt, tiles_per * tm - cnt, n_tiles, un, P, tm)
    y = _ffn_grouped(tile_expert, tile_rows, n_tiles, xs, w_gate, w_up, w_down, first_expert, tm)
    return _combine(pos1, pos2, h, routing, g_final, y, final_norm)


def _forward(x, norm_mix_g, w_in, ml_b_if, ml_norm_g, rg_conv_w, rg_conv_b, rg_w_a, rg_b_a,
             rg_w_x, rg_b_x, rg_lam, rg_norm_g, w_out, norm_ffn_g, ffn_w_gate, ffn_w_up,
             ffn_w_down, moe_w_router, moe_w_gate, moe_w_up, moe_w_down, norm_final_g,
             moe_tile=2048):
    B, S, D = x.shape
    T = B * S
    depth = w_in.shape[0]
    ml_w = ML_HEADS * ML_HEAD_DIM
    n_q = 4 * ml_w
    n_if = 2 * ML_HEADS
    rg_w = rg_lam.shape[1]
    n_r = 2 * rg_w

    h = x.reshape(T, D)
    g_final = norm_final_g.reshape(1, D)
    moe_gate_all = moe_w_gate.reshape((-1,) + moe_w_gate.shape[2:])
    moe_up_all = moe_w_up.reshape((-1,) + moe_w_up.shape[2:])
    moe_down_all = moe_w_down.reshape((-1,) + moe_w_down.shape[2:])
    for l in range(depth):
        wl = w_in[l]
        w_all = jnp.concatenate(
            [wl[:, :ml_w], wl[:, 2 * ml_w:n_q], wl[:, n_q + n_if:], wl[:, n_q:n_q + n_if],
             jnp.zeros((D, LANES - n_if), wl.dtype)], axis=1).astype(BF16)
        wt = jnp.concatenate([wl[:, ml_w:2 * ml_w], wl[:, n_q:n_q + n_if]], axis=1).T.astype(BF16)
        b_if = jnp.concatenate([ml_b_if[l], jnp.zeros((LANES - n_if,), F32)]).reshape(1, LANES)
        zq, kt, zr, zg, zgt = _inproj(h, norm_mix_g[l].reshape(1, D), b_if, ml_b_if[l].reshape(n_if, 1),
                                      w_all, wt, 3 * ml_w, n_r)
        yml = _mlstm(zq, kt, zg, zgt, ml_norm_g[l].reshape(1, ml_w), B, S)

        w_gates = jnp.concatenate([_block_diag(rg_w_a[l]), _block_diag(rg_w_x[l])], axis=1).astype(BF16)
        b_gates = jnp.concatenate([rg_b_a[l], rg_b_x[l]]).reshape(1, n_r)
        yrg = _rglru(zr, rg_conv_w[l], rg_conv_b[l].reshape(1, rg_w), w_gates, b_gates,
                     rg_lam[l].reshape(1, rg_w), rg_norm_g[l].reshape(1, rg_w), B, S)

        j = l // 2
        is_moe = (l % 2 == 1)
        w_router = None
        if is_moe:
            w_router = jnp.concatenate(
                [moe_w_router[j], jnp.zeros((D, LANES - N_EXPERTS), F32)], axis=1)
        res = _outproj(yml, yrg, h, w_out[l].astype(BF16), norm_ffn_g[l].reshape(1, D), w_router)
        h, un = res[0], res[1]
        last = (l == depth - 1)
        if is_moe:
            logits = res[2]
            h = _moe_layer(h, un, logits, moe_gate_all, moe_up_all, moe_down_all, j * N_EXPERTS,
                           g_final, last, moe_tile)
        else:
            h = _ffn_dense(un, h, ffn_w_gate, ffn_w_up, ffn_w_down, j)
            if last:
                h = _final_norm(h, g_final)
    return h.reshape(B, S, D)


def kernel(x, norm_mix_g, w_in, ml_b_if, ml_norm_g, rg_conv_w, rg_conv_b, rg_w_a, rg_b_a, rg_w_x,
           rg_b_x, rg_lam, rg_norm_g, w_out, norm_ffn_g, ffn_w_gate, ffn_w_up, ffn_w_down,
           moe_w_router, moe_w_gate, moe_w_up, moe_w_down, norm_final_g):
    return _forward(x, norm_mix_g, w_in, ml_b_if, ml_norm_g, rg_conv_w, rg_conv_b, rg_w_a, rg_b_a,
                    rg_w_x, rg_b_x, rg_lam, rg_norm_g, w_out, norm_ffn_g, ffn_w_gate, ffn_w_up,
                    ffn_w_down, moe_w_router, moe_w_gate, moe_w_up, moe_w_down, norm_final_g)
```

```python
import functools

import jax
import jax.numpy as jnp
from jax import lax
from jax.experimental import pallas as pl
from jax.experimental.pallas import tpu as pltpu

EPS = 1e-6
ML_HEADS = 4
ML_HEAD_DIM = 128
CHUNK = 128
RG_BLOCKS = 8
RG_C = 8.0
CONV_WIDTH = 4
N_EXPERTS = 8
LANES = 128
SUBLANES = 8
VMEM_LIMIT = 56 * 1024 * 1024
ISSUE_UNROLL = 8

BF16 = jnp.bfloat16
F32 = jnp.float32


def _cparams(*sem):
    return pltpu.CompilerParams(dimension_semantics=sem, vmem_limit_bytes=VMEM_LIMIT)


def _dot(a, b):
    return jnp.dot(a, b, preferred_element_type=F32)


def _dot_nt(a, b):
    return lax.dot_general(a, b, (((1,), (1,)), ((), ())), preferred_element_type=F32)


def _dot_tn(a, b):
    return lax.dot_general(a, b, (((0,), (0,)), ((), ())), preferred_element_type=F32)


def _dot_f32(a, b):
    return jnp.dot(a, b, preferred_element_type=F32, precision=lax.Precision.HIGHEST)


def _sigmoid(x):
    return 1.0 / (1.0 + jnp.exp(-x))


def _log_sigmoid(x):
    return jnp.minimum(x, 0.0) - jnp.log(1.0 + jnp.exp(-jnp.abs(x)))


def _softplus(x):
    return jnp.maximum(x, 0.0) + jnp.log(1.0 + jnp.exp(-jnp.abs(x)))


def _gelu_tanh(x):
    return 0.5 * x * (1.0 + jnp.tanh(0.7978845608028654 * (x + 0.044715 * (x * x * x))))


def _rms_rows(x):
    return x * lax.rsqrt(jnp.mean(x * x, axis=-1, keepdims=True) + EPS)


def _inproj_kernel(h_ref, g_ref, b_ref, bt_ref, w_ref, wt_ref, zq_ref, kt_ref, zr_ref, zg_ref, zgt_ref,
                   *, n_q, n_r, n_col):
    xn = (_rms_rows(h_ref[...]) * g_ref[...]).astype(BF16)
    for c0 in range(0, n_q, n_col):
        zq_ref[:, c0:c0 + n_col] = _dot(xn, w_ref[:, c0:c0 + n_col]).astype(BF16)
    for c0 in range(0, n_r, n_col):
        zr_ref[:, c0:c0 + n_col] = _dot(xn, w_ref[:, n_q + c0:n_q + c0 + n_col]).astype(BF16)
    zg_ref[...] = _dot(xn, w_ref[:, n_q + n_r:]) + b_ref[...]
    n_k = kt_ref.shape[1]
    t = _dot_nt(wt_ref[...], xn)
    for c in range(kt_ref.shape[0]):
        kt_ref[c] = t[:n_k, c * CHUNK:(c + 1) * CHUNK].astype(BF16)
        zgt_ref[c] = t[n_k:, c * CHUNK:(c + 1) * CHUNK] + bt_ref[...]


def _inproj(h, g, b_if, b_if_t, w_all, wt, n_q, n_r, tm=512):
    T, D = h.shape
    tm = min(tm, T)
    n_all = w_all.shape[1]
    n_g = n_all - n_q - n_r
    n_k = wt.shape[0] - SUBLANES
    kern = functools.partial(_inproj_kernel, n_q=n_q, n_r=n_r, n_col=512)
    const = lambda i: (0, 0)
    return pl.pallas_call(
        kern,
        grid=(T // tm,),
        in_specs=[
            pl.BlockSpec((tm, D), lambda i: (i, 0)),
            pl.BlockSpec((1, D), const),
            pl.BlockSpec((1, n_g), const),
            pl.BlockSpec((SUBLANES, 1), const),
            pl.BlockSpec((D, n_all), const),
            pl.BlockSpec((n_k + SUBLANES, D), const),
        ],
        out_specs=[
            pl.BlockSpec((tm, n_q), lambda i: (i, 0)),
            pl.BlockSpec((tm // CHUNK, n_k, CHUNK), lambda i: (i, 0, 0)),
            pl.BlockSpec((tm, n_r), lambda i: (i, 0)),
            pl.BlockSpec((tm, n_g), lambda i: (i, 0)),
            pl.BlockSpec((tm // CHUNK, SUBLANES, CHUNK), lambda i: (i, 0, 0)),
        ],
        out_shape=[
            jax.ShapeDtypeStruct((T, n_q), BF16),
            jax.ShapeDtypeStruct((T // CHUNK, n_k, CHUNK), BF16),
            jax.ShapeDtypeStruct((T, n_r), BF16),
            jax.ShapeDtypeStruct((T, n_g), F32),
            jax.ShapeDtypeStruct((T // CHUNK, SUBLANES, CHUNK), F32),
        ],
        compiler_params=_cparams("parallel"),
        name="inproj",
    )(h, g, b_if, b_if_t, w_all, wt)


def _mlstm_kernel(zq_ref, kt_ref, zg_ref, zgt_ref, gain_ref, y_ref, c_ref, m_ref, pcol_ref, prow_ref,
                  *, n_chunks):
    H, Dh, L = ML_HEADS, ML_HEAD_DIM, CHUNK
    W = H * Dh
    scale = Dh ** -0.5

    @pl.when(pl.program_id(1) == 0)
    def _():
        c_ref[...] = jnp.zeros_like(c_ref)
        m_ref[...] = jnp.zeros_like(m_ref)

    row = lax.broadcasted_iota(jnp.int32, (L, L), 0)
    col = lax.broadcasted_iota(jnp.int32, (L, L), 1)
    causal = col <= row
    tri_l = causal.astype(F32)
    tri_u = (row <= col).astype(F32)
    ones_blk = jnp.ones((L, Dh), BF16)

    def gate_prefix(c, slot):
        r0 = pl.multiple_of(c * L, L)
        gcol = zg_ref[pl.ds(r0, L), :]
        grow = zgt_ref[c]
        bcol_all = _dot_f32(tri_l, _log_sigmoid(gcol))
        brow_all = _dot_f32(_log_sigmoid(grow), tri_u)
        cm = gcol - pltpu.roll(bcol_all, LANES - H, 1)
        d = 1
        while d < L:
            cm = jnp.maximum(cm, jnp.where(row >= d, pltpu.roll(cm, d, 0), -jnp.inf))
            d *= 2
        pcol_ref[slot, 0] = bcol_all
        pcol_ref[slot, 1] = cm
        prow_ref[slot] = brow_all

    gate_prefix(0, 0)

    def chunk_body(c, carry):
        r0 = pl.multiple_of(c * L, L)
        slot = c % 2
        grow = zgt_ref[c]
        bcol_all = pcol_ref[slot, 0]
        cm = pcol_ref[slot, 1]
        brow_all = prow_ref[slot]

        for hd in range(H):
            q = zq_ref[pl.ds(r0, L), hd * Dh:(hd + 1) * Dh]
            v = zq_ref[pl.ds(r0, L), W + hd * Dh:W + (hd + 1) * Dh]
            o = zq_ref[pl.ds(r0, L), 2 * W + hd * Dh:2 * W + (hd + 1) * Dh]
            kt = kt_ref[c, hd * Dh:(hd + 1) * Dh, :]
            v_aug = jnp.concatenate([v, ones_blk], axis=1)

            li_row = grow[hd:hd + 1, :]
            b_row = brow_all[H + hd:H + hd + 1, :]
            g_tot = b_row[:, L - 1:L]
            r_row = li_row - b_row

            c_prev = c_ref[hd]
            m_prev = m_ref[hd]

            mx = jnp.maximum(jnp.broadcast_to(cm[:, hd:hd + 1], (L, L)), m_prev)
            bb = jnp.broadcast_to(bcol_all[:, H + hd:H + hd + 1], (L, L))
            p = jnp.exp(jnp.where(causal, r_row - mx, -jnp.inf))
            s_w = p * (_dot(q, kt) * scale)
            inter = jnp.exp(m_prev - mx)
            intra = _dot(s_w.astype(BF16), v_aug)
            cross = _dot(q, c_prev.astype(BF16))
            num = intra[:, :Dh] + inter * cross[:, :Dh]
            den = intra[:, Dh:] + inter * cross[:, Dh:]
            hh = num / jnp.maximum(jnp.abs(den), jnp.exp(-(bb + mx)))
            ht = _rms_rows(hh) * gain_ref[:, hd * Dh:(hd + 1) * Dh]
            y_ref[pl.ds(r0, L), hd * Dh:(hd + 1) * Dh] = (_sigmoid(o.astype(F32)) * ht).astype(BF16)

            a = g_tot + r_row
            m_loc = jnp.max(a, axis=1, keepdims=True)
            w = jnp.exp(a - m_loc) * scale
            c_loc = _dot((kt.astype(F32) * w).astype(BF16), v_aug)
            m_new = jnp.maximum(g_tot + m_prev, m_loc)
            s_old = jnp.exp(g_tot + m_prev - m_new)
            s_loc = jnp.exp(m_loc - m_new)
            c_ref[hd] = s_old * c_prev + s_loc * c_loc
            m_ref[hd] = m_new
        gate_prefix(jnp.minimum(c + 1, n_chunks - 1), 1 - slot)
        return carry

    lax.fori_loop(0, n_chunks, chunk_body, 0)


def _mlstm(zq, kt, zg, zgt, gain, B, S, rows=1024):
    T = B * S
    rows = min(rows, S)
    n_chunks = rows // CHUNK
    steps = S // rows
    W = ML_HEADS * ML_HEAD_DIM
    kern = functools.partial(_mlstm_kernel, n_chunks=n_chunks)
    return pl.pallas_call(
        kern,
        grid=(B, steps),
        in_specs=[
            pl.BlockSpec((rows, 3 * W), lambda b, s: (b * steps + s, 0)),
            pl.BlockSpec((n_chunks, W, CHUNK), lambda b, s: (b * steps + s, 0, 0)),
            pl.BlockSpec((rows, LANES), lambda b, s: (b * steps + s, 0)),
            pl.BlockSpec((n_chunks, SUBLANES, CHUNK), lambda b, s: (b * steps + s, 0, 0)),
            pl.BlockSpec((1, W), lambda b, s: (0, 0)),
        ],
        out_specs=pl.BlockSpec((rows, W), lambda b, s: (b * steps + s, 0)),
        out_shape=jax.ShapeDtypeStruct((T, W), BF16),
        scratch_shapes=[
            pltpu.VMEM((ML_HEADS, ML_HEAD_DIM, 2 * ML_HEAD_DIM), F32),
            pltpu.VMEM((ML_HEADS, 1, 1), F32),
            pltpu.VMEM((2, 2, CHUNK, LANES), F32),
            pltpu.VMEM((2, SUBLANES, CHUNK), F32),
        ],
        compiler_params=_cparams("parallel", "arbitrary"),
        name="mlstm",
    )(zq, kt, zg, zgt, gain)


def _rglru_kernel(zr_ref, cw_ref, cb_ref, wg_ref, bg_ref, lam_ref, gain_ref, y_ref,
                  nat_ref, ost_ref, a_ref, u_ref, p_ref, cx_ref, hc_ref, *, rows):
    Wd = y_ref.shape[1]
    NB = SUBLANES
    BL = rows // NB
    PITCH = BL + SUBLANES
    n_slab = Wd // LANES
    taps = CONV_WIDTH - 1

    @pl.when(pl.program_id(1) == 0)
    def _():
        cx_ref[...] = jnp.zeros_like(cx_ref)
        hc_ref[...] = jnp.zeros_like(hc_ref)

    for s in range(NB):
        blk = zr_ref[s * BL:(s + 1) * BL, :].astype(F32)
        for c in range(2 * n_slab):
            nat_ref[c, s * PITCH:s * PITCH + BL, :] = blk[:, c * LANES:(c + 1) * LANES]

    def interleaved(c):
        return jnp.concatenate([nat_ref[c, pl.ds(i, NB, stride=PITCH), :] for i in range(BL)], axis=0)

    first_sub = lax.broadcasted_iota(jnp.int32, (NB, LANES), 0) == 0
    xcs = []
    for c in range(n_slab):
        lanes = slice(c * LANES, (c + 1) * LANES)
        x = interleaved(c)
        head = []
        for k in range(taps, 0, -1):
            tail = x[(BL - k) * NB:(BL - k + 1) * NB, :]
            prev = cx_ref[c * taps + k - 1]
            head.append(jnp.where(first_sub, pltpu.roll(prev, 1, 0), pltpu.roll(tail, 1, 0)))
            cx_ref[c * taps + k - 1] = tail
        ext = jnp.concatenate(head + [x], axis=0)
        xc = cb_ref[:, lanes] + jnp.zeros((rows, LANES), F32)
        for j in range(CONV_WIDTH):
            xc = xc + cw_ref[j:j + 1, lanes] * ext[j * NB:j * NB + rows, :]
        xcs.append(xc)
    xc = jnp.concatenate(xcs, axis=1)

    gates = _dot(xc.astype(BF16), wg_ref[...]) + bg_ref[...]
    r = _sigmoid(gates[:, :Wd])
    ig = _sigmoid(gates[:, Wd:])
    a = jnp.exp(r * ((-RG_C) * _softplus(-lam_ref[...])))
    a_ref[...] = a
    d = 1.0 - a * a
    u_ref[...] = jnp.where(d > 0.0, d * lax.rsqrt(d), 0.0) * (ig * xc)

    for c in range(n_slab):
        lanes = slice(c * LANES, (c + 1) * LANES)
        h = jnp.zeros((NB, LANES), F32)
        p = jnp.ones((NB, LANES), F32)
        for i in range(BL):
            ai = a_ref[i * NB:(i + 1) * NB, lanes]
            h = ai * h + u_ref[i * NB:(i + 1) * NB, lanes]
            p = ai * p
            u_ref[i * NB:(i + 1) * NB, lanes] = h
            p_ref[i * NB:(i + 1) * NB, lanes] = p
    h_end = u_ref[(BL - 1) * NB:BL * NB, :]
    p_end = p_ref[(BL - 1) * NB:BL * NB, :]
    h_in = [hc_ref[0:1, :]]
    for s in range(NB):
        h_in.append(h_end[s:s + 1, :] + p_end[s:s + 1, :] * h_in[s])
    hc_ref[...] = jnp.broadcast_to(h_in[NB], hc_ref.shape)
    h_enter = jnp.concatenate(h_in[:NB], axis=0)
    h_all = u_ref[...] + p_ref[...] * jnp.concatenate([h_enter] * BL, axis=0)

    gate = jnp.concatenate([interleaved(n_slab + c) for c in range(n_slab)], axis=1)
    yv = h_all * _gelu_tanh(gate)
    yn = _rms_rows(yv) * gain_ref[...]
    for c in range(n_slab):
        for i in range(BL):
            ost_ref[c, pl.ds(i, NB, stride=PITCH), :] = yn[i * NB:(i + 1) * NB, c * LANES:(c + 1) * LANES]
    for s in range(NB):
        y_ref[s * BL:(s + 1) * BL, :] = jnp.concatenate(
            [ost_ref[c, s * PITCH:s * PITCH + BL, :] for c in range(n_slab)], axis=1).astype(BF16)


def _rglru(zr, conv_w, conv_b, w_gates, b_gates, lam, gain, B, S, rows=512):
    T = B * S
    Wd = zr.shape[1] // 2
    rows = min(rows, S)
    steps = S // rows
    pitch_rows = SUBLANES * (rows // SUBLANES + SUBLANES)
    n_slab = Wd // LANES
    kern = functools.partial(_rglru_kernel, rows=rows)
    const = lambda b, s: (0, 0)
    return pl.pallas_call(
        kern,
        grid=(B, steps),
        in_specs=[
            pl.BlockSpec((rows, 2 * Wd), lambda b, s: (b * steps + s, 0)),
            pl.BlockSpec((CONV_WIDTH, Wd), const),
            pl.BlockSpec((1, Wd), const),
            pl.BlockSpec((Wd, 2 * Wd), const),
            pl.BlockSpec((1, 2 * Wd), const),
            pl.BlockSpec((1, Wd), const),
            pl.BlockSpec((1, Wd), const),
        ],
        out_specs=pl.BlockSpec((rows, Wd), lambda b, s: (b * steps + s, 0)),
        out_shape=jax.ShapeDtypeStruct((T, Wd), BF16),
        scratch_shapes=[
            pltpu.VMEM((2 * n_slab, pitch_rows, LANES), F32),
            pltpu.VMEM((n_slab, pitch_rows, LANES), F32),
            pltpu.VMEM((rows, Wd), F32),
            pltpu.VMEM((rows, Wd), F32),
            pltpu.VMEM((rows, Wd), F32),
            pltpu.VMEM((n_slab * (CONV_WIDTH - 1), SUBLANES, LANES), F32),
            pltpu.VMEM((SUBLANES, Wd), F32),
        ],
        compiler_params=_cparams("parallel", "arbitrary"),
        name="rglru",
    )(zr, conv_w, conv_b, w_gates, b_gates, lam, gain)


def _outproj_kernel(yml_ref, yrg_ref, h_ref, w_ref, g_ref, *rest):
    half = yml_ref.shape[1]
    hn = (h_ref[...] + _dot(yml_ref[...], w_ref[:half, :]) + _dot(yrg_ref[...], w_ref[half:, :]))
    un = _rms_rows(hn) * g_ref[...]
    if len(rest) == 4:
        wr_ref, hn_ref, un_ref, lg_ref = rest
        wr = wr_ref[...]
        w_hi = wr.astype(BF16)
        w_lo = (wr - w_hi.astype(F32)).astype(BF16)
        u_hi = un.astype(BF16)
        u_lo = (un - u_hi.astype(F32)).astype(BF16)
        lg_ref[...] = _dot(u_hi, w_hi) + (_dot(u_hi, w_lo) + _dot(u_lo, w_hi))
    else:
        hn_ref, un_ref = rest
    hn_ref[...] = hn
    un_ref[...] = un.astype(un_ref.dtype)


def _outproj(yml, yrg, h, w_out, g_ffn, w_router, tm=512):
    T, D = h.shape
    half = yml.shape[1]
    tm = min(tm, T)
    with_router = w_router is not None
    row = lambda i: (i, 0)
    const = lambda i: (0, 0)
    in_specs = [
        pl.BlockSpec((tm, half), row),
        pl.BlockSpec((tm, half), row),
        pl.BlockSpec((tm, D), row),
        pl.BlockSpec((2 * half, D), const),
        pl.BlockSpec((1, D), const),
    ]
    out_specs = [pl.BlockSpec((tm, D), row), pl.BlockSpec((tm, D), row)]
    out_shape = [jax.ShapeDtypeStruct((T, D), F32),
                 jax.ShapeDtypeStruct((T, D), F32 if with_router else BF16)]
    args = [yml, yrg, h, w_out, g_ffn]
    if with_router:
        in_specs.append(pl.BlockSpec((D, LANES), const))
        out_specs.append(pl.BlockSpec((tm, LANES), row))
        out_shape.append(jax.ShapeDtypeStruct((T, LANES), F32))
        args.append(w_router)
    return pl.pallas_call(
        _outproj_kernel,
        grid=(T // tm,),
        in_specs=in_specs,
        out_specs=out_specs,
        out_shape=out_shape,
        compiler_params=_cparams("parallel"),
        name="outproj_router" if with_router else "outproj",
    )(*args)


W_SPLIT = 1
SUB_ROWS = 1024
SUB_STEP = 256


def _cast_slabs(parts, dst_ref):
    rows = dst_ref.shape[0] // len(parts)
    for q, part in enumerate(parts):
        dst_ref[q * rows:(q + 1) * rows, :] = part[0].astype(BF16)


def _swiglu_steps(j, nj, x_ref, init_fn, weights, scratch, o_ref, valid_fn):
    wg_parts, wu_parts, wd_parts = weights
    act_ref, wgb_ref, wub_ref, wdb_ref = scratch
    tm, D = o_ref.shape
    sub = min(SUB_ROWS, tm)

    for s in range(tm // sub):
        valid = valid_fn(s)
        sizes = [sub] if valid is None else list(range(SUB_STEP, sub, SUB_STEP)) + [sub]
        for z in sizes:
            rows = pl.ds(s * sub, z)

            def up(s=s, rows=rows):
                if s == 0:
                    _cast_slabs(wg_parts, wgb_ref)
                    _cast_slabs(wu_parts, wub_ref)
                x = x_ref[rows, :]
                g = _dot(x, wgb_ref[...])
                u = _dot(x, wub_ref[...])
                act_ref[rows, :] = (g * _sigmoid(g) * u).astype(BF16)

            def down(s=s, rows=rows):
                if s == 0:
                    _cast_slabs(wd_parts, wdb_ref)
                o_ref[rows, :] += _dot(act_ref[rows, :], wdb_ref[...])

            if valid is None:
                cond = lambda c: c
            else:
                on = valid > z - SUB_STEP
                if z < sub:
                    on = jnp.logical_and(on, valid <= z)
                cond = lambda c, on=on: jnp.logical_and(on, c)

            @pl.when(cond(j == 0))
            def _():
                up()
                o_ref[rows, :] = init_fn(rows)

            @pl.when(cond(jnp.logical_and(j > 0, j < nj)))
            def _():
                down()
                up()

            @pl.when(cond(j == nj))
            def _():
                down()
                if z < sub:
                    o_ref[pl.ds(s * sub + z, sub - z), :] = jnp.zeros((sub - z, D), o_ref.dtype)

        if valid is not None:
            @pl.when(jnp.logical_and(valid <= 0, j == nj))
            def _():
                o_ref[pl.ds(s * sub, sub), :] = jnp.zeros((sub, D), o_ref.dtype)


def _split_weights(refs):
    q = W_SPLIT
    return refs[:q], refs[q:2 * q], refs[2 * q:3 * q]


def _weight_specs(D, tf, up_map, down_map):
    q = W_SPLIT
    ups = [pl.BlockSpec((1, D // q, tf), functools.partial(up_map, s)) for s in range(q)]
    downs = [pl.BlockSpec((1, tf // q, D), functools.partial(down_map, s)) for s in range(q)]
    return ups + ups + downs


def _swiglu_scratch(tm, D, tf):
    return [pltpu.VMEM((tm, tf), BF16), pltpu.VMEM((D, tf), BF16), pltpu.VMEM((D, tf), BF16),
            pltpu.VMEM((tf, D), BF16)]


def _ffn_dense_kernel(u_ref, h_ref, *refs):
    weights = _split_weights(refs)
    o_ref = refs[3 * W_SPLIT]
    scratch = refs[3 * W_SPLIT + 1:]
    j = pl.program_id(1)
    nj = pl.num_programs(1) - 1
    _swiglu_steps(j, nj, u_ref, lambda rows: h_ref[rows, :], weights, scratch, o_ref, lambda s: None)


def _ffn_dense(un, h, wg, wu, wd, layer, tm=2048, tf=256):
    T, D = un.shape
    F = wg.shape[2]
    tm = min(tm, T)
    tf = min(tf, F)
    nj = F // tf
    q = W_SPLIT
    up = lambda s, i, j: (layer, s, jnp.minimum(j, nj - 1))
    down = lambda s, i, j: (layer, jnp.maximum(j - 1, 0) * q + s, 0)
    row = lambda i, j: (i, 0)
    return pl.pallas_call(
        _ffn_dense_kernel,
        grid=(T // tm, nj + 1),
        in_specs=[pl.BlockSpec((tm, D), row), pl.BlockSpec((tm, D), row)] + _weight_specs(D, tf, up, down),
        out_specs=pl.BlockSpec((tm, D), row),
        out_shape=jax.ShapeDtypeStruct((T, D), F32),
        scratch_shapes=_swiglu_scratch(tm, D, tf),
        compiler_params=_cparams("parallel", "arbitrary"),
        name="ffn_dense",
    )(un, h, *([wg] * q + [wu] * q + [wd] * q))


def _ffn_grouped_kernel(te_ref, tr_ref, nt_ref, x_ref, *refs):
    weights = _split_weights(refs)
    o_ref, xb_ref = refs[3 * W_SPLIT], refs[3 * W_SPLIT + 1]
    scratch = refs[3 * W_SPLIT + 2:]
    i = pl.program_id(0)
    j = pl.program_id(1)
    nj = pl.num_programs(1) - 1
    sub = min(SUB_ROWS, o_ref.shape[0])
    used = i < nt_ref[0]

    @pl.when(jnp.logical_and(used, j == 0))
    def _():
        xb_ref[...] = x_ref[...].astype(BF16)

    _swiglu_steps(j, nj, xb_ref, lambda rows: jnp.zeros((rows.size, o_ref.shape[1]), F32), weights, scratch,
                  o_ref, lambda s: jnp.where(used, tr_ref[i] - s * sub, 0))


def _ffn_grouped(tile_expert, tile_rows, n_tiles, xs, wg, wu, wd, first_expert, tm, tf=256):
    P, D = xs.shape
    F = wg.shape[2]
    tf = min(tf, F)
    nj = F // tf
    q = W_SPLIT

    def up(s, i, j, te, tr, nt):
        return (first_expert + te[i], s, jnp.where(i < nt[0], jnp.minimum(j, nj - 1), nj - 1))

    def down(s, i, j, te, tr, nt):
        return (first_expert + te[i], jnp.where(i < nt[0], jnp.maximum(j - 1, 0), nj - 1) * q + s, 0)

    grid_spec = pltpu.PrefetchScalarGridSpec(
        num_scalar_prefetch=3,
        grid=(P // tm, nj + 1),
        in_specs=[pl.BlockSpec((tm, D), lambda i, j, te, tr, nt: (jnp.minimum(i, nt[0] - 1), 0))]
        + _weight_specs(D, tf, up, down),
        out_specs=pl.BlockSpec((tm, D), lambda i, j, te, tr, nt: (i, 0)),
        scratch_shapes=[pltpu.VMEM((tm, D), BF16)] + _swiglu_scratch(tm, D, tf),
    )
    return pl.pallas_call(
        _ffn_grouped_kernel,
        grid_spec=grid_spec,
        out_shape=jax.ShapeDtypeStruct((P, D), F32),
        compiler_params=_cparams("arbitrary", "arbitrary"),
        name="ffn_grouped",
    )(tile_expert, tile_rows, n_tiles, xs, *([wg] * q + [wu] * q + [wd] * q))


def _router_kernel(lg_ref, rt_ref, rtt_ref, cnt_ref, carry_ref):
    tm = lg_ref.shape[0]

    @pl.when(pl.program_id(0) == 0)
    def _():
        carry_ref[...] = jnp.zeros_like(carry_ref)

    lane = lax.broadcasted_iota(jnp.int32, (tm, LANES), 1).astype(F32)
    lg = jnp.where(lane < N_EXPERTS, lg_ref[...], -jnp.inf)
    v1 = jnp.max(lg, axis=1, keepdims=True)
    e1 = jnp.min(jnp.where(lg == v1, lane, float(LANES)), axis=1, keepdims=True)
    lg2 = jnp.where(lane == e1, -jnp.inf, lg)
    v2 = jnp.max(lg2, axis=1, keepdims=True)
    e2 = jnp.min(jnp.where(lg2 == v2, lane, float(LANES)), axis=1, keepdims=True)
    ex = jnp.exp(v2 - v1)
    w1 = 1.0 / (1.0 + ex)
    w2 = ex / (1.0 + ex)

    oh1 = (lane == e1).astype(F32)
    oh2 = (lane == e2).astype(F32)
    r_i = lax.broadcasted_iota(jnp.int32, (tm, tm), 0)
    c_i = lax.broadcasted_iota(jnp.int32, (tm, tm), 1)
    strict = (c_i < r_i).astype(BF16)
    before = _dot(strict, (oh1 + oh2).astype(BF16)) + carry_ref[0:1, :]
    rank1 = jnp.sum(before * oh1, axis=1, keepdims=True)
    rank2 = jnp.sum(before * oh2, axis=1, keepdims=True)
    total = carry_ref[0:1, :] + jnp.sum(oh1 + oh2, axis=0, keepdims=True)
    carry_ref[...] = jnp.broadcast_to(total, carry_ref.shape)
    cnt_ref[...] = jnp.broadcast_to(total, cnt_ref.shape)

    out = jnp.where(lane == 0, e1, 0.0)
    out = jnp.where(lane == 1, e2, out)
    out = jnp.where(lane == 2, w1, out)
    out = jnp.where(lane == 3, w2, out)
    out = jnp.where(lane == 4, rank1, out)
    out = jnp.where(lane == 5, rank2, out)
    rt_ref[...] = out
    rtt_ref[...] = jnp.transpose(out)[:SUBLANES, :]


def _router(logits, tm=512):
    T = logits.shape[0]
    tm = min(tm, T)
    return pl.pallas_call(
        _router_kernel,
        grid=(T // tm,),
        in_specs=[pl.BlockSpec((tm, LANES), lambda i: (i, 0))],
        out_specs=[
            pl.BlockSpec((tm, LANES), lambda i: (i, 0)),
            pl.BlockSpec((SUBLANES, tm), lambda i: (0, i)),
            pl.BlockSpec((SUBLANES, LANES), lambda i: (0, 0)),
        ],
        out_shape=[
            jax.ShapeDtypeStruct((T, LANES), F32),
            jax.ShapeDtypeStruct((SUBLANES, T), F32),
            jax.ShapeDtypeStruct((SUBLANES, LANES), F32),
        ],
        scratch_shapes=[pltpu.VMEM((SUBLANES, LANES), F32)],
        compiler_params=_cparams("arbitrary"),
        name="router",
    )(logits)


def _dispatch_kernel(p1_ref, p2_ref, pad0_ref, padn_ref, nt_ref, u_ref, xs_hbm, zbuf_ref, sem, zsem,
                     *, pad_bits, tail_per_tile):
    tm = u_ref.shape[0]

    @pl.when(pl.program_id(0) == 0)
    def _():
        zbuf_ref[...] = jnp.zeros_like(zbuf_ref)

        def pad_copies(e, b):
            n = padn_ref[e]
            off = pad0_ref[e] + (n & ((1 << b) - 1))
            if (1 << b) >= SUBLANES:
                off = pl.multiple_of(off, SUBLANES)
                return [pltpu.make_async_copy(zbuf_ref.at[pl.ds(0, 1 << b)],
                                              xs_hbm.at[pl.ds(off, 1 << b)], zsem)]
            return [pltpu.make_async_copy(zbuf_ref.at[pl.ds(0, 1)], xs_hbm.at[pl.ds(off + r, 1)], zsem)
                    for r in range(1 << b)]

        for wait in (False, True):
            for e in range(N_EXPERTS):
                for b in range(pad_bits):
                    @pl.when(((padn_ref[e] >> b) & 1) == 1)
                    def _():
                        for cp in pad_copies(e, b):
                            if wait:
                                cp.wait()
                            else:
                                cp.start()

        zrows = zbuf_ref.shape[0]

        def tail_copy(t):
            off = pl.multiple_of(t * zrows, SUBLANES)
            return pltpu.make_async_copy(zbuf_ref, xs_hbm.at[pl.ds(off, zrows)], zsem)

        def tail_start(t, carry):
            tail_copy(t).start()
            return carry

        def tail_wait(t, carry):
            tail_copy(t).wait()
            return carry

        first, last = nt_ref[0] * tail_per_tile, xs_hbm.shape[0] // zrows
        lax.fori_loop(first, last, tail_start, 0)
        lax.fori_loop(first, last, tail_wait, 0)

    def row_copy(r, pos):
        return pltpu.make_async_copy(u_ref.at[pl.ds(r, 1)], xs_hbm.at[pl.ds(pos, 1)], sem)

    def issue(b, carry):
        for rr in range(ISSUE_UNROLL):
            r = b * ISSUE_UNROLL + rr
            row_copy(r, p1_ref[r]).start()
            row_copy(r, p2_ref[r]).start()
        return carry

    lax.fori_loop(0, tm // ISSUE_UNROLL, issue, 0)
    for _ in range(2):
        pltpu.make_async_copy(u_ref, xs_hbm.at[pl.ds(0, tm)], sem).wait()


def _dispatch(pos1, pos2, pad_start, pad_len, n_tiles, un, P, group_tile, tm=512):
    T, D = un.shape
    tm = min(tm, T)
    pad_bits = group_tile.bit_length() - 1
    assert group_tile == 1 << pad_bits and pad_bits >= 1
    zrows = group_tile // 2
    kern = functools.partial(_dispatch_kernel, pad_bits=pad_bits, tail_per_tile=group_tile // zrows)
    smem = pl.BlockSpec(memory_space=pltpu.SMEM)
    return pl.pallas_call(
        kern,
        grid=(T // tm,),
        in_specs=[
            pl.BlockSpec((tm,), lambda i: (i,), memory_space=pltpu.SMEM),
            pl.BlockSpec((tm,), lambda i: (i,), memory_space=pltpu.SMEM),
            smem, smem, smem,
            pl.BlockSpec((tm, D), lambda i: (i, 0)),
        ],
        out_specs=pl.BlockSpec(memory_space=pl.ANY),
        out_shape=jax.ShapeDtypeStruct((P, D), un.dtype),
        scratch_shapes=[pltpu.VMEM((zrows, D), un.dtype),
                        pltpu.SemaphoreType.DMA(()), pltpu.SemaphoreType.DMA(())],
        compiler_params=_cparams("arbitrary"),
        name="dispatch",
    )(pos1, pos2, pad_start, pad_len, n_tiles, un)


def _combine_kernel(p1_ref, p2_ref, q1_ref, q2_ref, h_ref, rt_ref, g_ref, y_hbm, o_ref, buf_ref, sem,
                    *, final_norm):
    tm = h_ref.shape[0]
    i = pl.program_id(0)
    n = pl.num_programs(0)
    slot = i % 2

    def issue_tile(i1_ref, i2_ref, s):
        def row_copy(k, r, pos):
            return pltpu.make_async_copy(y_hbm.at[pl.ds(pos, 1)], buf_ref.at[s, k, pl.ds(r, 1)], sem.at[s])

        def issue(b, carry):
            for rr in range(ISSUE_UNROLL):
                r = b * ISSUE_UNROLL + rr
                row_copy(0, r, i1_ref[r]).start()
                row_copy(1, r, i2_ref[r]).start()
            return carry

        lax.fori_loop(0, tm // ISSUE_UNROLL, issue, 0)

    @pl.when(i == 0)
    def _():
        issue_tile(p1_ref, p2_ref, 0)

    @pl.when(i + 1 < n)
    def _():
        issue_tile(q1_ref, q2_ref, 1 - slot)

    for k in range(2):
        pltpu.make_async_copy(y_hbm.at[pl.ds(0, tm)], buf_ref.at[slot, k], sem.at[slot]).wait()

    rt = rt_ref[...]
    out = h_ref[...] + (rt[:, 2:3] * buf_ref[slot, 0] + rt[:, 3:4] * buf_ref[slot, 1])
    if final_norm:
        out = _rms_rows(out) * g_ref[...]
    o_ref[...] = out


def _combine(pos1, pos2, h, routing, g_final, y, final_norm, tm=256):
    T, D = h.shape
    tm = min(tm, T)
    n = T // tm
    kern = functools.partial(_combine_kernel, final_norm=final_norm)
    cur = pl.BlockSpec((tm,), lambda i: (i,), memory_space=pltpu.SMEM)
    nxt = pl.BlockSpec((tm,), lambda i: (jnp.minimum(i + 1, n - 1),), memory_space=pltpu.SMEM)
    return pl.pallas_call(
        kern,
        grid=(n,),
        in_specs=[
            cur, cur, nxt, nxt,
            pl.BlockSpec((tm, D), lambda i: (i, 0)),
            pl.BlockSpec((tm, LANES), lambda i: (i, 0)),
            pl.BlockSpec((1, D), lambda i: (0, 0)),
            pl.BlockSpec(memory_space=pl.ANY),
        ],
        out_specs=pl.BlockSpec((tm, D), lambda i: (i, 0)),
        out_shape=jax.ShapeDtypeStruct((T, D), F32),
        scratch_shapes=[pltpu.VMEM((2, 2, tm, D), F32), pltpu.SemaphoreType.DMA((2,))],
        compiler_params=_cparams("arbitrary"),
        name="combine",
    )(pos1, pos2, pos1, pos2, h, routing, g_final, y)


def _final_norm_kernel(h_ref, g_ref, o_ref):
    o_ref[...] = _rms_rows(h_ref[...]) * g_ref[...]


def _final_norm(h, g, tm=1024):
    T, D = h.shape
    tm = min(tm, T)
    return pl.pallas_call(
        _final_norm_kernel,
        grid=(T // tm,),
        in_specs=[pl.BlockSpec((tm, D), lambda i: (i, 0)), pl.BlockSpec((1, D), lambda i: (0, 0))],
        out_specs=pl.BlockSpec((tm, D), lambda i: (i, 0)),
        out_shape=jax.ShapeDtypeStruct((T, D), F32),
        compiler_params=_cparams("parallel"),
        name="final_norm",
    )(h, g)


def _block_diag(w):
    G, n, _ = w.shape
    eye = jnp.eye(G, dtype=w.dtype)
    return (eye[:, None, :, None] * w[:, :, None, :]).reshape(G * n, G * n)


def _moe_layer(h, un, logits, w_gate, w_up, w_down, first_expert, g_final, final_norm, tm):
    T, D = h.shape
    E = N_EXPERTS
    routing, routing_t, counts = _router(logits)
    e1 = routing_t[0].astype(jnp.int32)
    e2 = routing_t[1].astype(jnp.int32)
    rank1 = routing_t[4].astype(jnp.int32)
    rank2 = routing_t[5].astype(jnp.int32)
    cnt = counts[0, :E].astype(jnp.int32)
    tiles_per = (cnt + tm - 1) // tm
    tile_end = jnp.cumsum(tiles_per)
    offs = (tile_end - tiles_per) * tm
    pos1 = offs[e1] + rank1
    pos2 = offs[e2] + rank2
    n_slots = (2 * T) // tm + E
    P = n_slots * tm
    tile_expert = jnp.minimum(
        jnp.sum(jnp.arange(n_slots, dtype=jnp.int32)[:, None] >= tile_end[None, :], axis=1), E - 1
    ).astype(jnp.int32)
    n_tiles = tile_end[E - 1:E].astype(jnp.int32)
    tile_in_group = jnp.arange(n_slots, dtype=jnp.int32) - (tile_end - tiles_per)[tile_expert]
    tile_rows = jnp.clip(cnt[tile_expert] - tile_in_group * tm, 0, tm).astype(jnp.int32)

    xs = _dispatch(pos1, pos2, offs + cnt, tiles_per * tm - cnt, n_tiles, un, P, tm)
    y = _ffn_grouped(tile_expert, tile_rows, n_tiles, xs, w_gate, w_up, w_down, first_expert, tm)
    return _combine(pos1, pos2, h, routing, g_final, y, final_norm)


def _forward(x, norm_mix_g, w_in, ml_b_if, ml_norm_g, rg_conv_w, rg_conv_b, rg_w_a, rg_b_a,
             rg_w_x, rg_b_x, rg_lam, rg_norm_g, w_out, norm_ffn_g, ffn_w_gate, ffn_w_up,
             ffn_w_down, moe_w_router, moe_w_gate, moe_w_up, moe_w_down, norm_final_g,
             moe_tile=2048):
    B, S, D = x.shape
    T = B * S
    depth = w_in.shape[0]
    ml_w = ML_HEADS * ML_HEAD_DIM
    n_q = 4 * ml_w
    n_if = 2 * ML_HEADS
    rg_w = rg_lam.shape[1]
    n_r = 2 * rg_w

    h = x.reshape(T, D)
    g_final = norm_final_g.reshape(1, D)
    moe_gate_all = moe_w_gate.reshape((-1,) + moe_w_gate.shape[2:])
    moe_up_all = moe_w_up.reshape((-1,) + moe_w_up.shape[2:])
    moe_down_all = moe_w_down.reshape((-1,) + moe_w_down.shape[2:])
    for l in range(depth):
        wl = w_in[l]
        w_all = jnp.concatenate(
            [wl[:, :ml_w], wl[:, 2 * ml_w:n_q], wl[:, n_q + n_if:], wl[:, n_q:n_q + n_if],
             jnp.zeros((D, LANES - n_if), wl.dtype)], axis=1).astype(BF16)
        wt = jnp.concatenate([wl[:, ml_w:2 * ml_w], wl[:, n_q:n_q + n_if]], axis=1).T.astype(BF16)
        b_if = jnp.concatenate([ml_b_if[l], jnp.zeros((LANES - n_if,), F32)]).reshape(1, LANES)
        zq, kt, zr, zg, zgt = _inproj(h, norm_mix_g[l].reshape(1, D), b_if, ml_b_if[l].reshape(n_if, 1),
                                      w_all, wt, 3 * ml_w, n_r)
        yml = _mlstm(zq, kt, zg, zgt, ml_norm_g[l].reshape(1, ml_w), B, S)

        w_gates = jnp.concatenate([_block_diag(rg_w_a[l]), _block_diag(rg_w_x[l])], axis=1).astype(BF16)
        b_gates = jnp.concatenate([rg_b_a[l], rg_b_x[l]]).reshape(1, n_r)
        yrg = _rglru(zr, rg_conv_w[l], rg_conv_b[l].reshape(1, rg_w), w_gates, b_gates,
                     rg_lam[l].reshape(1, rg_w), rg_norm_g[l].reshape(1, rg_w), B, S)

        j = l // 2
        is_moe = (l % 2 == 1)
        w_router = None
        if is_moe:
            w_router = jnp.concatenate(
                [moe_w_router[j], jnp.zeros((D, LANES - N_EXPERTS), F32)], axis=1)
        res = _outproj(yml, yrg, h, w_out[l].astype(BF16), norm_ffn_g[l].reshape(1, D), w_router)
        h, un = res[0], res[1]
        last = (l == depth - 1)
        if is_moe:
            logits = res[2]
            h = _moe_layer(h, un, logits, moe_gate_all, moe_up_all, moe_down_all, j * N_EXPERTS,
                           g_final, last, moe_tile)
        else:
            h = _ffn_dense(un, h, ffn_w_gate, ffn_w_up, ffn_w_down, j)
            if last:
                h = _final_norm(h, g_final)
    return h.reshape(B, S, D)


def kernel(x, norm_mix_g, w_in, ml_b_if, ml_norm_g, rg_conv_w, rg_conv_b, rg_w_a, rg_b_a, rg_w_x,
           rg_b_x, rg_lam, rg_norm_g, w_out, norm_ffn_g, ffn_w_gate, ffn_w_up, ffn_w_down,
           moe_w_router, moe_w_gate, moe_w_up, moe_w_down, norm_final_g):
    return _forward(x, norm_mix_g, w_in, ml_b_if, ml_norm_g, rg_conv_w, rg_conv_b, rg_w_a, rg_b_a,
                    rg_w_x, rg_b_x, rg_lam, rg_norm_g, w_out, norm_ffn_g, ffn_w_gate, ffn_w_up,
                    ffn_w_down, moe_w_router, moe_w_gate, moe_w_up, moe_w_down, norm_final_g)
```

```python
import functools

import jax
import jax.numpy as jnp
from jax import lax
from jax.experimental import pallas as pl
from jax.experimental.pallas import tpu as pltpu

EPS = 1e-6
ML_HEADS = 4
ML_HEAD_DIM = 128
CHUNK = 128
RG_BLOCKS = 8
RG_C = 8.0
CONV_WIDTH = 4
N_EXPERTS = 8
LANES = 128
SUBLANES = 8
VMEM_LIMIT = 60 * 1024 * 1024
ISSUE_UNROLL = 8

BF16 = jnp.bfloat16
F32 = jnp.float32


def _cparams(*sem):
    return pltpu.CompilerParams(dimension_semantics=sem, vmem_limit_bytes=VMEM_LIMIT)


def _dot(a, b):
    return jnp.dot(a, b, preferred_element_type=F32)


def _dot_nt(a, b):
    return lax.dot_general(a, b, (((1,), (1,)), ((), ())), preferred_element_type=F32)


def _dot_tn(a, b):
    return lax.dot_general(a, b, (((0,), (0,)), ((), ())), preferred_element_type=F32)


def _dot_f32(a, b):
    return jnp.dot(a, b, preferred_element_type=F32, precision=lax.Precision.HIGHEST)


def _sigmoid(x):
    return 1.0 / (1.0 + jnp.exp(-x))


def _log_sigmoid(x):
    return jnp.minimum(x, 0.0) - jnp.log(1.0 + jnp.exp(-jnp.abs(x)))


def _softplus(x):
    return jnp.maximum(x, 0.0) + jnp.log(1.0 + jnp.exp(-jnp.abs(x)))


def _gelu_tanh(x):
    return 0.5 * x * (1.0 + jnp.tanh(0.7978845608028654 * (x + 0.044715 * (x * x * x))))


def _rms_rows(x):
    return x * lax.rsqrt(jnp.mean(x * x, axis=-1, keepdims=True) + EPS)


def _inproj_kernel(h_ref, g_ref, b_ref, bt_ref, w_ref, wt_ref, zq_ref, kt_ref, zr_ref, zg_ref, zgt_ref,
                   *, n_q, n_r, n_col):
    xn = (_rms_rows(h_ref[...]) * g_ref[...]).astype(BF16)
    for c0 in range(0, n_q, n_col):
        zq_ref[:, c0:c0 + n_col] = _dot(xn, w_ref[:, c0:c0 + n_col]).astype(BF16)
    for c0 in range(0, n_r, n_col):
        zr_ref[:, c0:c0 + n_col] = _dot(xn, w_ref[:, n_q + c0:n_q + c0 + n_col]).astype(BF16)
    zg_ref[...] = _dot(xn, w_ref[:, n_q + n_r:]) + b_ref[...]
    n_k = kt_ref.shape[1]
    t = _dot_nt(wt_ref[...], xn)
    for c in range(kt_ref.shape[0]):
        kt_ref[c] = t[:n_k, c * CHUNK:(c + 1) * CHUNK].astype(BF16)
        zgt_ref[c] = t[n_k:, c * CHUNK:(c + 1) * CHUNK] + bt_ref[...]


def _inproj(h, g, b_if, b_if_t, w_all, wt, n_q, n_r, tm=512):
    T, D = h.shape
    tm = min(tm, T)
    n_all = w_all.shape[1]
    n_g = n_all - n_q - n_r
    n_k = wt.shape[0] - SUBLANES
    kern = functools.partial(_inproj_kernel, n_q=n_q, n_r=n_r, n_col=512)
    const = lambda i: (0, 0)
    return pl.pallas_call(
        kern,
        grid=(T // tm,),
        in_specs=[
            pl.BlockSpec((tm, D), lambda i: (i, 0)),
            pl.BlockSpec((1, D), const),
            pl.BlockSpec((1, n_g), const),
            pl.BlockSpec((SUBLANES, 1), const),
            pl.BlockSpec((D, n_all), const),
            pl.BlockSpec((n_k + SUBLANES, D), const),
        ],
        out_specs=[
            pl.BlockSpec((tm, n_q), lambda i: (i, 0)),
            pl.BlockSpec((tm // CHUNK, n_k, CHUNK), lambda i: (i, 0, 0)),
            pl.BlockSpec((tm, n_r), lambda i: (i, 0)),
            pl.BlockSpec((tm, n_g), lambda i: (i, 0)),
            pl.BlockSpec((tm // CHUNK, SUBLANES, CHUNK), lambda i: (i, 0, 0)),
        ],
        out_shape=[
            jax.ShapeDtypeStruct((T, n_q), BF16),
            jax.ShapeDtypeStruct((T // CHUNK, n_k, CHUNK), BF16),
            jax.ShapeDtypeStruct((T, n_r), BF16),
            jax.ShapeDtypeStruct((T, n_g), F32),
            jax.ShapeDtypeStruct((T // CHUNK, SUBLANES, CHUNK), F32),
        ],
        compiler_params=_cparams("parallel"),
        name="inproj",
    )(h, g, b_if, b_if_t, w_all, wt)


def _mlstm_kernel(zq_ref, kt_ref, zg_ref, zgt_ref, gain_ref, y_ref, c_ref, m_ref, pcol_ref, prow_ref,
                  *, n_chunks):
    H, Dh, L = ML_HEADS, ML_HEAD_DIM, CHUNK
    W = H * Dh
    scale = Dh ** -0.5

    @pl.when(pl.program_id(1) == 0)
    def _():
        c_ref[...] = jnp.zeros_like(c_ref)
        m_ref[...] = jnp.zeros_like(m_ref)

    row = lax.broadcasted_iota(jnp.int32, (L, L), 0)
    col = lax.broadcasted_iota(jnp.int32, (L, L), 1)
    causal = col <= row
    tri_l = causal.astype(F32)
    tri_u = (row <= col).astype(F32)
    ones_blk = jnp.ones((L, Dh), BF16)

    def gate_prefix(c, slot):
        r0 = pl.multiple_of(c * L, L)
        gcol = zg_ref[pl.ds(r0, L), :]
        grow = zgt_ref[c]
        bcol_all = _dot_f32(tri_l, _log_sigmoid(gcol))
        brow_all = _dot_f32(_log_sigmoid(grow), tri_u)
        cm = gcol - pltpu.roll(bcol_all, LANES - H, 1)
        d = 1
        while d < L:
            cm = jnp.maximum(cm, jnp.where(row >= d, pltpu.roll(cm, d, 0), -jnp.inf))
            d *= 2
        pcol_ref[slot, 0] = bcol_all
        pcol_ref[slot, 1] = cm
        prow_ref[slot] = brow_all

    gate_prefix(0, 0)

    def chunk_body(c, carry):
        r0 = pl.multiple_of(c * L, L)
        slot = c % 2
        grow = zgt_ref[c]
        bcol_all = pcol_ref[slot, 0]
        cm = pcol_ref[slot, 1]
        brow_all = prow_ref[slot]

        for hd in range(H):
            q = zq_ref[pl.ds(r0, L), hd * Dh:(hd + 1) * Dh]
            v = zq_ref[pl.ds(r0, L), W + hd * Dh:W + (hd + 1) * Dh]
            o = zq_ref[pl.ds(r0, L), 2 * W + hd * Dh:2 * W + (hd + 1) * Dh]
            kt = kt_ref[c, hd * Dh:(hd + 1) * Dh, :]
            v_aug = jnp.concatenate([v, ones_blk], axis=1)

            li_row = grow[hd:hd + 1, :]
            b_row = brow_all[H + hd:H + hd + 1, :]
            g_tot = b_row[:, L - 1:L]
            r_row = li_row - b_row

            c_prev = c_ref[hd]
            m_prev = m_ref[hd]

            mx = jnp.maximum(jnp.broadcast_to(cm[:, hd:hd + 1], (L, L)), m_prev)
            bb = jnp.broadcast_to(bcol_all[:, H + hd:H + hd + 1], (L, L))
            p = jnp.exp(jnp.where(causal, r_row - mx, -jnp.inf))
            s_w = p * (_dot(q, kt) * scale)
            inter = jnp.exp(m_prev - mx)
            intra = _dot(s_w.astype(BF16), v_aug)
            cross = _dot(q, c_prev.astype(BF16))
            num = intra[:, :Dh] + inter * cross[:, :Dh]
            den = intra[:, Dh:] + inter * cross[:, Dh:]
            hh = num / jnp.maximum(jnp.abs(den), jnp.exp(-(bb + mx)))
            ht = _rms_rows(hh) * gain_ref[:, hd * Dh:(hd + 1) * Dh]
            y_ref[pl.ds(r0, L), hd * Dh:(hd + 1) * Dh] = (_sigmoid(o.astype(F32)) * ht).astype(BF16)

            a = g_tot + r_row
            m_loc = jnp.max(a, axis=1, keepdims=True)
            w = jnp.exp(a - m_loc) * scale
            c_loc = _dot((kt.astype(F32) * w).astype(BF16), v_aug)
            m_new = jnp.maximum(g_tot + m_prev, m_loc)
            s_old = jnp.exp(g_tot + m_prev - m_new)
            s_loc = jnp.exp(m_loc - m_new)
            c_ref[hd] = s_old * c_prev + s_loc * c_loc
            m_ref[hd] = m_new
        gate_prefix(jnp.minimum(c + 1, n_chunks - 1), 1 - slot)
        return carry

    lax.fori_loop(0, n_chunks, chunk_body, 0)


def _mlstm(zq, kt, zg, zgt, gain, B, S, rows=1024):
    T = B * S
    rows = min(rows, S)
    n_chunks = rows // CHUNK
    steps = S // rows
    W = ML_HEADS * ML_HEAD_DIM
    kern = functools.partial(_mlstm_kernel, n_chunks=n_chunks)
    return pl.pallas_call(
        kern,
        grid=(B, steps),
        in_specs=[
            pl.BlockSpec((rows, 3 * W), lambda b, s: (b * steps + s, 0)),
            pl.BlockSpec((n_chunks, W, CHUNK), lambda b, s: (b * steps + s, 0, 0)),
            pl.BlockSpec((rows, LANES), lambda b, s: (b * steps + s, 0)),
            pl.BlockSpec((n_chunks, SUBLANES, CHUNK), lambda b, s: (b * steps + s, 0, 0)),
            pl.BlockSpec((1, W), lambda b, s: (0, 0)),
        ],
        out_specs=pl.BlockSpec((rows, W), lambda b, s: (b * steps + s, 0)),
        out_shape=jax.ShapeDtypeStruct((T, W), BF16),
        scratch_shapes=[
            pltpu.VMEM((ML_HEADS, ML_HEAD_DIM, 2 * ML_HEAD_DIM), F32),
            pltpu.VMEM((ML_HEADS, 1, 1), F32),
            pltpu.VMEM((2, 2, CHUNK, LANES), F32),
            pltpu.VMEM((2, SUBLANES, CHUNK), F32),
        ],
        compiler_params=_cparams("parallel", "arbitrary"),
        name="mlstm",
    )(zq, kt, zg, zgt, gain)


def _rglru_kernel(zr_ref, cw_ref, cb_ref, wg_ref, bg_ref, lam_ref, gain_ref, y_ref,
                  nat_ref, ost_ref, a_ref, u_ref, p_ref, cx_ref, hc_ref, *, rows):
    Wd = y_ref.shape[1]
    NB = SUBLANES
    BL = rows // NB
    PITCH = BL + SUBLANES
    n_slab = Wd // LANES
    taps = CONV_WIDTH - 1

    @pl.when(pl.program_id(1) == 0)
    def _():
        cx_ref[...] = jnp.zeros_like(cx_ref)
        hc_ref[...] = jnp.zeros_like(hc_ref)

    for s in range(NB):
        blk = zr_ref[s * BL:(s + 1) * BL, :].astype(F32)
        for c in range(2 * n_slab):
            nat_ref[c, s * PITCH:s * PITCH + BL, :] = blk[:, c * LANES:(c + 1) * LANES]

    def interleaved(c):
        return jnp.concatenate([nat_ref[c, pl.ds(i, NB, stride=PITCH), :] for i in range(BL)], axis=0)

    first_sub = lax.broadcasted_iota(jnp.int32, (NB, LANES), 0) == 0
    xcs = []
    for c in range(n_slab):
        lanes = slice(c * LANES, (c + 1) * LANES)
        x = interleaved(c)
        head = []
        for k in range(taps, 0, -1):
            tail = x[(BL - k) * NB:(BL - k + 1) * NB, :]
            prev = cx_ref[c * taps + k - 1]
            head.append(jnp.where(first_sub, pltpu.roll(prev, 1, 0), pltpu.roll(tail, 1, 0)))
            cx_ref[c * taps + k - 1] = tail
        ext = jnp.concatenate(head + [x], axis=0)
        xc = cb_ref[:, lanes] + jnp.zeros((rows, LANES), F32)
        for j in range(CONV_WIDTH):
            xc = xc + cw_ref[j:j + 1, lanes] * ext[j * NB:j * NB + rows, :]
        xcs.append(xc)
    xc = jnp.concatenate(xcs, axis=1)

    gates = _dot(xc.astype(BF16), wg_ref[...]) + bg_ref[...]
    r = _sigmoid(gates[:, :Wd])
    ig = _sigmoid(gates[:, Wd:])
    a = jnp.exp(r * ((-RG_C) * _softplus(-lam_ref[...])))
    a_ref[...] = a
    d = 1.0 - a * a
    u_ref[...] = jnp.where(d > 0.0, d * lax.rsqrt(d), 0.0) * (ig * xc)

    for c in range(n_slab):
        lanes = slice(c * LANES, (c + 1) * LANES)
        h = jnp.zeros((NB, LANES), F32)
        p = jnp.ones((NB, LANES), F32)
        for i in range(BL):
            ai = a_ref[i * NB:(i + 1) * NB, lanes]
            h = ai * h + u_ref[i * NB:(i + 1) * NB, lanes]
            p = ai * p
            u_ref[i * NB:(i + 1) * NB, lanes] = h
            p_ref[i * NB:(i + 1) * NB, lanes] = p
    h_end = u_ref[(BL - 1) * NB:BL * NB, :]
    p_end = p_ref[(BL - 1) * NB:BL * NB, :]
    h_in = [hc_ref[0:1, :]]
    for s in range(NB):
        h_in.append(h_end[s:s + 1, :] + p_end[s:s + 1, :] * h_in[s])
    hc_ref[...] = jnp.broadcast_to(h_in[NB], hc_ref.shape)
    h_enter = jnp.concatenate(h_in[:NB], axis=0)
    h_all = u_ref[...] + p_ref[...] * jnp.concatenate([h_enter] * BL, axis=0)

    gate = jnp.concatenate([interleaved(n_slab + c) for c in range(n_slab)], axis=1)
    yv = h_all * _gelu_tanh(gate)
    yn = _rms_rows(yv) * gain_ref[...]
    for c in range(n_slab):
        for i in range(BL):
            ost_ref[c, pl.ds(i, NB, stride=PITCH), :] = yn[i * NB:(i + 1) * NB, c * LANES:(c + 1) * LANES]
    for s in range(NB):
        y_ref[s * BL:(s + 1) * BL, :] = jnp.concatenate(
            [ost_ref[c, s * PITCH:s * PITCH + BL, :] for c in range(n_slab)], axis=1).astype(BF16)


def _rglru(zr, conv_w, conv_b, w_gates, b_gates, lam, gain, B, S, rows=512):
    T = B * S
    Wd = zr.shape[1] // 2
    rows = min(rows, S)
    steps = S // rows
    pitch_rows = SUBLANES * (rows // SUBLANES + SUBLANES)
    n_slab = Wd // LANES
    kern = functools.partial(_rglru_kernel, rows=rows)
    const = lambda b, s: (0, 0)
    return pl.pallas_call(
        kern,
        grid=(B, steps),
        in_specs=[
            pl.BlockSpec((rows, 2 * Wd), lambda b, s: (b * steps + s, 0)),
            pl.BlockSpec((CONV_WIDTH, Wd), const),
            pl.BlockSpec((1, Wd), const),
            pl.BlockSpec((Wd, 2 * Wd), const),
            pl.BlockSpec((1, 2 * Wd), const),
            pl.BlockSpec((1, Wd), const),
            pl.BlockSpec((1, Wd), const),
        ],
        out_specs=pl.BlockSpec((rows, Wd), lambda b, s: (b * steps + s, 0)),
        out_shape=jax.ShapeDtypeStruct((T, Wd), BF16),
        scratch_shapes=[
            pltpu.VMEM((2 * n_slab, pitch_rows, LANES), F32),
            pltpu.VMEM((n_slab, pitch_rows, LANES), F32),
            pltpu.VMEM((rows, Wd), F32),
            pltpu.VMEM((rows, Wd), F32),
            pltpu.VMEM((rows, Wd), F32),
            pltpu.VMEM((n_slab * (CONV_WIDTH - 1), SUBLANES, LANES), F32),
            pltpu.VMEM((SUBLANES, Wd), F32),
        ],
        compiler_params=_cparams("parallel", "arbitrary"),
        name="rglru",
    )(zr, conv_w, conv_b, w_gates, b_gates, lam, gain)


def _outproj_kernel(yml_ref, yrg_ref, h_ref, w_ref, g_ref, wr_ref, hn_ref, un_ref, lg_ref):
    half = yml_ref.shape[1]
    hn = (h_ref[...] + _dot(yml_ref[...], w_ref[:half, :]) + _dot(yrg_ref[...], w_ref[half:, :]))
    un = _rms_rows(hn) * g_ref[...]
    wr = wr_ref[...]
    w_hi = wr.astype(BF16)
    w_lo = (wr - w_hi.astype(F32)).astype(BF16)
    u_hi = un.astype(BF16)
    u_lo = (un - u_hi.astype(F32)).astype(BF16)
    lg_ref[...] = _dot(u_hi, w_hi) + (_dot(u_hi, w_lo) + _dot(u_lo, w_hi))
    hn_ref[...] = hn
    un_ref[...] = un


def _outproj(yml, yrg, h, w_out, g_ffn, w_router, tm=512):
    T, D = h.shape
    half = yml.shape[1]
    tm = min(tm, T)
    row = lambda i: (i, 0)
    const = lambda i: (0, 0)
    return pl.pallas_call(
        _outproj_kernel,
        grid=(T // tm,),
        in_specs=[
            pl.BlockSpec((tm, half), row),
            pl.BlockSpec((tm, half), row),
            pl.BlockSpec((tm, D), row),
            pl.BlockSpec((2 * half, D), const),
            pl.BlockSpec((1, D), const),
            pl.BlockSpec((D, LANES), const),
        ],
        out_specs=[pl.BlockSpec((tm, D), row), pl.BlockSpec((tm, D), row), pl.BlockSpec((tm, LANES), row)],
        out_shape=[jax.ShapeDtypeStruct((T, D), F32), jax.ShapeDtypeStruct((T, D), F32),
                   jax.ShapeDtypeStruct((T, LANES), F32)],
        compiler_params=_cparams("parallel"),
        name="outproj_router",
    )(yml, yrg, h, w_out, g_ffn, w_router)


W_SPLIT = 1
SUB_ROWS = 1024
SUB_STEP = 256


def _cast_slabs(parts, dst_ref):
    rows = dst_ref.shape[0] // len(parts)
    for q, part in enumerate(parts):
        dst_ref[q * rows:(q + 1) * rows, :] = part[0].astype(BF16)


def _swiglu_steps(j, nj, x_ref, init_fn, weights, scratch, o_ref, valid_fn):
    wg_parts, wu_parts, wd_parts = weights
    act_ref, wgb_ref, wub_ref, wdb_ref = scratch
    tm, D = o_ref.shape
    sub = min(SUB_ROWS, tm)

    for s in range(tm // sub):
        valid = valid_fn(s)
        sizes = [sub] if valid is None else list(range(SUB_STEP, sub, SUB_STEP)) + [sub]
        for z in sizes:
            rows = pl.ds(s * sub, z)

            def up(s=s, rows=rows):
                if s == 0:
                    _cast_slabs(wg_parts, wgb_ref)
                    _cast_slabs(wu_parts, wub_ref)
                x = x_ref[rows, :]
                g = _dot(x, wgb_ref[...])
                u = _dot(x, wub_ref[...])
                act_ref[rows, :] = (g * _sigmoid(g) * u).astype(BF16)

            def down(s=s, rows=rows):
                if s == 0:
                    _cast_slabs(wd_parts, wdb_ref)
                o_ref[rows, :] += _dot(act_ref[rows, :], wdb_ref[...])

            if valid is None:
                cond = lambda c: c
            else:
                on = valid > z - SUB_STEP
                if z < sub:
                    on = jnp.logical_and(on, valid <= z)
                cond = lambda c, on=on: jnp.logical_and(on, c)

            @pl.when(cond(j == 0))
            def _():
                up()
                if init_fn is not None:
                    o_ref[rows, :] = init_fn(rows)

            @pl.when(cond(jnp.logical_and(j > 0, j < nj)))
            def _():
                down()
                up()

            @pl.when(cond(j == nj))
            def _():
                down()
                if z < sub:
                    o_ref[pl.ds(s * sub + z, sub - z), :] = jnp.zeros((sub - z, D), o_ref.dtype)

        if valid is not None:
            @pl.when(jnp.logical_and(valid <= 0, j == nj))
            def _():
                o_ref[pl.ds(s * sub, sub), :] = jnp.zeros((sub, D), o_ref.dtype)


def _split_weights(refs):
    q = W_SPLIT
    return refs[:q], refs[q:2 * q], refs[2 * q:3 * q]


def _weight_specs(D, tf, up_map, down_map):
    q = W_SPLIT
    ups = [pl.BlockSpec((1, D // q, tf), functools.partial(up_map, s)) for s in range(q)]
    downs = [pl.BlockSpec((1, tf // q, D), functools.partial(down_map, s)) for s in range(q)]
    return ups + ups + downs


def _swiglu_scratch(tm, D, tf):
    return [pltpu.VMEM((tm, tf), BF16), pltpu.VMEM((D, tf), BF16), pltpu.VMEM((D, tf), BF16),
            pltpu.VMEM((tf, D), BF16)]


def _ffn_dense_kernel(yml_ref, yrg_ref, h_ref, wo_ref, g_ref, *refs):
    weights = _split_weights(refs)
    o_ref, un_ref = refs[3 * W_SPLIT], refs[3 * W_SPLIT + 1]
    scratch = refs[3 * W_SPLIT + 2:]
    j = pl.program_id(1)
    nj = pl.num_programs(1) - 1
    tm = o_ref.shape[0]
    sub = min(SUB_ROWS, tm)
    half = yml_ref.shape[1]

    @pl.when(j == 0)
    def _():
        for s in range(tm // sub):
            rows = pl.ds(s * sub, sub)
            hn = (h_ref[rows, :] + _dot(yml_ref[rows, :], wo_ref[:half, :])
                  + _dot(yrg_ref[rows, :], wo_ref[half:, :]))
            o_ref[rows, :] = hn
            un_ref[rows, :] = (_rms_rows(hn) * g_ref[...]).astype(BF16)

    _swiglu_steps(j, nj, un_ref, None, weights, scratch, o_ref, lambda s: None)


def _ffn_dense(yml, yrg, h, w_out, g_ffn, wg, wu, wd, layer, tm=1024, tf=256):
    T, D = h.shape
    half = yml.shape[1]
    F = wg.shape[2]
    tm = min(tm, T)
    tf = min(tf, F)
    nj = F // tf
    q = W_SPLIT
    up = lambda s, i, j: (layer, s, jnp.minimum(j, nj - 1))
    down = lambda s, i, j: (layer, jnp.maximum(j - 1, 0) * q + s, 0)
    row = lambda i, j: (i, 0)
    const = lambda i, j: (0, 0)
    return pl.pallas_call(
        _ffn_dense_kernel,
        grid=(T // tm, nj + 1),
        in_specs=[pl.BlockSpec((tm, half), row), pl.BlockSpec((tm, half), row), pl.BlockSpec((tm, D), row),
                  pl.BlockSpec((2 * half, D), const), pl.BlockSpec((1, D), const)]
        + _weight_specs(D, tf, up, down),
        out_specs=pl.BlockSpec((tm, D), row),
        out_shape=jax.ShapeDtypeStruct((T, D), F32),
        scratch_shapes=[pltpu.VMEM((tm, D), BF16)] + _swiglu_scratch(tm, D, tf),
        compiler_params=_cparams("parallel", "arbitrary"),
        name="ffn_dense",
    )(yml, yrg, h, w_out, g_ffn, *([wg] * q + [wu] * q + [wd] * q))


def _ffn_grouped_kernel(te_ref, tr_ref, nt_ref, x_ref, *refs):
    weights = _split_weights(refs)
    o_ref, xb_ref = refs[3 * W_SPLIT], refs[3 * W_SPLIT + 1]
    scratch = refs[3 * W_SPLIT + 2:]
    i = pl.program_id(0)
    j = pl.program_id(1)
    nj = pl.num_programs(1) - 1
    sub = min(SUB_ROWS, o_ref.shape[0])
    used = i < nt_ref[0]

    @pl.when(jnp.logical_and(used, j == 0))
    def _():
        xb_ref[...] = x_ref[...].astype(BF16)

    _swiglu_steps(j, nj, xb_ref, lambda rows: jnp.zeros((rows.size, o_ref.shape[1]), F32), weights, scratch,
                  o_ref, lambda s: jnp.where(used, tr_ref[i] - s * sub, 0))


def _ffn_grouped(tile_expert, tile_rows, n_tiles, xs, wg, wu, wd, first_expert, tm, tf=512):
    P, D = xs.shape
    F = wg.shape[2]
    tf = min(tf, F)
    nj = F // tf
    q = W_SPLIT

    def up(s, i, j, te, tr, nt):
        return (first_expert + te[i], s, jnp.where(i < nt[0], jnp.minimum(j, nj - 1), nj - 1))

    def down(s, i, j, te, tr, nt):
        return (first_expert + te[i], jnp.where(i < nt[0], jnp.maximum(j - 1, 0), nj - 1) * q + s, 0)

    grid_spec = pltpu.PrefetchScalarGridSpec(
        num_scalar_prefetch=3,
        grid=(P // tm, nj + 1),
        in_specs=[pl.BlockSpec((tm, D), lambda i, j, te, tr, nt: (jnp.minimum(i, nt[0] - 1), 0))]
        + _weight_specs(D, tf, up, down),
        out_specs=pl.BlockSpec((tm, D), lambda i, j, te, tr, nt: (i, 0)),
        scratch_shapes=[pltpu.VMEM((tm, D), BF16)] + _swiglu_scratch(tm, D, tf),
    )
    return pl.pallas_call(
        _ffn_grouped_kernel,
        grid_spec=grid_spec,
        out_shape=jax.ShapeDtypeStruct((P, D), F32),
        compiler_params=_cparams("arbitrary", "arbitrary"),
        name="ffn_grouped",
    )(tile_expert, tile_rows, n_tiles, xs, *([wg] * q + [wu] * q + [wd] * q))


def _router_kernel(lg_ref, rt_ref, rtt_ref, cnt_ref, carry_ref):
    tm = lg_ref.shape[0]

    @pl.when(pl.program_id(0) == 0)
    def _():
        carry_ref[...] = jnp.zeros_like(carry_ref)

    lane = lax.broadcasted_iota(jnp.int32, (tm, LANES), 1).astype(F32)
    lg = jnp.where(lane < N_EXPERTS, lg_ref[...], -jnp.inf)
    v1 = jnp.max(lg, axis=1, keepdims=True)
    e1 = jnp.min(jnp.where(lg == v1, lane, float(LANES)), axis=1, keepdims=True)
    lg2 = jnp.where(lane == e1, -jnp.inf, lg)
    v2 = jnp.max(lg2, axis=1, keepdims=True)
    e2 = jnp.min(jnp.where(lg2 == v2, lane, float(LANES)), axis=1, keepdims=True)
    ex = jnp.exp(v2 - v1)
    w1 = 1.0 / (1.0 + ex)
    w2 = ex / (1.0 + ex)

    oh1 = (lane == e1).astype(F32)
    oh2 = (lane == e2).astype(F32)
    r_i = lax.broadcasted_iota(jnp.int32, (tm, tm), 0)
    c_i = lax.broadcasted_iota(jnp.int32, (tm, tm), 1)
    strict = (c_i < r_i).astype(BF16)
    before = _dot(strict, (oh1 + oh2).astype(BF16)) + carry_ref[0:1, :]
    rank1 = jnp.sum(before * oh1, axis=1, keepdims=True)
    rank2 = jnp.sum(before * oh2, axis=1, keepdims=True)
    total = carry_ref[0:1, :] + jnp.sum(oh1 + oh2, axis=0, keepdims=True)
    carry_ref[...] = jnp.broadcast_to(total, carry_ref.shape)
    cnt_ref[...] = jnp.broadcast_to(total, cnt_ref.shape)

    out = jnp.where(lane == 0, e1, 0.0)
    out = jnp.where(lane == 1, e2, out)
    out = jnp.where(lane == 2, w1, out)
    out = jnp.where(lane == 3, w2, out)
    out = jnp.where(lane == 4, rank1, out)
    out = jnp.where(lane == 5, rank2, out)
    rt_ref[...] = out
    rtt_ref[...] = jnp.transpose(out)[:SUBLANES, :]


def _router(logits, tm=512):
    T = logits.shape[0]
    tm = min(tm, T)
    return pl.pallas_call(
        _router_kernel,
        grid=(T // tm,),
        in_specs=[pl.BlockSpec((tm, LANES), lambda i: (i, 0))],
        out_specs=[
            pl.BlockSpec((tm, LANES), lambda i: (i, 0)),
            pl.BlockSpec((SUBLANES, tm), lambda i: (0, i)),
            pl.BlockSpec((SUBLANES, LANES), lambda i: (0, 0)),
        ],
        out_shape=[
            jax.ShapeDtypeStruct((T, LANES), F32),
            jax.ShapeDtypeStruct((SUBLANES, T), F32),
            jax.ShapeDtypeStruct((SUBLANES, LANES), F32),
        ],
        scratch_shapes=[pltpu.VMEM((SUBLANES, LANES), F32)],
        compiler_params=_cparams("arbitrary"),
        name="router",
    )(logits)


def _dispatch_kernel(p1_ref, p2_ref, pad0_ref, padn_ref, nt_ref, u_ref, xs_hbm, zbuf_ref, sem, zsem,
                     *, pad_bits, tail_per_tile):
    tm = u_ref.shape[0]

    @pl.when(pl.program_id(0) == 0)
    def _():
        zbuf_ref[...] = jnp.zeros_like(zbuf_ref)

        def pad_copies(e, b):
            n = padn_ref[e]
            off = pad0_ref[e] + (n & ((1 << b) - 1))
            if (1 << b) >= SUBLANES:
                off = pl.multiple_of(off, SUBLANES)
                return [pltpu.make_async_copy(zbuf_ref.at[pl.ds(0, 1 << b)],
                                              xs_hbm.at[pl.ds(off, 1 << b)], zsem)]
            return [pltpu.make_async_copy(zbuf_ref.at[pl.ds(0, 1)], xs_hbm.at[pl.ds(off + r, 1)], zsem)
                    for r in range(1 << b)]

        for wait in (False, True):
            for e in range(N_EXPERTS):
                for b in range(pad_bits):
                    @pl.when(((padn_ref[e] >> b) & 1) == 1)
                    def _():
                        for cp in pad_copies(e, b):
                            if wait:
                                cp.wait()
                            else:
                                cp.start()

        zrows = zbuf_ref.shape[0]

        def tail_copy(t):
            off = pl.multiple_of(t * zrows, SUBLANES)
            return pltpu.make_async_copy(zbuf_ref, xs_hbm.at[pl.ds(off, zrows)], zsem)

        def tail_start(t, carry):
            tail_copy(t).start()
            return carry

        def tail_wait(t, carry):
            tail_copy(t).wait()
            return carry

        first, last = nt_ref[0] * tail_per_tile, xs_hbm.shape[0] // zrows
        lax.fori_loop(first, last, tail_start, 0)
        lax.fori_loop(first, last, tail_wait, 0)

    def row_copy(r, pos):
        return pltpu.make_async_copy(u_ref.at[pl.ds(r, 1)], xs_hbm.at[pl.ds(pos, 1)], sem)

    def issue(b, carry):
        for rr in range(ISSUE_UNROLL):
            r = b * ISSUE_UNROLL + rr
            row_copy(r, p1_ref[r]).start()
            row_copy(r, p2_ref[r]).start()
        return carry

    lax.fori_loop(0, tm // ISSUE_UNROLL, issue, 0)
    for _ in range(2):
        pltpu.make_async_copy(u_ref, xs_hbm.at[pl.ds(0, tm)], sem).wait()


def _dispatch(pos1, pos2, pad_start, pad_len, n_tiles, un, P, group_tile, tm=512):
    T, D = un.shape
    tm = min(tm, T)
    pad_bits = group_tile.bit_length() - 1
    assert group_tile == 1 << pad_bits and pad_bits >= 1
    zrows = group_tile // 2
    kern = functools.partial(_dispatch_kernel, pad_bits=pad_bits, tail_per_tile=group_tile // zrows)
    smem = pl.BlockSpec(memory_space=pltpu.SMEM)
    return pl.pallas_call(
        kern,
        grid=(T // tm,),
        in_specs=[
            pl.BlockSpec((tm,), lambda i: (i,), memory_space=pltpu.SMEM),
            pl.BlockSpec((tm,), lambda i: (i,), memory_space=pltpu.SMEM),
            smem, smem, smem,
            pl.BlockSpec((tm, D), lambda i: (i, 0)),
        ],
        out_specs=pl.BlockSpec(memory_space=pl.ANY),
        out_shape=jax.ShapeDtypeStruct((P, D), un.dtype),
        scratch_shapes=[pltpu.VMEM((zrows, D), un.dtype),
                        pltpu.SemaphoreType.DMA(()), pltpu.SemaphoreType.DMA(())],
        compiler_params=_cparams("arbitrary"),
        name="dispatch",
    )(pos1, pos2, pad_start, pad_len, n_tiles, un)


def _combine_kernel(p1_ref, p2_ref, q1_ref, q2_ref, h_ref, rt_ref, g_ref, y_hbm, o_ref, buf_ref, sem,
                    *, final_norm):
    tm = h_ref.shape[0]
    i = pl.program_id(0)
    n = pl.num_programs(0)
    slot = i % 2

    def issue_tile(i1_ref, i2_ref, s):
        def row_copy(k, r, pos):
            return pltpu.make_async_copy(y_hbm.at[pl.ds(pos, 1)], buf_ref.at[s, k, pl.ds(r, 1)], sem.at[s])

        def issue(b, carry):
            for rr in range(ISSUE_UNROLL):
                r = b * ISSUE_UNROLL + rr
                row_copy(0, r, i1_ref[r]).start()
                row_copy(1, r, i2_ref[r]).start()
            return carry

        lax.fori_loop(0, tm // ISSUE_UNROLL, issue, 0)

    @pl.when(i == 0)
    def _():
        issue_tile(p1_ref, p2_ref, 0)

    @pl.when(i + 1 < n)
    def _():
        issue_tile(q1_ref, q2_ref, 1 - slot)

    for k in range(2):
        pltpu.make_async_copy(y_hbm.at[pl.ds(0, tm)], buf_ref.at[slot, k], sem.at[slot]).wait()

    rt = rt_ref[...]
    out = h_ref[...] + (rt[:, 2:3] * buf_ref[slot, 0] + rt[:, 3:4] * buf_ref[slot, 1])
    if final_norm:
        out = _rms_rows(out) * g_ref[...]
    o_ref[...] = out


def _combine(pos1, pos2, h, routing, g_final, y, final_norm, tm=256):
    T, D = h.shape
    tm = min(tm, T)
    n = T // tm
    kern = functools.partial(_combine_kernel, final_norm=final_norm)
    cur = pl.BlockSpec((tm,), lambda i: (i,), memory_space=pltpu.SMEM)
    nxt = pl.BlockSpec((tm,), lambda i: (jnp.minimum(i + 1, n - 1),), memory_space=pltpu.SMEM)
    return pl.pallas_call(
        kern,
        grid=(n,),
        in_specs=[
            cur, cur, nxt, nxt,
            pl.BlockSpec((tm, D), lambda i: (i, 0)),
            pl.BlockSpec((tm, LANES), lambda i: (i, 0)),
            pl.BlockSpec((1, D), lambda i: (0, 0)),
            pl.BlockSpec(memory_space=pl.ANY),
        ],
        out_specs=pl.BlockSpec((tm, D), lambda i: (i, 0)),
        out_shape=jax.ShapeDtypeStruct((T, D), F32),
        scratch_shapes=[pltpu.VMEM((2, 2, tm, D), F32), pltpu.SemaphoreType.DMA((2,))],
        compiler_params=_cparams("arbitrary"),
        name="combine",
    )(pos1, pos2, pos1, pos2, h, routing, g_final, y)


def _final_norm_kernel(h_ref, g_ref, o_ref):
    o_ref[...] = _rms_rows(h_ref[...]) * g_ref[...]


def _final_norm(h, g, tm=1024):
    T, D = h.shape
    tm = min(tm, T)
    return pl.pallas_call(
        _final_norm_kernel,
        grid=(T // tm,),
        in_specs=[pl.BlockSpec((tm, D), lambda i: (i, 0)), pl.BlockSpec((1, D), lambda i: (0, 0))],
        out_specs=pl.BlockSpec((tm, D), lambda i: (i, 0)),
        out_shape=jax.ShapeDtypeStruct((T, D), F32),
        compiler_params=_cparams("parallel"),
        name="final_norm",
    )(h, g)


def _block_diag(w):
    G, n, _ = w.shape
    eye = jnp.eye(G, dtype=w.dtype)
    return (eye[:, None, :, None] * w[:, :, None, :]).reshape(G * n, G * n)


def _moe_layer(h, un, logits, w_gate, w_up, w_down, first_expert, g_final, final_norm, tm):
    T, D = h.shape
    E = N_EXPERTS
    routing, routing_t, counts = _router(logits)
    e1 = routing_t[0].astype(jnp.int32)
    e2 = routing_t[1].astype(jnp.int32)
    rank1 = routing_t[4].astype(jnp.int32)
    rank2 = routing_t[5].astype(jnp.int32)
    cnt = counts[0, :E].astype(jnp.int32)
    tiles_per = (cnt + tm - 1) // tm
    tile_end = jnp.cumsum(tiles_per)
    offs = (tile_end - tiles_per) * tm
    pos1 = offs[e1] + rank1
    pos2 = offs[e2] + rank2
    n_slots = (2 * T) // tm + E
    P = n_slots * tm
    tile_expert = jnp.minimum(
        jnp.sum(jnp.arange(n_slots, dtype=jnp.int32)[:, None] >= tile_end[None, :], axis=1), E - 1
    ).astype(jnp.int32)
    n_tiles = tile_end[E - 1:E].astype(jnp.int32)
    tile_in_group = jnp.arange(n_slots, dtype=jnp.int32) - (tile_end - tiles_per)[tile_expert]
    tile_rows = jnp.clip(cnt[tile_expert] - tile_in_group * tm, 0, tm).astype(jnp.int32)

    xs = _dispatch(pos1, pos2, offs + cnt, tiles_per * tm - cnt, n_tiles, un, P, tm)
    y = _ffn_grouped(tile_expert, tile_rows, n_tiles, xs, w_gate, w_up, w_down, first_expert, tm)
    return _combine(pos1, pos2, h, routing, g_final, y, final_norm)


def _forward(x, norm_mix_g, w_in, ml_b_if, ml_norm_g, rg_conv_w, rg_conv_b, rg_w_a, rg_b_a,
             rg_w_x, rg_b_x, rg_lam, rg_norm_g, w_out, norm_ffn_g, ffn_w_gate, ffn_w_up,
             ffn_w_down, moe_w_router, moe_w_gate, moe_w_up, moe_w_down, norm_final_g,
             moe_tile=2048):
    B, S, D = x.shape
    T = B * S
    depth = w_in.shape[0]
    ml_w = ML_HEADS * ML_HEAD_DIM
    n_q = 4 * ml_w
    n_if = 2 * ML_HEADS
    rg_w = rg_lam.shape[1]
    n_r = 2 * rg_w

    h = x.reshape(T, D)
    g_final = norm_final_g.reshape(1, D)
    moe_gate_all = moe_w_gate.reshape((-1,) + moe_w_gate.shape[2:])
    moe_up_all = moe_w_up.reshape((-1,) + moe_w_up.shape[2:])
    moe_down_all = moe_w_down.reshape((-1,) + moe_w_down.shape[2:])
    for l in range(depth):
        wl = w_in[l]
        w_all = jnp.concatenate(
            [wl[:, :ml_w], wl[:, 2 * ml_w:n_q], wl[:, n_q + n_if:], wl[:, n_q:n_q + n_if],
             jnp.zeros((D, LANES - n_if), wl.dtype)], axis=1).astype(BF16)
        wt = jnp.concatenate([wl[:, ml_w:2 * ml_w], wl[:, n_q:n_q + n_if]], axis=1).T.astype(BF16)
        b_if = jnp.concatenate([ml_b_if[l], jnp.zeros((LANES - n_if,), F32)]).reshape(1, LANES)
        zq, kt, zr, zg, zgt = _inproj(h, norm_mix_g[l].reshape(1, D), b_if, ml_b_if[l].reshape(n_if, 1),
                                      w_all, wt, 3 * ml_w, n_r)
        yml = _mlstm(zq, kt, zg, zgt, ml_norm_g[l].reshape(1, ml_w), B, S)

        w_gates = jnp.concatenate([_block_diag(rg_w_a[l]), _block_diag(rg_w_x[l])], axis=1).astype(BF16)
        b_gates = jnp.concatenate([rg_b_a[l], rg_b_x[l]]).reshape(1, n_r)
        yrg = _rglru(zr, rg_conv_w[l], rg_conv_b[l].reshape(1, rg_w), w_gates, b_gates,
                     rg_lam[l].reshape(1, rg_w), rg_norm_g[l].reshape(1, rg_w), B, S)

        j = l // 2
        is_moe = (l % 2 == 1)
        wo = w_out[l].astype(BF16)
        g_ffn = norm_ffn_g[l].reshape(1, D)
        last = (l == depth - 1)
        if is_moe:
            w_router = jnp.concatenate(
                [moe_w_router[j], jnp.zeros((D, LANES - N_EXPERTS), F32)], axis=1)
            h, un, logits = _outproj(yml, yrg, h, wo, g_ffn, w_router)
            h = _moe_layer(h, un, logits, moe_gate_all, moe_up_all, moe_down_all, j * N_EXPERTS,
                           g_final, last, moe_tile)
        else:
            h = _ffn_dense(yml, yrg, h, wo, g_ffn, ffn_w_gate, ffn_w_up, ffn_w_down, j)
            if last:
                h = _final_norm(h, g_final)
    return h.reshape(B, S, D)


def kernel(x, norm_mix_g, w_in, ml_b_if, ml_norm_g, rg_conv_w, rg_conv_b, rg_w_a, rg_b_a, rg_w_x,
           rg_b_x, rg_lam, rg_norm_g, w_out, norm_ffn_g, ffn_w_gate, ffn_w_up, ffn_w_down,
           moe_w_router, moe_w_gate, moe_w_up, moe_w_down, norm_final_g):
    return _forward(x, norm_mix_g, w_in, ml_b_if, ml_norm_g, rg_conv_w, rg_conv_b, rg_w_a, rg_b_a,
                    rg_w_x, rg_b_x, rg_lam, rg_norm_g, w_out, norm_ffn_g, ffn_w_gate, ffn_w_up,
                    ffn_w_down, moe_w_router, moe_w_gate, moe_w_up, moe_w_down, norm_final_g)
```

```python
import functools

import jax
import jax.numpy as jnp
from jax import lax
from jax.experimental import pallas as pl
from jax.experimental.pallas import tpu as pltpu

EPS = 1e-6
ML_HEADS = 4
ML_HEAD_DIM = 128
CHUNK = 128
RG_BLOCKS = 8
RG_C = 8.0
CONV_WIDTH = 4
N_EXPERTS = 8
LANES = 128
SUBLANES = 8
VMEM_LIMIT = 60 * 1024 * 1024
ISSUE_UNROLL = 8

BF16 = jnp.bfloat16
F32 = jnp.float32


def _cparams(*sem):
    return pltpu.CompilerParams(dimension_semantics=sem, vmem_limit_bytes=VMEM_LIMIT)


def _dot(a, b):
    return jnp.dot(a, b, preferred_element_type=F32)


def _dot_nt(a, b):
    return lax.dot_general(a, b, (((1,), (1,)), ((), ())), preferred_element_type=F32)


def _dot_tn(a, b):
    return lax.dot_general(a, b, (((0,), (0,)), ((), ())), preferred_element_type=F32)


def _dot_f32(a, b):
    return jnp.dot(a, b, preferred_element_type=F32, precision=lax.Precision.HIGHEST)


def _sigmoid(x):
    return 1.0 / (1.0 + jnp.exp(-x))


def _log_sigmoid(x):
    return jnp.minimum(x, 0.0) - jnp.log(1.0 + jnp.exp(-jnp.abs(x)))


def _softplus(x):
    return jnp.maximum(x, 0.0) + jnp.log(1.0 + jnp.exp(-jnp.abs(x)))


def _gelu_tanh(x):
    return 0.5 * x * (1.0 + jnp.tanh(0.7978845608028654 * (x + 0.044715 * (x * x * x))))


def _rms_rows(x):
    return x * lax.rsqrt(jnp.mean(x * x, axis=-1, keepdims=True) + EPS)


def _inproj_kernel(h_ref, g_ref, b_ref, bt_ref, w_ref, wt_ref, zq_ref, kt_ref, zr_ref, zg_ref, zgt_ref,
                   *, n_q, n_r, n_col):
    xn = (_rms_rows(h_ref[...]) * g_ref[...]).astype(BF16)
    for c0 in range(0, n_q, n_col):
        zq_ref[:, c0:c0 + n_col] = _dot(xn, w_ref[:, c0:c0 + n_col]).astype(BF16)
    for c0 in range(0, n_r, n_col):
        zr_ref[:, c0:c0 + n_col] = _dot(xn, w_ref[:, n_q + c0:n_q + c0 + n_col]).astype(BF16)
    zg_ref[...] = _dot(xn, w_ref[:, n_q + n_r:]) + b_ref[...]
    n_k = kt_ref.shape[1]
    t = _dot_nt(wt_ref[...], xn)
    for c in range(kt_ref.shape[0]):
        kt_ref[c] = t[:n_k, c * CHUNK:(c + 1) * CHUNK].astype(BF16)
        zgt_ref[c] = t[n_k:, c * CHUNK:(c + 1) * CHUNK] + bt_ref[...]


def _inproj(h, g, b_if, b_if_t, w_all, wt, n_q, n_r, tm=1024):
    T, D = h.shape
    tm = min(tm, T)
    n_all = w_all.shape[1]
    n_g = n_all - n_q - n_r
    n_k = wt.shape[0] - SUBLANES
    kern = functools.partial(_inproj_kernel, n_q=n_q, n_r=n_r, n_col=512)
    const = lambda i: (0, 0)
    return pl.pallas_call(
        kern,
        grid=(T // tm,),
        in_specs=[
            pl.BlockSpec((tm, D), lambda i: (i, 0)),
            pl.BlockSpec((1, D), const),
            pl.BlockSpec((1, n_g), const),
            pl.BlockSpec((SUBLANES, 1), const),
            pl.BlockSpec((D, n_all), const),
            pl.BlockSpec((n_k + SUBLANES, D), const),
        ],
        out_specs=[
            pl.BlockSpec((tm, n_q), lambda i: (i, 0)),
            pl.BlockSpec((tm // CHUNK, n_k, CHUNK), lambda i: (i, 0, 0)),
            pl.BlockSpec((tm, n_r), lambda i: (i, 0)),
            pl.BlockSpec((tm, n_g), lambda i: (i, 0)),
            pl.BlockSpec((tm // CHUNK, SUBLANES, CHUNK), lambda i: (i, 0, 0)),
        ],
        out_shape=[
            jax.ShapeDtypeStruct((T, n_q), BF16),
            jax.ShapeDtypeStruct((T // CHUNK, n_k, CHUNK), BF16),
            jax.ShapeDtypeStruct((T, n_r), BF16),
            jax.ShapeDtypeStruct((T, n_g), F32),
            jax.ShapeDtypeStruct((T // CHUNK, SUBLANES, CHUNK), F32),
        ],
        compiler_params=_cparams("parallel"),
        name="inproj",
    )(h, g, b_if, b_if_t, w_all, wt)


def _mlstm_kernel(zq_ref, kt_ref, zg_ref, zgt_ref, gain_ref, y_ref, c_ref, m_ref, pcol_ref, prow_ref,
                  *, n_chunks):
    H, Dh, L = ML_HEADS, ML_HEAD_DIM, CHUNK
    W = H * Dh
    scale = Dh ** -0.5

    @pl.when(pl.program_id(1) == 0)
    def _():
        c_ref[...] = jnp.zeros_like(c_ref)
        m_ref[...] = jnp.zeros_like(m_ref)

    row = lax.broadcasted_iota(jnp.int32, (L, L), 0)
    col = lax.broadcasted_iota(jnp.int32, (L, L), 1)
    causal = col <= row
    tri_l = causal.astype(F32)
    tri_u = (row <= col).astype(F32)
    ones_blk = jnp.ones((L, Dh), BF16)

    def gate_prefix(c, slot):
        r0 = pl.multiple_of(c * L, L)
        gcol = zg_ref[pl.ds(r0, L), :]
        grow = zgt_ref[c]
        bcol_all = _dot_f32(tri_l, _log_sigmoid(gcol))
        brow_all = _dot_f32(_log_sigmoid(grow), tri_u)
        cm = gcol - pltpu.roll(bcol_all, LANES - H, 1)
        d = 1
        while d < L:
            cm = jnp.maximum(cm, jnp.where(row >= d, pltpu.roll(cm, d, 0), -jnp.inf))
            d *= 2
        pcol_ref[slot, 0] = bcol_all
        pcol_ref[slot, 1] = cm
        prow_ref[slot] = brow_all

    gate_prefix(0, 0)

    def chunk_body(c, carry):
        r0 = pl.multiple_of(c * L, L)
        slot = c % 2
        grow = zgt_ref[c]
        bcol_all = pcol_ref[slot, 0]
        cm = pcol_ref[slot, 1]
        brow_all = prow_ref[slot]

        for hd in range(H):
            q = zq_ref[pl.ds(r0, L), hd * Dh:(hd + 1) * Dh]
            v = zq_ref[pl.ds(r0, L), W + hd * Dh:W + (hd + 1) * Dh]
            o = zq_ref[pl.ds(r0, L), 2 * W + hd * Dh:2 * W + (hd + 1) * Dh]
            kt = kt_ref[c, hd * Dh:(hd + 1) * Dh, :]
            v_aug = jnp.concatenate([v, ones_blk], axis=1)

            li_row = grow[hd:hd + 1, :]
            b_row = brow_all[H + hd:H + hd + 1, :]
            g_tot = b_row[:, L - 1:L]
            r_row = li_row - b_row

            c_prev = c_ref[hd]
            m_prev = m_ref[hd]

            mx = jnp.maximum(jnp.broadcast_to(cm[:, hd:hd + 1], (L, L)), m_prev)
            bb = jnp.broadcast_to(bcol_all[:, H + hd:H + hd + 1], (L, L))
            p = jnp.exp(jnp.where(causal, r_row - mx, -jnp.inf))
            s_w = p * (_dot(q, kt) * scale)
            inter = jnp.exp(m_prev - mx)
            intra = _dot(s_w.astype(BF16), v_aug)
            cross = _dot(q, c_prev.astype(BF16))
            num = intra[:, :Dh] + inter * cross[:, :Dh]
            den = intra[:, Dh:] + inter * cross[:, Dh:]
            hh = num / jnp.maximum(jnp.abs(den), jnp.exp(-(bb + mx)))
            ht = _rms_rows(hh) * gain_ref[:, hd * Dh:(hd + 1) * Dh]
            y_ref[pl.ds(r0, L), hd * Dh:(hd + 1) * Dh] = (_sigmoid(o.astype(F32)) * ht).astype(BF16)

            a = g_tot + r_row
            m_loc = jnp.max(a, axis=1, keepdims=True)
            w = jnp.exp(a - m_loc) * scale
            c_loc = _dot((kt.astype(F32) * w).astype(BF16), v_aug)
            m_new = jnp.maximum(g_tot + m_prev, m_loc)
            s_old = jnp.exp(g_tot + m_prev - m_new)
            s_loc = jnp.exp(m_loc - m_new)
            c_ref[hd] = s_old * c_prev + s_loc * c_loc
            m_ref[hd] = m_new
        gate_prefix(jnp.minimum(c + 1, n_chunks - 1), 1 - slot)
        return carry

    lax.fori_loop(0, n_chunks, chunk_body, 0)


def _mlstm(zq, kt, zg, zgt, gain, B, S, rows=1024):
    T = B * S
    rows = min(rows, S)
    n_chunks = rows // CHUNK
    steps = S // rows
    W = ML_HEADS * ML_HEAD_DIM
    kern = functools.partial(_mlstm_kernel, n_chunks=n_chunks)
    return pl.pallas_call(
        kern,
        grid=(B, steps),
        in_specs=[
            pl.BlockSpec((rows, 3 * W), lambda b, s: (b * steps + s, 0)),
            pl.BlockSpec((n_chunks, W, CHUNK), lambda b, s: (b * steps + s, 0, 0)),
            pl.BlockSpec((rows, LANES), lambda b, s: (b * steps + s, 0)),
            pl.BlockSpec((n_chunks, SUBLANES, CHUNK), lambda b, s: (b * steps + s, 0, 0)),
            pl.BlockSpec((1, W), lambda b, s: (0, 0)),
        ],
        out_specs=pl.BlockSpec((rows, W), lambda b, s: (b * steps + s, 0)),
        out_shape=jax.ShapeDtypeStruct((T, W), BF16),
        scratch_shapes=[
            pltpu.VMEM((ML_HEADS, ML_HEAD_DIM, 2 * ML_HEAD_DIM), F32),
            pltpu.VMEM((ML_HEADS, 1, 1), F32),
            pltpu.VMEM((2, 2, CHUNK, LANES), F32),
            pltpu.VMEM((2, SUBLANES, CHUNK), F32),
        ],
        compiler_params=_cparams("parallel", "arbitrary"),
        name="mlstm",
    )(zq, kt, zg, zgt, gain)


def _rglru_kernel(zr_ref, cw_ref, cb_ref, wg_ref, bg_ref, lam_ref, gain_ref, y_ref,
                  nat_ref, ost_ref, a_ref, u_ref, p_ref, cx_ref, hc_ref, *, rows):
    Wd = y_ref.shape[1]
    NB = SUBLANES
    BL = rows // NB
    PITCH = BL + SUBLANES
    n_slab = Wd // LANES
    taps = CONV_WIDTH - 1

    @pl.when(pl.program_id(1) == 0)
    def _():
        cx_ref[...] = jnp.zeros_like(cx_ref)
        hc_ref[...] = jnp.zeros_like(hc_ref)

    for s in range(NB):
        blk = zr_ref[s * BL:(s + 1) * BL, :].astype(F32)
        for c in range(2 * n_slab):
            nat_ref[c, s * PITCH:s * PITCH + BL, :] = blk[:, c * LANES:(c + 1) * LANES]

    def interleaved(c):
        return jnp.concatenate([nat_ref[c, pl.ds(i, NB, stride=PITCH), :] for i in range(BL)], axis=0)

    first_sub = lax.broadcasted_iota(jnp.int32, (NB, LANES), 0) == 0
    xcs = []
    for c in range(n_slab):
        lanes = slice(c * LANES, (c + 1) * LANES)
        x = interleaved(c)
        head = []
        for k in range(taps, 0, -1):
            tail = x[(BL - k) * NB:(BL - k + 1) * NB, :]
            prev = cx_ref[c * taps + k - 1]
            head.append(jnp.where(first_sub, pltpu.roll(prev, 1, 0), pltpu.roll(tail, 1, 0)))
            cx_ref[c * taps + k - 1] = tail
        ext = jnp.concatenate(head + [x], axis=0)
        xc = cb_ref[:, lanes] + jnp.zeros((rows, LANES), F32)
        for j in range(CONV_WIDTH):
            xc = xc + cw_ref[j:j + 1, lanes] * ext[j * NB:j * NB + rows, :]
        xcs.append(xc)
    xc = jnp.concatenate(xcs, axis=1)

    gates = _dot(xc.astype(BF16), wg_ref[...]) + bg_ref[...]
    r = _sigmoid(gates[:, :Wd])
    ig = _sigmoid(gates[:, Wd:])
    a = jnp.exp(r * ((-RG_C) * _softplus(-lam_ref[...])))
    a_ref[...] = a
    d = 1.0 - a * a
    u_ref[...] = jnp.where(d > 0.0, d * lax.rsqrt(d), 0.0) * (ig * xc)

    for c in range(n_slab):
        lanes = slice(c * LANES, (c + 1) * LANES)
        h = jnp.zeros((NB, LANES), F32)
        p = jnp.ones((NB, LANES), F32)
        for i in range(BL):
            ai = a_ref[i * NB:(i + 1) * NB, lanes]
            h = ai * h + u_ref[i * NB:(i + 1) * NB, lanes]
            p = ai * p
            u_ref[i * NB:(i + 1) * NB, lanes] = h
            p_ref[i * NB:(i + 1) * NB, lanes] = p
    h_end = u_ref[(BL - 1) * NB:BL * NB, :]
    p_end = p_ref[(BL - 1) * NB:BL * NB, :]
    h_in = [hc_ref[0:1, :]]
    for s in range(NB):
        h_in.append(h_end[s:s + 1, :] + p_end[s:s + 1, :] * h_in[s])
    hc_ref[...] = jnp.broadcast_to(h_in[NB], hc_ref.shape)
    h_enter = jnp.concatenate(h_in[:NB], axis=0)
    h_all = u_ref[...] + p_ref[...] * jnp.concatenate([h_enter] * BL, axis=0)

    gate = jnp.concatenate([interleaved(n_slab + c) for c in range(n_slab)], axis=1)
    yv = h_all * _gelu_tanh(gate)
    yn = _rms_rows(yv) * gain_ref[...]
    for c in range(n_slab):
        for i in range(BL):
            ost_ref[c, pl.ds(i, NB, stride=PITCH), :] = yn[i * NB:(i + 1) * NB, c * LANES:(c + 1) * LANES]
    for s in range(NB):
        y_ref[s * BL:(s + 1) * BL, :] = jnp.concatenate(
            [ost_ref[c, s * PITCH:s * PITCH + BL, :] for c in range(n_slab)], axis=1).astype(BF16)


def _rglru(zr, conv_w, conv_b, w_gates, b_gates, lam, gain, B, S, rows=1024):
    T = B * S
    Wd = zr.shape[1] // 2
    rows = min(rows, S)
    steps = S // rows
    pitch_rows = SUBLANES * (rows // SUBLANES + SUBLANES)
    n_slab = Wd // LANES
    kern = functools.partial(_rglru_kernel, rows=rows)
    const = lambda b, s: (0, 0)
    return pl.pallas_call(
        kern,
        grid=(B, steps),
        in_specs=[
            pl.BlockSpec((rows, 2 * Wd), lambda b, s: (b * steps + s, 0)),
            pl.BlockSpec((CONV_WIDTH, Wd), const),
            pl.BlockSpec((1, Wd), const),
            pl.BlockSpec((Wd, 2 * Wd), const),
            pl.BlockSpec((1, 2 * Wd), const),
            pl.BlockSpec((1, Wd), const),
            pl.BlockSpec((1, Wd), const),
        ],
        out_specs=pl.BlockSpec((rows, Wd), lambda b, s: (b * steps + s, 0)),
        out_shape=jax.ShapeDtypeStruct((T, Wd), BF16),
        scratch_shapes=[
            pltpu.VMEM((2 * n_slab, pitch_rows, LANES), F32),
            pltpu.VMEM((n_slab, pitch_rows, LANES), F32),
            pltpu.VMEM((rows, Wd), F32),
            pltpu.VMEM((rows, Wd), F32),
            pltpu.VMEM((rows, Wd), F32),
            pltpu.VMEM((n_slab * (CONV_WIDTH - 1), SUBLANES, LANES), F32),
            pltpu.VMEM((SUBLANES, Wd), F32),
        ],
        compiler_params=_cparams("parallel", "arbitrary"),
        name="rglru",
    )(zr, conv_w, conv_b, w_gates, b_gates, lam, gain)


def _outproj_kernel(yml_ref, yrg_ref, h_ref, w_ref, g_ref, wr_ref, hn_ref, un_ref, lg_ref):
    half = yml_ref.shape[1]
    hn = (h_ref[...] + _dot(yml_ref[...], w_ref[:half, :]) + _dot(yrg_ref[...], w_ref[half:, :]))
    un = _rms_rows(hn) * g_ref[...]
    wr = wr_ref[...]
    w_hi = wr.astype(BF16)
    w_lo = (wr - w_hi.astype(F32)).astype(BF16)
    u_hi = un.astype(BF16)
    u_lo = (un - u_hi.astype(F32)).astype(BF16)
    lg_ref[...] = _dot(u_hi, w_hi) + (_dot(u_hi, w_lo) + _dot(u_lo, w_hi))
    hn_ref[...] = hn
    un_ref[...] = un


def _outproj(yml, yrg, h, w_out, g_ffn, w_router, tm=1024):
    T, D = h.shape
    half = yml.shape[1]
    tm = min(tm, T)
    row = lambda i: (i, 0)
    const = lambda i: (0, 0)
    return pl.pallas_call(
        _outproj_kernel,
        grid=(T // tm,),
        in_specs=[
            pl.BlockSpec((tm, half), row),
            pl.BlockSpec((tm, half), row),
            pl.BlockSpec((tm, D), row),
            pl.BlockSpec((2 * half, D), const),
            pl.BlockSpec((1, D), const),
            pl.BlockSpec((D, LANES), const),
        ],
        out_specs=[pl.BlockSpec((tm, D), row), pl.BlockSpec((tm, D), row), pl.BlockSpec((tm, LANES), row)],
        out_shape=[jax.ShapeDtypeStruct((T, D), F32), jax.ShapeDtypeStruct((T, D), F32),
                   jax.ShapeDtypeStruct((T, LANES), F32)],
        compiler_params=_cparams("parallel"),
        name="outproj_router",
    )(yml, yrg, h, w_out, g_ffn, w_router)


SUB_ROWS = 1024
SUB_STEP = 256


def _swiglu_tile(x_ref, o_ref, init_fn, valid_fn, w_hbm, e_cur, e_next, first, has_next, wbufs, sems,
                 scratch, tf):
    wg_hbm, wu_hbm, wd_hbm = w_hbm
    gbuf, ubuf, dbuf = wbufs
    act_ref, wgb_ref, wub_ref, wdb_ref = scratch
    tm, D = o_ref.shape
    sub = min(SUB_ROWS, tm)
    nj = wg_hbm.shape[2] // tf

    def up_copies(e, j, slot):
        cols = pl.ds(pl.multiple_of(j * tf, LANES), tf)
        return (pltpu.make_async_copy(wg_hbm.at[e, :, cols], gbuf.at[slot], sems.at[0, slot]),
                pltpu.make_async_copy(wu_hbm.at[e, :, cols], ubuf.at[slot], sems.at[1, slot]))

    def down_copy(e, j, slot):
        rws = pl.ds(pl.multiple_of(j * tf, SUBLANES), tf)
        return pltpu.make_async_copy(wd_hbm.at[e, rws, :], dbuf.at[slot], sems.at[2, slot])

    def start_up(e, j, slot):
        for cp in up_copies(e, j, slot):
            cp.start()

    def wait_up(e, j, slot):
        for cp in up_copies(e, j, slot):
            cp.wait()

    def up(rows, s, slot):
        if s == 0:
            wgb_ref[...] = gbuf[slot].astype(BF16)
            wub_ref[...] = ubuf[slot].astype(BF16)
        x = x_ref[rows, :]
        g = _dot(x, wgb_ref[...])
        u = _dot(x, wub_ref[...])
        act_ref[rows, :] = (g * _sigmoid(g) * u).astype(BF16)

    def down(rows, s, slot):
        if s == 0:
            wdb_ref[...] = dbuf[slot].astype(BF16)
        o_ref[rows, :] += _dot(act_ref[rows, :], wdb_ref[...])

    def sub_tiles(body):
        for s in range(tm // sub):
            valid = valid_fn(s)
            if valid is None:
                body(pl.ds(s * sub, sub), s, sub)
                continue
            for z in list(range(SUB_STEP, sub, SUB_STEP)) + [sub]:
                on = valid > z - SUB_STEP
                if z < sub:
                    on = jnp.logical_and(on, valid <= z)
                pl.when(on)(functools.partial(body, pl.ds(s * sub, z), s, z))

    @pl.when(first)
    def _():
        start_up(e_cur, 0, 0)

    wait_up(e_cur, 0, 0)
    if nj > 1:
        start_up(e_cur, 1, 1)
    down_copy(e_cur, 0, 0).start()

    def first_body(rows, s, z):
        up(rows, s, 0)
        if init_fn is not None:
            o_ref[rows, :] = init_fn(rows)

    sub_tiles(first_body)

    def step(j, carry):
        slot = j % 2
        wait_up(e_cur, j, slot)
        down_copy(e_cur, j - 1, 1 - slot).wait()

        @pl.when(j + 1 < nj)
        def _():
            start_up(e_cur, j + 1, 1 - slot)

        down_copy(e_cur, j, slot).start()

        def body(rows, s, z):
            down(rows, s, 1 - slot)
            up(rows, s, slot)

        sub_tiles(body)
        return carry

    lax.fori_loop(1, nj, step, 0)

    last_slot = (nj - 1) % 2
    down_copy(e_cur, nj - 1, last_slot).wait()

    @pl.when(has_next)
    def _():
        start_up(e_next, 0, 0)

    def last_body(rows, s, z):
        down(rows, s, last_slot)
        if z < sub:
            o_ref[pl.ds(s * sub + z, sub - z), :] = jnp.zeros((sub - z, D), o_ref.dtype)

    sub_tiles(last_body)
    for s in range(tm // sub):
        valid = valid_fn(s)
        if valid is not None:
            @pl.when(valid <= 0)
            def _():
                o_ref[pl.ds(s * sub, sub), :] = jnp.zeros((sub, D), o_ref.dtype)


def _swiglu_scratch(tm, D, tf):
    return [pltpu.VMEM((2, D, tf), F32), pltpu.VMEM((2, D, tf), F32), pltpu.VMEM((2, tf, D), F32),
            pltpu.SemaphoreType.DMA((3, 2)),
            pltpu.VMEM((tm, tf), BF16), pltpu.VMEM((D, tf), BF16), pltpu.VMEM((D, tf), BF16),
            pltpu.VMEM((tf, D), BF16)]


def _ffn_dense_kernel(yml_ref, yrg_ref, h_ref, wo_ref, g_ref, wg_hbm, wu_hbm, wd_hbm, o_ref, un_ref,
                      gbuf, ubuf, dbuf, sems, *scratch, layer, tf):
    i = pl.program_id(0)
    tm = o_ref.shape[0]
    sub = min(SUB_ROWS, tm)
    half = yml_ref.shape[1]

    for s in range(tm // sub):
        rows = pl.ds(s * sub, sub)
        hn = (h_ref[rows, :] + _dot(yml_ref[rows, :], wo_ref[:half, :])
              + _dot(yrg_ref[rows, :], wo_ref[half:, :]))
        o_ref[rows, :] = hn
        un_ref[rows, :] = (_rms_rows(hn) * g_ref[...]).astype(BF16)

    _swiglu_tile(un_ref, o_ref, None, lambda s: None, (wg_hbm, wu_hbm, wd_hbm), layer, layer,
                 i == 0, i + 1 < pl.num_programs(0), (gbuf, ubuf, dbuf), sems, scratch, tf)


def _ffn_dense(yml, yrg, h, w_out, g_ffn, wg, wu, wd, layer, tm=1024, tf=256):
    T, D = h.shape
    half = yml.shape[1]
    F = wg.shape[2]
    tm = min(tm, T)
    tf = min(tf, F)
    row = lambda i: (i, 0)
    const = lambda i: (0, 0)
    hbm = pl.BlockSpec(memory_space=pl.ANY)
    kern = functools.partial(_ffn_dense_kernel, layer=layer, tf=tf)
    return pl.pallas_call(
        kern,
        grid=(T // tm,),
        in_specs=[pl.BlockSpec((tm, half), row), pl.BlockSpec((tm, half), row), pl.BlockSpec((tm, D), row),
                  pl.BlockSpec((2 * half, D), const), pl.BlockSpec((1, D), const), hbm, hbm, hbm],
        out_specs=pl.BlockSpec((tm, D), row),
        out_shape=jax.ShapeDtypeStruct((T, D), F32),
        scratch_shapes=[pltpu.VMEM((tm, D), BF16)] + _swiglu_scratch(tm, D, tf),
        compiler_params=_cparams("arbitrary"),
        name="ffn_dense",
    )(yml, yrg, h, w_out, g_ffn, wg, wu, wd)


def _ffn_grouped_kernel(te_ref, tr_ref, nt_ref, x_ref, wg_hbm, wu_hbm, wd_hbm, o_ref, xb_ref,
                        gbuf, ubuf, dbuf, sems, *scratch, first_expert, tf):
    i = pl.program_id(0)
    sub = min(SUB_ROWS, o_ref.shape[0])
    n_slots = pl.num_programs(0)

    @pl.when(i < nt_ref[0])
    def _():
        xb_ref[...] = x_ref[...].astype(BF16)
        e_next = first_expert + te_ref[jnp.minimum(i + 1, n_slots - 1)]
        _swiglu_tile(xb_ref, o_ref, lambda rows: jnp.zeros((rows.size, o_ref.shape[1]), F32),
                     lambda s: tr_ref[i] - s * sub, (wg_hbm, wu_hbm, wd_hbm), first_expert + te_ref[i],
                     e_next, i == 0, i + 1 < nt_ref[0], (gbuf, ubuf, dbuf), sems, scratch, tf)

    @pl.when(i >= nt_ref[0])
    def _():
        o_ref[...] = jnp.zeros_like(o_ref)


def _ffn_grouped(tile_expert, tile_rows, n_tiles, xs, wg, wu, wd, first_expert, tm, tf=512):
    P, D = xs.shape
    F = wg.shape[2]
    tf = min(tf, F)
    hbm = pl.BlockSpec(memory_space=pl.ANY)
    grid_spec = pltpu.PrefetchScalarGridSpec(
        num_scalar_prefetch=3,
        grid=(P // tm,),
        in_specs=[pl.BlockSpec((tm, D), lambda i, te, tr, nt: (jnp.minimum(i, nt[0] - 1), 0)), hbm, hbm, hbm],
        out_specs=pl.BlockSpec((tm, D), lambda i, te, tr, nt: (i, 0)),
        scratch_shapes=[pltpu.VMEM((tm, D), BF16)] + _swiglu_scratch(tm, D, tf),
    )
    kern = functools.partial(_ffn_grouped_kernel, first_expert=first_expert, tf=tf)
    return pl.pallas_call(
        kern,
        grid_spec=grid_spec,
        out_shape=jax.ShapeDtypeStruct((P, D), F32),
        compiler_params=_cparams("arbitrary"),
        name="ffn_grouped",
    )(tile_expert, tile_rows, n_tiles, xs, wg, wu, wd)


def _router_kernel(lg_ref, rt_ref, rtt_ref, cnt_ref, carry_ref):
    tm = lg_ref.shape[0]

    @pl.when(pl.program_id(0) == 0)
    def _():
        carry_ref[...] = jnp.zeros_like(carry_ref)

    lane = lax.broadcasted_iota(jnp.int32, (tm, LANES), 1).astype(F32)
    lg = jnp.where(lane < N_EXPERTS, lg_ref[...], -jnp.inf)
    v1 = jnp.max(lg, axis=1, keepdims=True)
    e1 = jnp.min(jnp.where(lg == v1, lane, float(LANES)), axis=1, keepdims=True)
    lg2 = jnp.where(lane == e1, -jnp.inf, lg)
    v2 = jnp.max(lg2, axis=1, keepdims=True)
    e2 = jnp.min(jnp.where(lg2 == v2, lane, float(LANES)), axis=1, keepdims=True)
    ex = jnp.exp(v2 - v1)
    w1 = 1.0 / (1.0 + ex)
    w2 = ex / (1.0 + ex)

    oh1 = (lane == e1).astype(F32)
    oh2 = (lane == e2).astype(F32)
    r_i = lax.broadcasted_iota(jnp.int32, (tm, tm), 0)
    c_i = lax.broadcasted_iota(jnp.int32, (tm, tm), 1)
    strict = (c_i < r_i).astype(BF16)
    before = _dot(strict, (oh1 + oh2).astype(BF16)) + carry_ref[0:1, :]
    rank1 = jnp.sum(before * oh1, axis=1, keepdims=True)
    rank2 = jnp.sum(before * oh2, axis=1, keepdims=True)
    total = carry_ref[0:1, :] + jnp.sum(oh1 + oh2, axis=0, keepdims=True)
    carry_ref[...] = jnp.broadcast_to(total, carry_ref.shape)
    cnt_ref[...] = jnp.broadcast_to(total, cnt_ref.shape)

    out = jnp.where(lane == 0, e1, 0.0)
    out = jnp.where(lane == 1, e2, out)
    out = jnp.where(lane == 2, w1, out)
    out = jnp.where(lane == 3, w2, out)
    out = jnp.where(lane == 4, rank1, out)
    out = jnp.where(lane == 5, rank2, out)
    rt_ref[...] = out
    rtt_ref[...] = jnp.transpose(out)[:SUBLANES, :]


def _router(logits, tm=1024):
    T = logits.shape[0]
    tm = min(tm, T)
    return pl.pallas_call(
        _router_kernel,
        grid=(T // tm,),
        in_specs=[pl.BlockSpec((tm, LANES), lambda i: (i, 0))],
        out_specs=[
            pl.BlockSpec((tm, LANES), lambda i: (i, 0)),
            pl.BlockSpec((SUBLANES, tm), lambda i: (0, i)),
            pl.BlockSpec((SUBLANES, LANES), lambda i: (0, 0)),
        ],
        out_shape=[
            jax.ShapeDtypeStruct((T, LANES), F32),
            jax.ShapeDtypeStruct((SUBLANES, T), F32),
            jax.ShapeDtypeStruct((SUBLANES, LANES), F32),
        ],
        scratch_shapes=[pltpu.VMEM((SUBLANES, LANES), F32)],
        compiler_params=_cparams("arbitrary"),
        name="router",
    )(logits)


def _dispatch_kernel(p1_ref, p2_ref, pad0_ref, padn_ref, nt_ref, u_ref, xs_hbm, zbuf_ref, sem, zsem,
                     *, pad_bits, tail_per_tile):
    tm = u_ref.shape[0]

    @pl.when(pl.program_id(0) == 0)
    def _():
        zbuf_ref[...] = jnp.zeros_like(zbuf_ref)

        def pad_copies(e, b):
            n = padn_ref[e]
            off = pad0_ref[e] + (n & ((1 << b) - 1))
            if (1 << b) >= SUBLANES:
                off = pl.multiple_of(off, SUBLANES)
                return [pltpu.make_async_copy(zbuf_ref.at[pl.ds(0, 1 << b)],
                                              xs_hbm.at[pl.ds(off, 1 << b)], zsem)]
            return [pltpu.make_async_copy(zbuf_ref.at[pl.ds(0, 1)], xs_hbm.at[pl.ds(off + r, 1)], zsem)
                    for r in range(1 << b)]

        for wait in (False, True):
            for e in range(N_EXPERTS):
                for b in range(pad_bits):
                    @pl.when(((padn_ref[e] >> b) & 1) == 1)
                    def _():
                        for cp in pad_copies(e, b):
                            if wait:
                                cp.wait()
                            else:
                                cp.start()

        zrows = zbuf_ref.shape[0]

        def tail_copy(t):
            off = pl.multiple_of(t * zrows, SUBLANES)
            return pltpu.make_async_copy(zbuf_ref, xs_hbm.at[pl.ds(off, zrows)], zsem)

        def tail_start(t, carry):
            tail_copy(t).start()
            return carry

        def tail_wait(t, carry):
            tail_copy(t).wait()
            return carry

        first, last = nt_ref[0] * tail_per_tile, xs_hbm.shape[0] // zrows
        lax.fori_loop(first, last, tail_start, 0)
        lax.fori_loop(first, last, tail_wait, 0)

    def row_copy(r, pos):
        return pltpu.make_async_copy(u_ref.at[pl.ds(r, 1)], xs_hbm.at[pl.ds(pos, 1)], sem)

    def issue(b, carry):
        for rr in range(ISSUE_UNROLL):
            r = b * ISSUE_UNROLL + rr
            row_copy(r, p1_ref[r]).start()
            row_copy(r, p2_ref[r]).start()
        return carry

    lax.fori_loop(0, tm // ISSUE_UNROLL, issue, 0)
    for _ in range(2):
        pltpu.make_async_copy(u_ref, xs_hbm.at[pl.ds(0, tm)], sem).wait()


def _dispatch(pos1, pos2, pad_start, pad_len, n_tiles, un, P, group_tile, tm=1024):
    T, D = un.shape
    tm = min(tm, T)
    pad_bits = group_tile.bit_length() - 1
    assert group_tile == 1 << pad_bits and pad_bits >= 1
    zrows = group_tile // 2
    kern = functools.partial(_dispatch_kernel, pad_bits=pad_bits, tail_per_tile=group_tile // zrows)
    smem = pl.BlockSpec(memory_space=pltpu.SMEM)
    return pl.pallas_call(
        kern,
        grid=(T // tm,),
        in_specs=[
            pl.BlockSpec((tm,), lambda i: (i,), memory_space=pltpu.SMEM),
            pl.BlockSpec((tm,), lambda i: (i,), memory_space=pltpu.SMEM),
            smem, smem, smem,
            pl.BlockSpec((tm, D), lambda i: (i, 0)),
        ],
        out_specs=pl.BlockSpec(memory_space=pl.ANY),
        out_shape=jax.ShapeDtypeStruct((P, D), un.dtype),
        scratch_shapes=[pltpu.VMEM((zrows, D), un.dtype),
                        pltpu.SemaphoreType.DMA(()), pltpu.SemaphoreType.DMA(())],
        compiler_params=_cparams("arbitrary"),
        name="dispatch",
    )(pos1, pos2, pad_start, pad_len, n_tiles, un)


def _combine_kernel(p1_ref, p2_ref, q1_ref, q2_ref, h_ref, rt_ref, g_ref, y_hbm, o_ref, buf_ref, sem,
                    *, final_norm):
    tm = h_ref.shape[0]
    i = pl.program_id(0)
    n = pl.num_programs(0)
    slot = i % 2

    def issue_tile(i1_ref, i2_ref, s):
        def row_copy(k, r, pos):
            return pltpu.make_async_copy(y_hbm.at[pl.ds(pos, 1)], buf_ref.at[s, k, pl.ds(r, 1)], sem.at[s])

        def issue(b, carry):
            for rr in range(ISSUE_UNROLL):
                r = b * ISSUE_UNROLL + rr
                row_copy(0, r, i1_ref[r]).start()
                row_copy(1, r, i2_ref[r]).start()
            return carry

        lax.fori_loop(0, tm // ISSUE_UNROLL, issue, 0)

    @pl.when(i == 0)
    def _():
        issue_tile(p1_ref, p2_ref, 0)

    @pl.when(i + 1 < n)
    def _():
        issue_tile(q1_ref, q2_ref, 1 - slot)

    for k in range(2):
        pltpu.make_async_copy(y_hbm.at[pl.ds(0, tm)], buf_ref.at[slot, k], sem.at[slot]).wait()

    rt = rt_ref[...]
    out = h_ref[...] + (rt[:, 2:3] * buf_ref[slot, 0] + rt[:, 3:4] * buf_ref[slot, 1])
    if final_norm:
        out = _rms_rows(out) * g_ref[...]
    o_ref[...] = out


def _combine(pos1, pos2, h, routing, g_final, y, final_norm, tm=512):
    T, D = h.shape
    tm = min(tm, T)
    n = T // tm
    kern = functools.partial(_combine_kernel, final_norm=final_norm)
    cur = pl.BlockSpec((tm,), lambda i: (i,), memory_space=pltpu.SMEM)
    nxt = pl.BlockSpec((tm,), lambda i: (jnp.minimum(i + 1, n - 1),), memory_space=pltpu.SMEM)
    return pl.pallas_call(
        kern,
        grid=(n,),
        in_specs=[
            cur, cur, nxt, nxt,
            pl.BlockSpec((tm, D), lambda i: (i, 0)),
            pl.BlockSpec((tm, LANES), lambda i: (i, 0)),
            pl.BlockSpec((1, D), lambda i: (0, 0)),
            pl.BlockSpec(memory_space=pl.ANY),
        ],
        out_specs=pl.BlockSpec((tm, D), lambda i: (i, 0)),
        out_shape=jax.ShapeDtypeStruct((T, D), F32),
        scratch_shapes=[pltpu.VMEM((2, 2, tm, D), F32), pltpu.SemaphoreType.DMA((2,))],
        compiler_params=_cparams("arbitrary"),
        name="combine",
    )(pos1, pos2, pos1, pos2, h, routing, g_final, y)


def _final_norm_kernel(h_ref, g_ref, o_ref):
    o_ref[...] = _rms_rows(h_ref[...]) * g_ref[...]


def _final_norm(h, g, tm=1024):
    T, D = h.shape
    tm = min(tm, T)
    return pl.pallas_call(
        _final_norm_kernel,
        grid=(T // tm,),
        in_specs=[pl.BlockSpec((tm, D), lambda i: (i, 0)), pl.BlockSpec((1, D), lambda i: (0, 0))],
        out_specs=pl.BlockSpec((tm, D), lambda i: (i, 0)),
        out_shape=jax.ShapeDtypeStruct((T, D), F32),
        compiler_params=_cparams("parallel"),
        name="final_norm",
    )(h, g)


def _block_diag(w):
    G, n, _ = w.shape
    eye = jnp.eye(G, dtype=w.dtype)
    return (eye[:, None, :, None] * w[:, :, None, :]).reshape(G * n, G * n)


def _moe_layer(h, un, logits, w_gate, w_up, w_down, first_expert, g_final, final_norm, tm):
    T, D = h.shape
    E = N_EXPERTS
    routing, routing_t, counts = _router(logits)
    e1 = routing_t[0].astype(jnp.int32)
    e2 = routing_t[1].astype(jnp.int32)
    rank1 = routing_t[4].astype(jnp.int32)
    rank2 = routing_t[5].astype(jnp.int32)
    cnt = counts[0, :E].astype(jnp.int32)
    tiles_per = (cnt + tm - 1) // tm
    tile_end = jnp.cumsum(tiles_per)
    offs = (tile_end - tiles_per) * tm
    pos1 = offs[e1] + rank1
    pos2 = offs[e2] + rank2
    n_slots = (2 * T) // tm + E
    P = n_slots * tm
    tile_expert = jnp.minimum(
        jnp.sum(jnp.arange(n_slots, dtype=jnp.int32)[:, None] >= tile_end[None, :], axis=1), E - 1
    ).astype(jnp.int32)
    n_tiles = tile_end[E - 1:E].astype(jnp.int32)
    tile_in_group = jnp.arange(n_slots, dtype=jnp.int32) - (tile_end - tiles_per)[tile_expert]
    tile_rows = jnp.clip(cnt[tile_expert] - tile_in_group * tm, 0, tm).astype(jnp.int32)

    xs = _dispatch(pos1, pos2, offs + cnt, tiles_per * tm - cnt, n_tiles, un, P, tm)
    y = _ffn_grouped(tile_expert, tile_rows, n_tiles, xs, w_gate, w_up, w_down, first_expert, tm)
    return _combine(pos1, pos2, h, routing, g_final, y, final_norm)


def _forward(x, norm_mix_g, w_in, ml_b_if, ml_norm_g, rg_conv_w, rg_conv_b, rg_w_a, rg_b_a,
             rg_w_x, rg_b_x, rg_lam, rg_norm_g, w_out, norm_ffn_g, ffn_w_gate, ffn_w_up,
             ffn_w_down, moe_w_router, moe_w_gate, moe_w_up, moe_w_down, norm_final_g,
             moe_tile=2048):
    B, S, D = x.shape
    T = B * S
    depth = w_in.shape[0]
    ml_w = ML_HEADS * ML_HEAD_DIM
    n_q = 4 * ml_w
    n_if = 2 * ML_HEADS
    rg_w = rg_lam.shape[1]
    n_r = 2 * rg_w

    h = x.reshape(T, D)
    g_final = norm_final_g.reshape(1, D)
    moe_gate_all = moe_w_gate.reshape((-1,) + moe_w_gate.shape[2:])
    moe_up_all = moe_w_up.reshape((-1,) + moe_w_up.shape[2:])
    moe_down_all = moe_w_down.reshape((-1,) + moe_w_down.shape[2:])
    for l in range(depth):
        wl = w_in[l]
        w_all = jnp.concatenate(
            [wl[:, :ml_w], wl[:, 2 * ml_w:n_q], wl[:, n_q + n_if:], wl[:, n_q:n_q + n_if],
             jnp.zeros((D, LANES - n_if), wl.dtype)], axis=1).astype(BF16)
        wt = jnp.concatenate([wl[:, ml_w:2 * ml_w], wl[:, n_q:n_q + n_if]], axis=1).T.astype(BF16)
        b_if = jnp.concatenate([ml_b_if[l], jnp.zeros((LANES - n_if,), F32)]).reshape(1, LANES)
        zq, kt, zr, zg, zgt = _inproj(h, norm_mix_g[l].reshape(1, D), b_if, ml_b_if[l].reshape(n_if, 1),
                                      w_all, wt, 3 * ml_w, n_r)
        yml = _mlstm(zq, kt, zg, zgt, ml_norm_g[l].reshape(1, ml_w), B, S)

        w_gates = jnp.concatenate([_block_diag(rg_w_a[l]), _block_diag(rg_w_x[l])], axis=1).astype(BF16)
        b_gates = jnp.concatenate([rg_b_a[l], rg_b_x[l]]).reshape(1, n_r)
        yrg = _rglru(zr, rg_conv_w[l], rg_conv_b[l].reshape(1, rg_w), w_gates, b_gates,
                     rg_lam[l].reshape(1, rg_w), rg_norm_g[l].reshape(1, rg_w), B, S)

        j = l // 2
        is_moe = (l % 2 == 1)
        wo = w_out[l].astype(BF16)
        g_ffn = norm_ffn_g[l].reshape(1, D)
        last = (l == depth - 1)
        if is_moe:
            w_router = jnp.concatenate(
                [moe_w_router[j], jnp.zeros((D, LANES - N_EXPERTS), F32)], axis=1)
            h, un, logits = _outproj(yml, yrg, h, wo, g_ffn, w_router)
            h = _moe_layer(h, un, logits, moe_gate_all, moe_up_all, moe_down_all, j * N_EXPERTS,
                           g_final, last, moe_tile)
        else:
            h = _ffn_dense(yml, yrg, h, wo, g_ffn, ffn_w_gate, ffn_w_up, ffn_w_down, j)
            if last:
                h = _final_norm(h, g_final)
    return h.reshape(B, S, D)


def kernel(x, norm_mix_g, w_in, ml_b_if, ml_norm_g, rg_conv_w, rg_conv_b, rg_w_a, rg_b_a, rg_w_x,
           rg_b_x, rg_lam, rg_norm_g, w_out, norm_ffn_g, ffn_w_gate, ffn_w_up, ffn_w_down,
           moe_w_router, moe_w_gate, moe_w_up, moe_w_down, norm_final_g):
    return _forward(x, norm_mix_g, w_in, ml_b_if, ml_norm_g, rg_conv_w, rg_conv_b, rg_w_a, rg_b_a,
                    rg_w_x, rg_b_x, rg_lam, rg_norm_g, w_out, norm_ffn_g, ffn_w_gate, ffn_w_up,
                    ffn_w_down, moe_w_router, moe_w_gate, moe_w_up, moe_w_down, norm_final_g)
```

```python
import functools

import jax
import jax.numpy as jnp
from jax import lax
from jax.experimental import pallas as pl
from jax.experimental.pallas import tpu as pltpu

EPS = 1e-6
ML_HEADS = 4
ML_HEAD_DIM = 128
CHUNK = 128
RG_BLOCKS = 8
RG_C = 8.0
CONV_WIDTH = 4
N_EXPERTS = 8
LANES = 128
SUBLANES = 8
VMEM_LIMIT = 60 * 1024 * 1024
ISSUE_UNROLL = 8

BF16 = jnp.bfloat16
F32 = jnp.float32


def _cparams(*sem):
    return pltpu.CompilerParams(dimension_semantics=sem, vmem_limit_bytes=VMEM_LIMIT)


def _dot(a, b):
    return jnp.dot(a, b, preferred_element_type=F32)


def _dot_nt(a, b):
    return lax.dot_general(a, b, (((1,), (1,)), ((), ())), preferred_element_type=F32)


def _dot_tn(a, b):
    return lax.dot_general(a, b, (((0,), (0,)), ((), ())), preferred_element_type=F32)


def _dot_f32(a, b):
    return jnp.dot(a, b, preferred_element_type=F32, precision=lax.Precision.HIGHEST)


def _sigmoid(x):
    return 1.0 / (1.0 + jnp.exp(-x))


def _log_sigmoid(x):
    return jnp.minimum(x, 0.0) - jnp.log(1.0 + jnp.exp(-jnp.abs(x)))


def _softplus(x):
    return jnp.maximum(x, 0.0) + jnp.log(1.0 + jnp.exp(-jnp.abs(x)))


def _gelu_tanh(x):
    return 0.5 * x * (1.0 + jnp.tanh(0.7978845608028654 * (x + 0.044715 * (x * x * x))))


def _rms_rows(x):
    return x * lax.rsqrt(jnp.mean(x * x, axis=-1, keepdims=True) + EPS)


def _inproj_kernel(h_ref, g_ref, b_ref, bt_ref, w_ref, wt_ref, zq_ref, kt_ref, zr_ref, zg_ref, zgt_ref,
                   *, n_q, n_r, n_col):
    xn = (_rms_rows(h_ref[...]) * g_ref[...]).astype(BF16)
    for c0 in range(0, n_q, n_col):
        zq_ref[:, c0:c0 + n_col] = _dot(xn, w_ref[:, c0:c0 + n_col]).astype(BF16)
    for c0 in range(0, n_r, n_col):
        zr_ref[:, c0:c0 + n_col] = _dot(xn, w_ref[:, n_q + c0:n_q + c0 + n_col]).astype(BF16)
    zg_ref[...] = _dot(xn, w_ref[:, n_q + n_r:]) + b_ref[...]
    n_k = kt_ref.shape[1]
    t = _dot_nt(wt_ref[...], xn)
    for c in range(kt_ref.shape[0]):
        kt_ref[c] = t[:n_k, c * CHUNK:(c + 1) * CHUNK].astype(BF16)
        zgt_ref[c] = t[n_k:, c * CHUNK:(c + 1) * CHUNK] + bt_ref[...]


def _inproj(h, g, b_if, b_if_t, w_all, wt, n_q, n_r, tm=1024):
    T, D = h.shape
    tm = min(tm, T)
    n_all = w_all.shape[1]
    n_g = n_all - n_q - n_r
    n_k = wt.shape[0] - SUBLANES
    kern = functools.partial(_inproj_kernel, n_q=n_q, n_r=n_r, n_col=512)
    const = lambda i: (0, 0)
    return pl.pallas_call(
        kern,
        grid=(T // tm,),
        in_specs=[
            pl.BlockSpec((tm, D), lambda i: (i, 0)),
            pl.BlockSpec((1, D), const),
            pl.BlockSpec((1, n_g), const),
            pl.BlockSpec((SUBLANES, 1), const),
            pl.BlockSpec((D, n_all), const),
            pl.BlockSpec((n_k + SUBLANES, D), const),
        ],
        out_specs=[
            pl.BlockSpec((tm, n_q), lambda i: (i, 0)),
            pl.BlockSpec((tm // CHUNK, n_k, CHUNK), lambda i: (i, 0, 0)),
            pl.BlockSpec((tm, n_r), lambda i: (i, 0)),
            pl.BlockSpec((tm, n_g), lambda i: (i, 0)),
            pl.BlockSpec((tm // CHUNK, SUBLANES, CHUNK), lambda i: (i, 0, 0)),
        ],
        out_shape=[
            jax.ShapeDtypeStruct((T, n_q), BF16),
            jax.ShapeDtypeStruct((T // CHUNK, n_k, CHUNK), BF16),
            jax.ShapeDtypeStruct((T, n_r), BF16),
            jax.ShapeDtypeStruct((T, n_g), F32),
            jax.ShapeDtypeStruct((T // CHUNK, SUBLANES, CHUNK), F32),
        ],
        compiler_params=_cparams("parallel"),
        name="inproj",
    )(h, g, b_if, b_if_t, w_all, wt)


def _mlstm_kernel(zq_ref, kt_ref, zg_ref, zgt_ref, gain_ref, y_ref, c_ref, m_ref, pcol_ref, prow_ref,
                  *, n_chunks):
    H, Dh, L = ML_HEADS, ML_HEAD_DIM, CHUNK
    W = H * Dh
    scale = Dh ** -0.5

    @pl.when(pl.program_id(1) == 0)
    def _():
        c_ref[...] = jnp.zeros_like(c_ref)
        m_ref[...] = jnp.zeros_like(m_ref)

    row = lax.broadcasted_iota(jnp.int32, (L, L), 0)
    col = lax.broadcasted_iota(jnp.int32, (L, L), 1)
    causal = col <= row
    tri_l = causal.astype(F32)
    tri_u = (row <= col).astype(F32)
    ones_blk = jnp.ones((L, Dh), BF16)

    def gate_prefix(c, slot):
        r0 = pl.multiple_of(c * L, L)
        gcol = zg_ref[pl.ds(r0, L), :]
        grow = zgt_ref[c]
        bcol_all = _dot_f32(tri_l, _log_sigmoid(gcol))
        brow_all = _dot_f32(_log_sigmoid(grow), tri_u)
        cm = gcol - pltpu.roll(bcol_all, LANES - H, 1)
        d = 1
        while d < L:
            cm = jnp.maximum(cm, jnp.where(row >= d, pltpu.roll(cm, d, 0), -jnp.inf))
            d *= 2
        pcol_ref[slot, 0] = bcol_all
        pcol_ref[slot, 1] = cm
        prow_ref[slot] = brow_all

    gate_prefix(0, 0)

    def chunk_body(c, carry):
        r0 = pl.multiple_of(c * L, L)
        slot = c % 2
        grow = zgt_ref[c]
        bcol_all = pcol_ref[slot, 0]
        cm = pcol_ref[slot, 1]
        brow_all = prow_ref[slot]

        for hd in range(H):
            q = zq_ref[pl.ds(r0, L), hd * Dh:(hd + 1) * Dh]
            v = zq_ref[pl.ds(r0, L), W + hd * Dh:W + (hd + 1) * Dh]
            o = zq_ref[pl.ds(r0, L), 2 * W + hd * Dh:2 * W + (hd + 1) * Dh]
            kt = kt_ref[c, hd * Dh:(hd + 1) * Dh, :]
            v_aug = jnp.concatenate([v, ones_blk], axis=1)

            li_row = grow[hd:hd + 1, :]
            b_row = brow_all[H + hd:H + hd + 1, :]
            g_tot = b_row[:, L - 1:L]
            r_row = li_row - b_row

            c_prev = c_ref[hd]
            m_prev = m_ref[hd]

            mx = jnp.maximum(jnp.broadcast_to(cm[:, hd:hd + 1], (L, L)), m_prev)
            bb = jnp.broadcast_to(bcol_all[:, H + hd:H + hd + 1], (L, L))
            p = jnp.exp(jnp.where(causal, r_row - mx, -jnp.inf))
            s_w = p * (_dot(q, kt) * scale)
            inter = jnp.exp(m_prev - mx)
            intra = _dot(s_w.astype(BF16), v_aug)
            cross = _dot(q, c_prev.astype(BF16))
            num = intra[:, :Dh] + inter * cross[:, :Dh]
            den = intra[:, Dh:] + inter * cross[:, Dh:]
            hh = num / jnp.maximum(jnp.abs(den), jnp.exp(-(bb + mx)))
            ht = _rms_rows(hh) * gain_ref[:, hd * Dh:(hd + 1) * Dh]
            y_ref[pl.ds(r0, L), hd * Dh:(hd + 1) * Dh] = (_sigmoid(o.astype(F32)) * ht).astype(BF16)

            a = g_tot + r_row
            m_loc = jnp.max(a, axis=1, keepdims=True)
            w = jnp.exp(a - m_loc) * scale
            c_loc = _dot((kt.astype(F32) * w).astype(BF16), v_aug)
            m_new = jnp.maximum(g_tot + m_prev, m_loc)
            s_old = jnp.exp(g_tot + m_prev - m_new)
            s_loc = jnp.exp(m_loc - m_new)
            c_ref[hd] = s_old * c_prev + s_loc * c_loc
            m_ref[hd] = m_new
        gate_prefix(jnp.minimum(c + 1, n_chunks - 1), 1 - slot)
        return carry

    lax.fori_loop(0, n_chunks, chunk_body, 0)


def _mlstm(zq, kt, zg, zgt, gain, B, S, rows=2048):
    T = B * S
    rows = min(rows, S)
    n_chunks = rows // CHUNK
    steps = S // rows
    W = ML_HEADS * ML_HEAD_DIM
    kern = functools.partial(_mlstm_kernel, n_chunks=n_chunks)
    return pl.pallas_call(
        kern,
        grid=(B, steps),
        in_specs=[
            pl.BlockSpec((rows, 3 * W), lambda b, s: (b * steps + s, 0)),
            pl.BlockSpec((n_chunks, W, CHUNK), lambda b, s: (b * steps + s, 0, 0)),
            pl.BlockSpec((rows, LANES), lambda b, s: (b * steps + s, 0)),
            pl.BlockSpec((n_chunks, SUBLANES, CHUNK), lambda b, s: (b * steps + s, 0, 0)),
            pl.BlockSpec((1, W), lambda b, s: (0, 0)),
        ],
        out_specs=pl.BlockSpec((rows, W), lambda b, s: (b * steps + s, 0)),
        out_shape=jax.ShapeDtypeStruct((T, W), BF16),
        scratch_shapes=[
            pltpu.VMEM((ML_HEADS, ML_HEAD_DIM, 2 * ML_HEAD_DIM), F32),
            pltpu.VMEM((ML_HEADS, 1, 1), F32),
            pltpu.VMEM((2, 2, CHUNK, LANES), F32),
            pltpu.VMEM((2, SUBLANES, CHUNK), F32),
        ],
        compiler_params=_cparams("parallel", "arbitrary"),
        name="mlstm",
    )(zq, kt, zg, zgt, gain)


def _rglru_kernel(zr_ref, cw_ref, cb_ref, wg_ref, bg_ref, lam_ref, gain_ref, y_ref,
                  nat_ref, ost_ref, a_ref, u_ref, p_ref, cx_ref, hc_ref, *, rows):
    Wd = y_ref.shape[1]
    NB = SUBLANES
    BL = rows // NB
    PITCH = BL + SUBLANES
    n_slab = Wd // LANES
    taps = CONV_WIDTH - 1

    @pl.when(pl.program_id(1) == 0)
    def _():
        cx_ref[...] = jnp.zeros_like(cx_ref)
        hc_ref[...] = jnp.zeros_like(hc_ref)

    for s in range(NB):
        blk = zr_ref[s * BL:(s + 1) * BL, :].astype(F32)
        for c in range(2 * n_slab):
            nat_ref[c, s * PITCH:s * PITCH + BL, :] = blk[:, c * LANES:(c + 1) * LANES]

    def interleaved(c):
        return jnp.concatenate([nat_ref[c, pl.ds(i, NB, stride=PITCH), :] for i in range(BL)], axis=0)

    first_sub = lax.broadcasted_iota(jnp.int32, (NB, LANES), 0) == 0
    xcs = []
    for c in range(n_slab):
        lanes = slice(c * LANES, (c + 1) * LANES)
        x = interleaved(c)
        head = []
        for k in range(taps, 0, -1):
            tail = x[(BL - k) * NB:(BL - k + 1) * NB, :]
            prev = cx_ref[c * taps + k - 1]
            head.append(jnp.where(first_sub, pltpu.roll(prev, 1, 0), pltpu.roll(tail, 1, 0)))
            cx_ref[c * taps + k - 1] = tail
        ext = jnp.concatenate(head + [x], axis=0)
        xc = cb_ref[:, lanes] + jnp.zeros((rows, LANES), F32)
        for j in range(CONV_WIDTH):
            xc = xc + cw_ref[j:j + 1, lanes] * ext[j * NB:j * NB + rows, :]
        xcs.append(xc)
    xc = jnp.concatenate(xcs, axis=1)

    gates = _dot(xc.astype(BF16), wg_ref[...]) + bg_ref[...]
    r = _sigmoid(gates[:, :Wd])
    ig = _sigmoid(gates[:, Wd:])
    a = jnp.exp(r * ((-RG_C) * _softplus(-lam_ref[...])))
    a_ref[...] = a
    d = 1.0 - a * a
    u_ref[...] = jnp.where(d > 0.0, d * lax.rsqrt(d), 0.0) * (ig * xc)

    for c in range(n_slab):
        lanes = slice(c * LANES, (c + 1) * LANES)
        h = jnp.zeros((NB, LANES), F32)
        p = jnp.ones((NB, LANES), F32)
        for i in range(BL):
            ai = a_ref[i * NB:(i + 1) * NB, lanes]
            h = ai * h + u_ref[i * NB:(i + 1) * NB, lanes]
            p = ai * p
            u_ref[i * NB:(i + 1) * NB, lanes] = h
            p_ref[i * NB:(i + 1) * NB, lanes] = p
    h_end = u_ref[(BL - 1) * NB:BL * NB, :]
    p_end = p_ref[(BL - 1) * NB:BL * NB, :]
    h_in = [hc_ref[0:1, :]]
    for s in range(NB):
        h_in.append(h_end[s:s + 1, :] + p_end[s:s + 1, :] * h_in[s])
    hc_ref[...] = jnp.broadcast_to(h_in[NB], hc_ref.shape)
    h_enter = jnp.concatenate(h_in[:NB], axis=0)
    h_all = u_ref[...] + p_ref[...] * jnp.concatenate([h_enter] * BL, axis=0)

    gate = jnp.concatenate([interleaved(n_slab + c) for c in range(n_slab)], axis=1)
    yv = h_all * _gelu_tanh(gate)
    yn = _rms_rows(yv) * gain_ref[...]
    for c in range(n_slab):
        for i in range(BL):
            ost_ref[c, pl.ds(i, NB, stride=PITCH), :] = yn[i * NB:(i + 1) * NB, c * LANES:(c + 1) * LANES]
    for s in range(NB):
        y_ref[s * BL:(s + 1) * BL, :] = jnp.concatenate(
            [ost_ref[c, s * PITCH:s * PITCH + BL, :] for c in range(n_slab)], axis=1).astype(BF16)


def _rglru(zr, conv_w, conv_b, w_gates, b_gates, lam, gain, B, S, rows=1024):
    T = B * S
    Wd = zr.shape[1] // 2
    rows = min(rows, S)
    steps = S // rows
    pitch_rows = SUBLANES * (rows // SUBLANES + SUBLANES)
    n_slab = Wd // LANES
    kern = functools.partial(_rglru_kernel, rows=rows)
    const = lambda b, s: (0, 0)
    return pl.pallas_call(
        kern,
        grid=(B, steps),
        in_specs=[
            pl.BlockSpec((rows, 2 * Wd), lambda b, s: (b * steps + s, 0)),
            pl.BlockSpec((CONV_WIDTH, Wd), const),
            pl.BlockSpec((1, Wd), const),
            pl.BlockSpec((Wd, 2 * Wd), const),
            pl.BlockSpec((1, 2 * Wd), const),
            pl.BlockSpec((1, Wd), const),
            pl.BlockSpec((1, Wd), const),
        ],
        out_specs=pl.BlockSpec((rows, Wd), lambda b, s: (b * steps + s, 0)),
        out_shape=jax.ShapeDtypeStruct((T, Wd), BF16),
        scratch_shapes=[
            pltpu.VMEM((2 * n_slab, pitch_rows, LANES), F32),
            pltpu.VMEM((n_slab, pitch_rows, LANES), F32),
            pltpu.VMEM((rows, Wd), F32),
            pltpu.VMEM((rows, Wd), F32),
            pltpu.VMEM((rows, Wd), F32),
            pltpu.VMEM((n_slab * (CONV_WIDTH - 1), SUBLANES, LANES), F32),
            pltpu.VMEM((SUBLANES, Wd), F32),
        ],
        compiler_params=_cparams("parallel", "arbitrary"),
        name="rglru",
    )(zr, conv_w, conv_b, w_gates, b_gates, lam, gain)


def _outproj_kernel(yml_ref, yrg_ref, h_ref, w_ref, g_ref, wr_ref, hn_ref, un_ref, lg_ref):
    half = yml_ref.shape[1]
    hn = (h_ref[...] + _dot(yml_ref[...], w_ref[:half, :]) + _dot(yrg_ref[...], w_ref[half:, :]))
    un = _rms_rows(hn) * g_ref[...]
    wr = wr_ref[...]
    w_hi = wr.astype(BF16)
    w_lo = (wr - w_hi.astype(F32)).astype(BF16)
    u_hi = un.astype(BF16)
    u_lo = (un - u_hi.astype(F32)).astype(BF16)
    lg_ref[...] = _dot(u_hi, w_hi) + (_dot(u_hi, w_lo) + _dot(u_lo, w_hi))
    hn_ref[...] = hn
    un_ref[...] = un


def _outproj(yml, yrg, h, w_out, g_ffn, w_router, tm=1024):
    T, D = h.shape
    half = yml.shape[1]
    tm = min(tm, T)
    row = lambda i: (i, 0)
    const = lambda i: (0, 0)
    return pl.pallas_call(
        _outproj_kernel,
        grid=(T // tm,),
        in_specs=[
            pl.BlockSpec((tm, half), row),
            pl.BlockSpec((tm, half), row),
            pl.BlockSpec((tm, D), row),
            pl.BlockSpec((2 * half, D), const),
            pl.BlockSpec((1, D), const),
            pl.BlockSpec((D, LANES), const),
        ],
        out_specs=[pl.BlockSpec((tm, D), row), pl.BlockSpec((tm, D), row), pl.BlockSpec((tm, LANES), row)],
        out_shape=[jax.ShapeDtypeStruct((T, D), F32), jax.ShapeDtypeStruct((T, D), F32),
                   jax.ShapeDtypeStruct((T, LANES), F32)],
        compiler_params=_cparams("parallel"),
        name="outproj_router",
    )(yml, yrg, h, w_out, g_ffn, w_router)


SUB_ROWS = 1024
SUB_STEP = 256


def _swiglu_tile(x_ref, o_ref, init_fn, valid_fn, w_hbm, e_cur, e_next, first, has_next, wbufs, sems,
                 scratch, tf):
    wg_hbm, wu_hbm, wd_hbm = w_hbm
    gbuf, ubuf, dbuf = wbufs
    act_ref, wgb_ref, wub_ref, wdb_ref = scratch
    tm, D = o_ref.shape
    sub = min(SUB_ROWS, tm)
    nj = wg_hbm.shape[2] // tf

    def up_copies(e, j, slot):
        cols = pl.ds(pl.multiple_of(j * tf, LANES), tf)
        return (pltpu.make_async_copy(wg_hbm.at[e, :, cols], gbuf.at[slot], sems.at[0, slot]),
                pltpu.make_async_copy(wu_hbm.at[e, :, cols], ubuf.at[slot], sems.at[1, slot]))

    def down_copy(e, j, slot):
        rws = pl.ds(pl.multiple_of(j * tf, SUBLANES), tf)
        return pltpu.make_async_copy(wd_hbm.at[e, rws, :], dbuf.at[slot], sems.at[2, slot])

    def start_up(e, j, slot):
        for cp in up_copies(e, j, slot):
            cp.start()

    def wait_up(e, j, slot):
        for cp in up_copies(e, j, slot):
            cp.wait()

    def up(rows, s, slot):
        if s == 0:
            wgb_ref[...] = gbuf[slot].astype(BF16)
            wub_ref[...] = ubuf[slot].astype(BF16)
        x = x_ref[rows, :]
        g = _dot(x, wgb_ref[...])
        u = _dot(x, wub_ref[...])
        act_ref[rows, :] = (g * _sigmoid(g) * u).astype(BF16)

    def down(rows, s, slot):
        if s == 0:
            wdb_ref[...] = dbuf[slot].astype(BF16)
        o_ref[rows, :] += _dot(act_ref[rows, :], wdb_ref[...])

    def sub_tiles(body):
        for s in range(tm // sub):
            valid = valid_fn(s)
            if valid is None:
                body(pl.ds(s * sub, sub), s, sub)
                continue
            for z in list(range(SUB_STEP, sub, SUB_STEP)) + [sub]:
                on = valid > z - SUB_STEP
                if z < sub:
                    on = jnp.logical_and(on, valid <= z)
                pl.when(on)(functools.partial(body, pl.ds(s * sub, z), s, z))

    @pl.when(first)
    def _():
        start_up(e_cur, 0, 0)

    wait_up(e_cur, 0, 0)
    if nj > 1:
        start_up(e_cur, 1, 1)
    down_copy(e_cur, 0, 0).start()

    def first_body(rows, s, z):
        up(rows, s, 0)
        if init_fn is not None:
            o_ref[rows, :] = init_fn(rows)

    sub_tiles(first_body)

    def step(j, carry):
        slot = j % 2
        wait_up(e_cur, j, slot)
        down_copy(e_cur, j - 1, 1 - slot).wait()

        @pl.when(j + 1 < nj)
        def _():
            start_up(e_cur, j + 1, 1 - slot)

        down_copy(e_cur, j, slot).start()

        def body(rows, s, z):
            down(rows, s, 1 - slot)
            up(rows, s, slot)

        sub_tiles(body)
        return carry

    lax.fori_loop(1, nj, step, 0)

    last_slot = (nj - 1) % 2
    down_copy(e_cur, nj - 1, last_slot).wait()

    @pl.when(has_next)
    def _():
        start_up(e_next, 0, 0)

    def last_body(rows, s, z):
        down(rows, s, last_slot)
        if z < sub:
            o_ref[pl.ds(s * sub + z, sub - z), :] = jnp.zeros((sub - z, D), o_ref.dtype)

    sub_tiles(last_body)
    for s in range(tm // sub):
        valid = valid_fn(s)
        if valid is not None:
            @pl.when(valid <= 0)
            def _():
                o_ref[pl.ds(s * sub, sub), :] = jnp.zeros((sub, D), o_ref.dtype)


def _swiglu_scratch(tm, D, tf):
    return [pltpu.VMEM((2, D, tf), F32), pltpu.VMEM((2, D, tf), F32), pltpu.VMEM((2, tf, D), F32),
            pltpu.SemaphoreType.DMA((3, 2)),
            pltpu.VMEM((tm, tf), BF16), pltpu.VMEM((D, tf), BF16), pltpu.VMEM((D, tf), BF16),
            pltpu.VMEM((tf, D), BF16)]


def _ffn_dense_kernel(yml_ref, yrg_ref, h_ref, wo_ref, g_ref, wg_hbm, wu_hbm, wd_hbm, o_ref, un_ref,
                      gbuf, ubuf, dbuf, sems, *scratch, layer, tf):
    i = pl.program_id(0)
    tm = o_ref.shape[0]
    sub = min(SUB_ROWS, tm)
    half = yml_ref.shape[1]

    for s in range(tm // sub):
        rows = pl.ds(s * sub, sub)
        hn = (h_ref[rows, :] + _dot(yml_ref[rows, :], wo_ref[:half, :])
              + _dot(yrg_ref[rows, :], wo_ref[half:, :]))
        o_ref[rows, :] = hn
        un_ref[rows, :] = (_rms_rows(hn) * g_ref[...]).astype(BF16)

    _swiglu_tile(un_ref, o_ref, None, lambda s: None, (wg_hbm, wu_hbm, wd_hbm), layer, layer,
                 i == 0, i + 1 < pl.num_programs(0), (gbuf, ubuf, dbuf), sems, scratch, tf)


def _ffn_dense(yml, yrg, h, w_out, g_ffn, wg, wu, wd, layer, tm=1024, tf=256):
    T, D = h.shape
    half = yml.shape[1]
    F = wg.shape[2]
    tm = min(tm, T)
    tf = min(tf, F)
    row = lambda i: (i, 0)
    const = lambda i: (0, 0)
    hbm = pl.BlockSpec(memory_space=pl.ANY)
    kern = functools.partial(_ffn_dense_kernel, layer=layer, tf=tf)
    return pl.pallas_call(
        kern,
        grid=(T // tm,),
        in_specs=[pl.BlockSpec((tm, half), row), pl.BlockSpec((tm, half), row), pl.BlockSpec((tm, D), row),
                  pl.BlockSpec((2 * half, D), const), pl.BlockSpec((1, D), const), hbm, hbm, hbm],
        out_specs=pl.BlockSpec((tm, D), row),
        out_shape=jax.ShapeDtypeStruct((T, D), F32),
        scratch_shapes=[pltpu.VMEM((tm, D), BF16)] + _swiglu_scratch(tm, D, tf),
        compiler_params=_cparams("arbitrary"),
        name="ffn_dense",
    )(yml, yrg, h, w_out, g_ffn, wg, wu, wd)


def _ffn_grouped_kernel(te_ref, tr_ref, nt_ref, x_ref, wg_hbm, wu_hbm, wd_hbm, o_ref, xb_ref,
                        gbuf, ubuf, dbuf, sems, *scratch, first_expert, tf):
    i = pl.program_id(0)
    sub = min(SUB_ROWS, o_ref.shape[0])
    n_slots = pl.num_programs(0)

    @pl.when(i < nt_ref[0])
    def _():
        xb_ref[...] = x_ref[...].astype(BF16)
        e_next = first_expert + te_ref[jnp.minimum(i + 1, n_slots - 1)]
        _swiglu_tile(xb_ref, o_ref, lambda rows: jnp.zeros((rows.size, o_ref.shape[1]), F32),
                     lambda s: tr_ref[i] - s * sub, (wg_hbm, wu_hbm, wd_hbm), first_expert + te_ref[i],
                     e_next, i == 0, i + 1 < nt_ref[0], (gbuf, ubuf, dbuf), sems, scratch, tf)

    @pl.when(i >= nt_ref[0])
    def _():
        o_ref[...] = jnp.zeros_like(o_ref)


def _ffn_grouped(tile_expert, tile_rows, n_tiles, xs, wg, wu, wd, first_expert, tm, tf=512):
    P, D = xs.shape
    F = wg.shape[2]
    tf = min(tf, F)
    hbm = pl.BlockSpec(memory_space=pl.ANY)
    grid_spec = pltpu.PrefetchScalarGridSpec(
        num_scalar_prefetch=3,
        grid=(P // tm,),
        in_specs=[pl.BlockSpec((tm, D), lambda i, te, tr, nt: (jnp.minimum(i, nt[0] - 1), 0)), hbm, hbm, hbm],
        out_specs=pl.BlockSpec((tm, D), lambda i, te, tr, nt: (i, 0)),
        scratch_shapes=[pltpu.VMEM((tm, D), BF16)] + _swiglu_scratch(tm, D, tf),
    )
    kern = functools.partial(_ffn_grouped_kernel, first_expert=first_expert, tf=tf)
    return pl.pallas_call(
        kern,
        grid_spec=grid_spec,
        out_shape=jax.ShapeDtypeStruct((P, D), F32),
        compiler_params=_cparams("arbitrary"),
        name="ffn_grouped",
    )(tile_expert, tile_rows, n_tiles, xs, wg, wu, wd)


def _router_kernel(lg_ref, rt_ref, rtt_ref, cnt_ref, carry_ref):
    tm = lg_ref.shape[0]

    @pl.when(pl.program_id(0) == 0)
    def _():
        carry_ref[...] = jnp.zeros_like(carry_ref)

    lane = lax.broadcasted_iota(jnp.int32, (tm, LANES), 1).astype(F32)
    lg = jnp.where(lane < N_EXPERTS, lg_ref[...], -jnp.inf)
    v1 = jnp.max(lg, axis=1, keepdims=True)
    e1 = jnp.min(jnp.where(lg == v1, lane, float(LANES)), axis=1, keepdims=True)
    lg2 = jnp.where(lane == e1, -jnp.inf, lg)
    v2 = jnp.max(lg2, axis=1, keepdims=True)
    e2 = jnp.min(jnp.where(lg2 == v2, lane, float(LANES)), axis=1, keepdims=True)
    ex = jnp.exp(v2 - v1)
    w1 = 1.0 / (1.0 + ex)
    w2 = ex / (1.0 + ex)

    oh1 = (lane == e1).astype(F32)
    oh2 = (lane == e2).astype(F32)
    r_i = lax.broadcasted_iota(jnp.int32, (tm, tm), 0)
    c_i = lax.broadcasted_iota(jnp.int32, (tm, tm), 1)
    strict = (c_i < r_i).astype(BF16)
    before = _dot(strict, (oh1 + oh2).astype(BF16)) + carry_ref[0:1, :]
    rank1 = jnp.sum(before * oh1, axis=1, keepdims=True)
    rank2 = jnp.sum(before * oh2, axis=1, keepdims=True)
    total = carry_ref[0:1, :] + jnp.sum(oh1 + oh2, axis=0, keepdims=True)
    carry_ref[...] = jnp.broadcast_to(total, carry_ref.shape)
    cnt_ref[...] = jnp.broadcast_to(total, cnt_ref.shape)

    out = jnp.where(lane == 0, e1, 0.0)
    out = jnp.where(lane == 1, e2, out)
    out = jnp.where(lane == 2, w1, out)
    out = jnp.where(lane == 3, w2, out)
    out = jnp.where(lane == 4, rank1, out)
    out = jnp.where(lane == 5, rank2, out)
    rt_ref[...] = out
    rtt_ref[...] = jnp.transpose(out)[:SUBLANES, :]


def _router(logits, tm=1024):
    T = logits.shape[0]
    tm = min(tm, T)
    return pl.pallas_call(
        _router_kernel,
        grid=(T // tm,),
        in_specs=[pl.BlockSpec((tm, LANES), lambda i: (i, 0))],
        out_specs=[
            pl.BlockSpec((tm, LANES), lambda i: (i, 0)),
            pl.BlockSpec((SUBLANES, tm), lambda i: (0, i)),
            pl.BlockSpec((SUBLANES, LANES), lambda i: (0, 0)),
        ],
        out_shape=[
            jax.ShapeDtypeStruct((T, LANES), F32),
            jax.ShapeDtypeStruct((SUBLANES, T), F32),
            jax.ShapeDtypeStruct((SUBLANES, LANES), F32),
        ],
        scratch_shapes=[pltpu.VMEM((SUBLANES, LANES), F32)],
        compiler_params=_cparams("arbitrary"),
        name="router",
    )(logits)


def _dispatch_kernel(p1_ref, p2_ref, pad0_ref, padn_ref, nt_ref, u_ref, xs_hbm, zbuf_ref, sem, zsem,
                     *, pad_bits, tail_per_tile):
    tm = u_ref.shape[0]

    @pl.when(pl.program_id(0) == 0)
    def _():
        zbuf_ref[...] = jnp.zeros_like(zbuf_ref)

        def pad_copies(e, b):
            n = padn_ref[e]
            off = pad0_ref[e] + (n & ((1 << b) - 1))
            if (1 << b) >= SUBLANES:
                off = pl.multiple_of(off, SUBLANES)
                return [pltpu.make_async_copy(zbuf_ref.at[pl.ds(0, 1 << b)],
                                              xs_hbm.at[pl.ds(off, 1 << b)], zsem)]
            return [pltpu.make_async_copy(zbuf_ref.at[pl.ds(0, 1)], xs_hbm.at[pl.ds(off + r, 1)], zsem)
                    for r in range(1 << b)]

        for wait in (False, True):
            for e in range(N_EXPERTS):
                for b in range(pad_bits):
                    @pl.when(((padn_ref[e] >> b) & 1) == 1)
                    def _():
                        for cp in pad_copies(e, b):
                            if wait:
                                cp.wait()
                            else:
                                cp.start()

        zrows = zbuf_ref.shape[0]

        def tail_copy(t):
            off = pl.multiple_of(t * zrows, SUBLANES)
            return pltpu.make_async_copy(zbuf_ref, xs_hbm.at[pl.ds(off, zrows)], zsem)

        def tail_start(t, carry):
            tail_copy(t).start()
            return carry

        def tail_wait(t, carry):
            tail_copy(t).wait()
            return carry

        first, last = nt_ref[0] * tail_per_tile, xs_hbm.shape[0] // zrows
        lax.fori_loop(first, last, tail_start, 0)
        lax.fori_loop(first, last, tail_wait, 0)

    def row_copy(r, pos):
        return pltpu.make_async_copy(u_ref.at[pl.ds(r, 1)], xs_hbm.at[pl.ds(pos, 1)], sem)

    def issue(b, carry):
        for rr in range(ISSUE_UNROLL):
            r = b * ISSUE_UNROLL + rr
            row_copy(r, p1_ref[r]).start()
            row_copy(r, p2_ref[r]).start()
        return carry

    lax.fori_loop(0, tm // ISSUE_UNROLL, issue, 0)
    for _ in range(2):
        pltpu.make_async_copy(u_ref, xs_hbm.at[pl.ds(0, tm)], sem).wait()


def _dispatch(pos1, pos2, pad_start, pad_len, n_tiles, un, P, group_tile, tm=1024):
    T, D = un.shape
    tm = min(tm, T)
    pad_bits = group_tile.bit_length() - 1
    assert group_tile == 1 << pad_bits and pad_bits >= 1
    zrows = group_tile // 2
    kern = functools.partial(_dispatch_kernel, pad_bits=pad_bits, tail_per_tile=group_tile // zrows)
    smem = pl.BlockSpec(memory_space=pltpu.SMEM)
    return pl.pallas_call(
        kern,
        grid=(T // tm,),
        in_specs=[
            pl.BlockSpec((tm,), lambda i: (i,), memory_space=pltpu.SMEM),
            pl.BlockSpec((tm,), lambda i: (i,), memory_space=pltpu.SMEM),
            smem, smem, smem,
            pl.BlockSpec((tm, D), lambda i: (i, 0)),
        ],
        out_specs=pl.BlockSpec(memory_space=pl.ANY),
        out_shape=jax.ShapeDtypeStruct((P, D), un.dtype),
        scratch_shapes=[pltpu.VMEM((zrows, D), un.dtype),
                        pltpu.SemaphoreType.DMA(()), pltpu.SemaphoreType.DMA(())],
        compiler_params=_cparams("arbitrary"),
        name="dispatch",
    )(pos1, pos2, pad_start, pad_len, n_tiles, un)


def _combine_kernel(p1_ref, p2_ref, q1_ref, q2_ref, h_ref, rt_ref, g_ref, y_hbm, o_ref, buf_ref, sem,
                    *, final_norm):
    tm = h_ref.shape[0]
    i = pl.program_id(0)
    n = pl.num_programs(0)
    slot = i % 2

    def issue_tile(i1_ref, i2_ref, s):
        def row_copy(k, r, pos):
            return pltpu.make_async_copy(y_hbm.at[pl.ds(pos, 1)], buf_ref.at[s, k, pl.ds(r, 1)], sem.at[s])

        def issue(b, carry):
            for rr in range(ISSUE_UNROLL):
                r = b * ISSUE_UNROLL + rr
                row_copy(0, r, i1_ref[r]).start()
                row_copy(1, r, i2_ref[r]).start()
            return carry

        lax.fori_loop(0, tm // ISSUE_UNROLL, issue, 0)

    @pl.when(i == 0)
    def _():
        issue_tile(p1_ref, p2_ref, 0)

    @pl.when(i + 1 < n)
    def _():
        issue_tile(q1_ref, q2_ref, 1 - slot)

    for k in range(2):
        pltpu.make_async_copy(y_hbm.at[pl.ds(0, tm)], buf_ref.at[slot, k], sem.at[slot]).wait()

    rt = rt_ref[...]
    out = h_ref[...] + (rt[:, 2:3] * buf_ref[slot, 0] + rt[:, 3:4] * buf_ref[slot, 1])
    if final_norm:
        out = _rms_rows(out) * g_ref[...]
    o_ref[...] = out


def _combine(pos1, pos2, h, routing, g_final, y, final_norm, tm=1024):
    T, D = h.shape
    tm = min(tm, T)
    n = T // tm
    kern = functools.partial(_combine_kernel, final_norm=final_norm)
    cur = pl.BlockSpec((tm,), lambda i: (i,), memory_space=pltpu.SMEM)
    nxt = pl.BlockSpec((tm,), lambda i: (jnp.minimum(i + 1, n - 1),), memory_space=pltpu.SMEM)
    return pl.pallas_call(
        kern,
        grid=(n,),
        in_specs=[
            cur, cur, nxt, nxt,
            pl.BlockSpec((tm, D), lambda i: (i, 0)),
            pl.BlockSpec((tm, LANES), lambda i: (i, 0)),
            pl.BlockSpec((1, D), lambda i: (0, 0)),
            pl.BlockSpec(memory_space=pl.ANY),
        ],
        out_specs=pl.BlockSpec((tm, D), lambda i: (i, 0)),
        out_shape=jax.ShapeDtypeStruct((T, D), F32),
        scratch_shapes=[pltpu.VMEM((2, 2, tm, D), F32), pltpu.SemaphoreType.DMA((2,))],
        compiler_params=_cparams("arbitrary"),
        name="combine",
    )(pos1, pos2, pos1, pos2, h, routing, g_final, y)


def _final_norm_kernel(h_ref, g_ref, o_ref):
    o_ref[...] = _rms_rows(h_ref[...]) * g_ref[...]


def _final_norm(h, g, tm=1024):
    T, D = h.shape
    tm = min(tm, T)
    return pl.pallas_call(
        _final_norm_kernel,
        grid=(T // tm,),
        in_specs=[pl.BlockSpec((tm, D), lambda i: (i, 0)), pl.BlockSpec((1, D), lambda i: (0, 0))],
        out_specs=pl.BlockSpec((tm, D), lambda i: (i, 0)),
        out_shape=jax.ShapeDtypeStruct((T, D), F32),
        compiler_params=_cparams("parallel"),
        name="final_norm",
    )(h, g)


def _block_diag(w):
    G, n, _ = w.shape
    eye = jnp.eye(G, dtype=w.dtype)
    return (eye[:, None, :, None] * w[:, :, None, :]).reshape(G * n, G * n)


def _moe_layer(h, un, logits, w_gate, w_up, w_down, first_expert, g_final, final_norm, tm):
    T, D = h.shape
    E = N_EXPERTS
    routing, routing_t, counts = _router(logits)
    e1 = routing_t[0].astype(jnp.int32)
    e2 = routing_t[1].astype(jnp.int32)
    rank1 = routing_t[4].astype(jnp.int32)
    rank2 = routing_t[5].astype(jnp.int32)
    cnt = counts[0, :E].astype(jnp.int32)
    tiles_per = (cnt + tm - 1) // tm
    tile_end = jnp.cumsum(tiles_per)
    offs = (tile_end - tiles_per) * tm
    pos1 = offs[e1] + rank1
    pos2 = offs[e2] + rank2
    n_slots = (2 * T) // tm + E
    P = n_slots * tm
    tile_expert = jnp.minimum(
        jnp.sum(jnp.arange(n_slots, dtype=jnp.int32)[:, None] >= tile_end[None, :], axis=1), E - 1
    ).astype(jnp.int32)
    n_tiles = tile_end[E - 1:E].astype(jnp.int32)
    tile_in_group = jnp.arange(n_slots, dtype=jnp.int32) - (tile_end - tiles_per)[tile_expert]
    tile_rows = jnp.clip(cnt[tile_expert] - tile_in_group * tm, 0, tm).astype(jnp.int32)

    xs = _dispatch(pos1, pos2, offs + cnt, tiles_per * tm - cnt, n_tiles, un, P, tm)
    y = _ffn_grouped(tile_expert, tile_rows, n_tiles, xs, w_gate, w_up, w_down, first_expert, tm)
    return _combine(pos1, pos2, h, routing, g_final, y, final_norm)


def _forward(x, norm_mix_g, w_in, ml_b_if, ml_norm_g, rg_conv_w, rg_conv_b, rg_w_a, rg_b_a,
             rg_w_x, rg_b_x, rg_lam, rg_norm_g, w_out, norm_ffn_g, ffn_w_gate, ffn_w_up,
             ffn_w_down, moe_w_router, moe_w_gate, moe_w_up, moe_w_down, norm_final_g,
             moe_tile=2048):
    B, S, D = x.shape
    T = B * S
    depth = w_in.shape[0]
    ml_w = ML_HEADS * ML_HEAD_DIM
    n_q = 4 * ml_w
    n_if = 2 * ML_HEADS
    rg_w = rg_lam.shape[1]
    n_r = 2 * rg_w

    h = x.reshape(T, D)
    g_final = norm_final_g.reshape(1, D)
    moe_gate_all = moe_w_gate.reshape((-1,) + moe_w_gate.shape[2:])
    moe_up_all = moe_w_up.reshape((-1,) + moe_w_up.shape[2:])
    moe_down_all = moe_w_down.reshape((-1,) + moe_w_down.shape[2:])
    for l in range(depth):
        wl = w_in[l]
        w_all = jnp.concatenate(
            [wl[:, :ml_w], wl[:, 2 * ml_w:n_q], wl[:, n_q + n_if:], wl[:, n_q:n_q + n_if],
             jnp.zeros((D, LANES - n_if), wl.dtype)], axis=1).astype(BF16)
        wt = jnp.concatenate([wl[:, ml_w:2 * ml_w], wl[:, n_q:n_q + n_if]], axis=1).T.astype(BF16)
        b_if = jnp.concatenate([ml_b_if[l], jnp.zeros((LANES - n_if,), F32)]).reshape(1, LANES)
        zq, kt, zr, zg, zgt = _inproj(h, norm_mix_g[l].reshape(1, D), b_if, ml_b_if[l].reshape(n_if, 1),
                                      w_all, wt, 3 * ml_w, n_r)
        yml = _mlstm(zq, kt, zg, zgt, ml_norm_g[l].reshape(1, ml_w), B, S)

        w_gates = jnp.concatenate([_block_diag(rg_w_a[l]), _block_diag(rg_w_x[l])], axis=1).astype(BF16)
        b_gates = jnp.concatenate([rg_b_a[l], rg_b_x[l]]).reshape(1, n_r)
        yrg = _rglru(zr, rg_conv_w[l], rg_conv_b[l].reshape(1, rg_w), w_gates, b_gates,
                     rg_lam[l].reshape(1, rg_w), rg_norm_g[l].reshape(1, rg_w), B, S)

        j = l // 2
        is_moe = (l % 2 == 1)
        wo = w_out[l].astype(BF16)
        g_ffn = norm_ffn_g[l].reshape(1, D)
        last = (l == depth - 1)
        if is_moe:
            w_router = jnp.concatenate(
                [moe_w_router[j], jnp.zeros((D, LANES - N_EXPERTS), F32)], axis=1)
            h, un, logits = _outproj(yml, yrg, h, wo, g_ffn, w_router)
            h = _moe_layer(h, un, logits, moe_gate_all, moe_up_all, moe_down_all, j * N_EXPERTS,
                           g_final, last, moe_tile)
        else:
            h = _ffn_dense(yml, yrg, h, wo, g_ffn, ffn_w_gate, ffn_w_up, ffn_w_down, j)
            if last:
                h = _final_norm(h, g_final)
    return h.reshape(B, S, D)


def kernel(x, norm_mix_g, w_in, ml_b_if, ml_norm_g, rg_conv_w, rg_conv_b, rg_w_a, rg_b_a, rg_w_x,
           rg_b_x, rg_lam, rg_norm_g, w_out, norm_ffn_g, ffn_w_gate, ffn_w_up, ffn_w_down,
           moe_w_router, moe_w_gate, moe_w_up, moe_w_down, norm_final_g):
    return _forward(x, norm_mix_g, w_in, ml_b_if, ml_norm_g, rg_conv_w, rg_conv_b, rg_w_a, rg_b_a,
                    rg_w_x, rg_b_x, rg_lam, rg_norm_g, w_out, norm_ffn_g, ffn_w_gate, ffn_w_up,
                    ffn_w_down, moe_w_router, moe_w_gate, moe_w_up, moe_w_down, norm_final_g)
```

```python
import functools

import jax
import jax.numpy as jnp
from jax import lax
from jax.experimental import pallas as pl
from jax.experimental.pallas import tpu as pltpu

EPS = 1e-6
ML_HEADS = 4
ML_HEAD_DIM = 128
CHUNK = 128
RG_BLOCKS = 8
RG_C = 8.0
CONV_WIDTH = 4
N_EXPERTS = 8
LANES = 128
SUBLANES = 8
VMEM_LIMIT = 60 * 1024 * 1024
ISSUE_UNROLL = 8

BF16 = jnp.bfloat16
F32 = jnp.float32


def _cparams(*sem):
    return pltpu.CompilerParams(dimension_semantics=sem, vmem_limit_bytes=VMEM_LIMIT)


def _dot(a, b):
    return jnp.dot(a, b, preferred_element_type=F32)


def _dot_nt(a, b):
    return lax.dot_general(a, b, (((1,), (1,)), ((), ())), preferred_element_type=F32)


def _dot_tn(a, b):
    return lax.dot_general(a, b, (((0,), (0,)), ((), ())), preferred_element_type=F32)


def _dot_f32(a, b):
    return jnp.dot(a, b, preferred_element_type=F32, precision=lax.Precision.HIGHEST)


def _sigmoid(x):
    return 1.0 / (1.0 + jnp.exp(-x))


def _log_sigmoid(x):
    return jnp.minimum(x, 0.0) - jnp.log(1.0 + jnp.exp(-jnp.abs(x)))


def _softplus(x):
    return jnp.maximum(x, 0.0) + jnp.log(1.0 + jnp.exp(-jnp.abs(x)))


def _gelu_tanh(x):
    return 0.5 * x * (1.0 + jnp.tanh(0.7978845608028654 * (x + 0.044715 * (x * x * x))))


def _rms_rows(x):
    return x * lax.rsqrt(jnp.mean(x * x, axis=-1, keepdims=True) + EPS)


def _inproj_kernel(h_ref, g_ref, b_ref, bt_ref, w_ref, wt_ref, zq_ref, kt_ref, zr_ref, zg_ref, zgt_ref,
                   *, n_q, n_r, n_col):
    xn = (_rms_rows(h_ref[...]) * g_ref[...]).astype(BF16)
    for c0 in range(0, n_q, n_col):
        zq_ref[:, c0:c0 + n_col] = _dot(xn, w_ref[:, c0:c0 + n_col]).astype(BF16)
    for c0 in range(0, n_r, n_col):
        zr_ref[:, c0:c0 + n_col] = _dot(xn, w_ref[:, n_q + c0:n_q + c0 + n_col]).astype(BF16)
    zg_ref[...] = _dot(xn, w_ref[:, n_q + n_r:]) + b_ref[...]
    n_k = kt_ref.shape[1]
    t = _dot_nt(wt_ref[...], xn)
    for c in range(kt_ref.shape[0]):
        kt_ref[c] = t[:n_k, c * CHUNK:(c + 1) * CHUNK].astype(BF16)
        zgt_ref[c] = t[n_k:, c * CHUNK:(c + 1) * CHUNK] + bt_ref[...]


def _inproj(h, g, b_if, b_if_t, w_all, wt, n_q, n_r, tm=1024):
    T, D = h.shape
    tm = min(tm, T)
    n_all = w_all.shape[1]
    n_g = n_all - n_q - n_r
    n_k = wt.shape[0] - SUBLANES
    kern = functools.partial(_inproj_kernel, n_q=n_q, n_r=n_r, n_col=512)
    const = lambda i: (0, 0)
    return pl.pallas_call(
        kern,
        grid=(T // tm,),
        in_specs=[
            pl.BlockSpec((tm, D), lambda i: (i, 0)),
            pl.BlockSpec((1, D), const),
            pl.BlockSpec((1, n_g), const),
            pl.BlockSpec((SUBLANES, 1), const),
            pl.BlockSpec((D, n_all), const),
            pl.BlockSpec((n_k + SUBLANES, D), const),
        ],
        out_specs=[
            pl.BlockSpec((tm, n_q), lambda i: (i, 0)),
            pl.BlockSpec((tm // CHUNK, n_k, CHUNK), lambda i: (i, 0, 0)),
            pl.BlockSpec((tm, n_r), lambda i: (i, 0)),
            pl.BlockSpec((tm, n_g), lambda i: (i, 0)),
            pl.BlockSpec((tm // CHUNK, SUBLANES, CHUNK), lambda i: (i, 0, 0)),
        ],
        out_shape=[
            jax.ShapeDtypeStruct((T, n_q), BF16),
            jax.ShapeDtypeStruct((T // CHUNK, n_k, CHUNK), BF16),
            jax.ShapeDtypeStruct((T, n_r), BF16),
            jax.ShapeDtypeStruct((T, n_g), F32),
            jax.ShapeDtypeStruct((T // CHUNK, SUBLANES, CHUNK), F32),
        ],
        compiler_params=_cparams("parallel"),
        name="inproj",
    )(h, g, b_if, b_if_t, w_all, wt)


def _mlstm_kernel(zq_ref, kt_ref, zg_ref, zgt_ref, gain_ref, y_ref, c_ref, m_ref, pcol_ref, prow_ref,
                  *, n_chunks):
    H, Dh, L = ML_HEADS, ML_HEAD_DIM, CHUNK
    W = H * Dh
    scale = Dh ** -0.5

    @pl.when(pl.program_id(1) == 0)
    def _():
        c_ref[...] = jnp.zeros_like(c_ref)
        m_ref[...] = jnp.zeros_like(m_ref)

    row = lax.broadcasted_iota(jnp.int32, (L, L), 0)
    col = lax.broadcasted_iota(jnp.int32, (L, L), 1)
    causal = col <= row
    tri_l = causal.astype(F32)
    tri_u = (row <= col).astype(F32)
    ones_blk = jnp.ones((L, Dh), BF16)

    def gate_prefix(c, slot):
        r0 = pl.multiple_of(c * L, L)
        gcol = zg_ref[pl.ds(r0, L), :]
        grow = zgt_ref[c]
        bcol_all = _dot_f32(tri_l, _log_sigmoid(gcol))
        brow_all = _dot_f32(_log_sigmoid(grow), tri_u)
        cm = gcol - pltpu.roll(bcol_all, LANES - H, 1)
        d = 1
        while d < L:
            cm = jnp.maximum(cm, jnp.where(row >= d, pltpu.roll(cm, d, 0), -jnp.inf))
            d *= 2
        pcol_ref[slot, 0] = bcol_all
        pcol_ref[slot, 1] = cm
        prow_ref[slot] = brow_all

    gate_prefix(0, 0)

    def chunk_body(c, carry):
        r0 = pl.multiple_of(c * L, L)
        slot = c % 2
        grow = zgt_ref[c]
        bcol_all = pcol_ref[slot, 0]
        cm = pcol_ref[slot, 1]
        brow_all = prow_ref[slot]

        for hd in range(H):
            q = zq_ref[pl.ds(r0, L), hd * Dh:(hd + 1) * Dh]
            v = zq_ref[pl.ds(r0, L), W + hd * Dh:W + (hd + 1) * Dh]
            o = zq_ref[pl.ds(r0, L), 2 * W + hd * Dh:2 * W + (hd + 1) * Dh]
            kt = kt_ref[c, hd * Dh:(hd + 1) * Dh, :]
            v_aug = jnp.concatenate([v, ones_blk], axis=1)

            li_row = grow[hd:hd + 1, :]
            b_row = brow_all[H + hd:H + hd + 1, :]
            g_tot = b_row[:, L - 1:L]
            r_row = li_row - b_row

            c_prev = c_ref[hd]
            m_prev = m_ref[hd]

            mx = jnp.maximum(jnp.broadcast_to(cm[:, hd:hd + 1], (L, L)), m_prev)
            bb = jnp.broadcast_to(bcol_all[:, H + hd:H + hd + 1], (L, L))
            p = jnp.exp(jnp.where(causal, r_row - mx, -jnp.inf))
            s_w = p * (_dot(q, kt) * scale)
            inter = jnp.exp(m_prev - mx)
            intra = _dot(s_w.astype(BF16), v_aug)
            cross = _dot(q, c_prev.astype(BF16))
            num = intra[:, :Dh] + inter * cross[:, :Dh]
            den = intra[:, Dh:] + inter * cross[:, Dh:]
            hh = num / jnp.maximum(jnp.abs(den), jnp.exp(-(bb + mx)))
            ht = _rms_rows(hh) * gain_ref[:, hd * Dh:(hd + 1) * Dh]
            y_ref[pl.ds(r0, L), hd * Dh:(hd + 1) * Dh] = (_sigmoid(o.astype(F32)) * ht).astype(BF16)

            a = g_tot + r_row
            m_loc = jnp.max(a, axis=1, keepdims=True)
            w = jnp.exp(a - m_loc) * scale
            c_loc = _dot((kt.astype(F32) * w).astype(BF16), v_aug)
            m_new = jnp.maximum(g_tot + m_prev, m_loc)
            s_old = jnp.exp(g_tot + m_prev - m_new)
            s_loc = jnp.exp(m_loc - m_new)
            c_ref[hd] = s_old * c_prev + s_loc * c_loc
            m_ref[hd] = m_new
        gate_prefix(jnp.minimum(c + 1, n_chunks - 1), 1 - slot)
        return carry

    lax.fori_loop(0, n_chunks, chunk_body, 0)


def _mlstm(zq, kt, zg, zgt, gain, B, S, rows=2048):
    T = B * S
    rows = min(rows, S)
    n_chunks = rows // CHUNK
    steps = S // rows
    W = ML_HEADS * ML_HEAD_DIM
    kern = functools.partial(_mlstm_kernel, n_chunks=n_chunks)
    return pl.pallas_call(
        kern,
        grid=(B, steps),
        in_specs=[
            pl.BlockSpec((rows, 3 * W), lambda b, s: (b * steps + s, 0)),
            pl.BlockSpec((n_chunks, W, CHUNK), lambda b, s: (b * steps + s, 0, 0)),
            pl.BlockSpec((rows, LANES), lambda b, s: (b * steps + s, 0)),
            pl.BlockSpec((n_chunks, SUBLANES, CHUNK), lambda b, s: (b * steps + s, 0, 0)),
            pl.BlockSpec((1, W), lambda b, s: (0, 0)),
        ],
        out_specs=pl.BlockSpec((rows, W), lambda b, s: (b * steps + s, 0)),
        out_shape=jax.ShapeDtypeStruct((T, W), BF16),
        scratch_shapes=[
            pltpu.VMEM((ML_HEADS, ML_HEAD_DIM, 2 * ML_HEAD_DIM), F32),
            pltpu.VMEM((ML_HEADS, 1, 1), F32),
            pltpu.VMEM((2, 2, CHUNK, LANES), F32),
            pltpu.VMEM((2, SUBLANES, CHUNK), F32),
        ],
        compiler_params=_cparams("parallel", "arbitrary"),
        name="mlstm",
    )(zq, kt, zg, zgt, gain)


def _rglru_kernel(zr_ref, cw_ref, cb_ref, wg_ref, bg_ref, lam_ref, gain_ref, y_ref,
                  nat_ref, ost_ref, a_ref, u_ref, p_ref, cx_ref, hc_ref, *, rows):
    Wd = y_ref.shape[1]
    NB = SUBLANES
    BL = rows // NB
    PITCH = BL + SUBLANES
    n_slab = Wd // LANES
    taps = CONV_WIDTH - 1

    @pl.when(pl.program_id(1) == 0)
    def _():
        cx_ref[...] = jnp.zeros_like(cx_ref)
        hc_ref[...] = jnp.zeros_like(hc_ref)

    for s in range(NB):
        blk = zr_ref[s * BL:(s + 1) * BL, :].astype(F32)
        for c in range(2 * n_slab):
            nat_ref[c, s * PITCH:s * PITCH + BL, :] = blk[:, c * LANES:(c + 1) * LANES]

    def interleaved(c):
        return jnp.concatenate([nat_ref[c, pl.ds(i, NB, stride=PITCH), :] for i in range(BL)], axis=0)

    first_sub = lax.broadcasted_iota(jnp.int32, (NB, LANES), 0) == 0
    xcs = []
    for c in range(n_slab):
        lanes = slice(c * LANES, (c + 1) * LANES)
        x = interleaved(c)
        head = []
        for k in range(taps, 0, -1):
            tail = x[(BL - k) * NB:(BL - k + 1) * NB, :]
            prev = cx_ref[c * taps + k - 1]
            head.append(jnp.where(first_sub, pltpu.roll(prev, 1, 0), pltpu.roll(tail, 1, 0)))
            cx_ref[c * taps + k - 1] = tail
        ext = jnp.concatenate(head + [x], axis=0)
        xc = cb_ref[:, lanes] + jnp.zeros((rows, LANES), F32)
        for j in range(CONV_WIDTH):
            xc = xc + cw_ref[j:j + 1, lanes] * ext[j * NB:j * NB + rows, :]
        xcs.append(xc)
    xc = jnp.concatenate(xcs, axis=1)

    gates = _dot(xc.astype(BF16), wg_ref[...]) + bg_ref[...]
    r = _sigmoid(gates[:, :Wd])
    ig = _sigmoid(gates[:, Wd:])
    a = jnp.exp(r * ((-RG_C) * _softplus(-lam_ref[...])))
    a_ref[...] = a
    d = 1.0 - a * a
    u_ref[...] = jnp.where(d > 0.0, d * lax.rsqrt(d), 0.0) * (ig * xc)

    for c in range(n_slab):
        lanes = slice(c * LANES, (c + 1) * LANES)
        h = jnp.zeros((NB, LANES), F32)
        p = jnp.ones((NB, LANES), F32)
        for i in range(BL):
            ai = a_ref[i * NB:(i + 1) * NB, lanes]
            h = ai * h + u_ref[i * NB:(i + 1) * NB, lanes]
            p = ai * p
            u_ref[i * NB:(i + 1) * NB, lanes] = h
            p_ref[i * NB:(i + 1) * NB, lanes] = p
    h_end = u_ref[(BL - 1) * NB:BL * NB, :]
    p_end = p_ref[(BL - 1) * NB:BL * NB, :]
    h_in = [hc_ref[0:1, :]]
    for s in range(NB):
        h_in.append(h_end[s:s + 1, :] + p_end[s:s + 1, :] * h_in[s])
    hc_ref[...] = jnp.broadcast_to(h_in[NB], hc_ref.shape)
    h_enter = jnp.concatenate(h_in[:NB], axis=0)
    h_all = u_ref[...] + p_ref[...] * jnp.concatenate([h_enter] * BL, axis=0)

    gate = jnp.concatenate([interleaved(n_slab + c) for c in range(n_slab)], axis=1)
    yv = h_all * _gelu_tanh(gate)
    yn = _rms_rows(yv) * gain_ref[...]
    for c in range(n_slab):
        for i in range(BL):
            ost_ref[c, pl.ds(i, NB, stride=PITCH), :] = yn[i * NB:(i + 1) * NB, c * LANES:(c + 1) * LANES]
    for s in range(NB):
        y_ref[s * BL:(s + 1) * BL, :] = jnp.concatenate(
            [ost_ref[c, s * PITCH:s * PITCH + BL, :] for c in range(n_slab)], axis=1).astype(BF16)


def _rglru(zr, conv_w, conv_b, w_gates, b_gates, lam, gain, B, S, rows=1024):
    T = B * S
    Wd = zr.shape[1] // 2
    rows = min(rows, S)
    steps = S // rows
    pitch_rows = SUBLANES * (rows // SUBLANES + SUBLANES)
    n_slab = Wd // LANES
    kern = functools.partial(_rglru_kernel, rows=rows)
    const = lambda b, s: (0, 0)
    return pl.pallas_call(
        kern,
        grid=(B, steps),
        in_specs=[
            pl.BlockSpec((rows, 2 * Wd), lambda b, s: (b * steps + s, 0)),
            pl.BlockSpec((CONV_WIDTH, Wd), const),
            pl.BlockSpec((1, Wd), const),
            pl.BlockSpec((Wd, 2 * Wd), const),
            pl.BlockSpec((1, 2 * Wd), const),
            pl.BlockSpec((1, Wd), const),
            pl.BlockSpec((1, Wd), const),
        ],
        out_specs=pl.BlockSpec((rows, Wd), lambda b, s: (b * steps + s, 0)),
        out_shape=jax.ShapeDtypeStruct((T, Wd), BF16),
        scratch_shapes=[
            pltpu.VMEM((2 * n_slab, pitch_rows, LANES), F32),
            pltpu.VMEM((n_slab, pitch_rows, LANES), F32),
            pltpu.VMEM((rows, Wd), F32),
            pltpu.VMEM((rows, Wd), F32),
            pltpu.VMEM((rows, Wd), F32),
            pltpu.VMEM((n_slab * (CONV_WIDTH - 1), SUBLANES, LANES), F32),
            pltpu.VMEM((SUBLANES, Wd), F32),
        ],
        compiler_params=_cparams("parallel", "arbitrary"),
        name="rglru",
    )(zr, conv_w, conv_b, w_gates, b_gates, lam, gain)


def _outproj_kernel(yml_ref, yrg_ref, h_ref, w_ref, g_ref, wr_ref, hn_ref, un_ref, lg_ref):
    half = yml_ref.shape[1]
    hn = (h_ref[...] + _dot(yml_ref[...], w_ref[:half, :]) + _dot(yrg_ref[...], w_ref[half:, :]))
    un = _rms_rows(hn) * g_ref[...]
    wr = wr_ref[...]
    w_hi = wr.astype(BF16)
    w_lo = (wr - w_hi.astype(F32)).astype(BF16)
    u_hi = un.astype(BF16)
    u_lo = (un - u_hi.astype(F32)).astype(BF16)
    both = _dot(u_hi, jnp.concatenate([w_hi, w_lo], axis=1))
    lg_ref[...] = both[:, :LANES] + (both[:, LANES:] + _dot(u_lo, w_hi))
    hn_ref[...] = hn
    un_ref[...] = un


def _outproj(yml, yrg, h, w_out, g_ffn, w_router, tm=1024):
    T, D = h.shape
    half = yml.shape[1]
    tm = min(tm, T)
    row = lambda i: (i, 0)
    const = lambda i: (0, 0)
    return pl.pallas_call(
        _outproj_kernel,
        grid=(T // tm,),
        in_specs=[
            pl.BlockSpec((tm, half), row),
            pl.BlockSpec((tm, half), row),
            pl.BlockSpec((tm, D), row),
            pl.BlockSpec((2 * half, D), const),
            pl.BlockSpec((1, D), const),
            pl.BlockSpec((D, LANES), const),
        ],
        out_specs=[pl.BlockSpec((tm, D), row), pl.BlockSpec((tm, D), row), pl.BlockSpec((tm, LANES), row)],
        out_shape=[jax.ShapeDtypeStruct((T, D), F32), jax.ShapeDtypeStruct((T, D), F32),
                   jax.ShapeDtypeStruct((T, LANES), F32)],
        compiler_params=_cparams("parallel"),
        name="outproj_router",
    )(yml, yrg, h, w_out, g_ffn, w_router)


SUB_ROWS = 1024
SUB_STEP = 256


def _swiglu_tile(x_ref, o_ref, init_fn, valid_fn, w_hbm, e_cur, e_next, first, has_next, wbufs, sems,
                 scratch, tf):
    wg_hbm, wu_hbm, wd_hbm = w_hbm
    gbuf, ubuf, dbuf = wbufs
    act_ref, wgb_ref, wub_ref, wdb_ref = scratch
    tm, D = o_ref.shape
    sub = min(SUB_ROWS, tm)
    nj = wg_hbm.shape[2] // tf

    def up_copies(e, j, slot):
        cols = pl.ds(pl.multiple_of(j * tf, LANES), tf)
        return (pltpu.make_async_copy(wg_hbm.at[e, :, cols], gbuf.at[slot], sems.at[0, slot]),
                pltpu.make_async_copy(wu_hbm.at[e, :, cols], ubuf.at[slot], sems.at[1, slot]))

    def down_copy(e, j, slot):
        rws = pl.ds(pl.multiple_of(j * tf, SUBLANES), tf)
        return pltpu.make_async_copy(wd_hbm.at[e, rws, :], dbuf.at[slot], sems.at[2, slot])

    def start_up(e, j, slot):
        for cp in up_copies(e, j, slot):
            cp.start()

    def wait_up(e, j, slot):
        for cp in up_copies(e, j, slot):
            cp.wait()

    def up(rows, s, slot):
        if s == 0:
            wgb_ref[...] = gbuf[slot].astype(BF16)
            wub_ref[...] = ubuf[slot].astype(BF16)
        x = x_ref[rows, :]
        g = _dot(x, wgb_ref[...])
        u = _dot(x, wub_ref[...])
        act_ref[rows, :] = (g * _sigmoid(g) * u).astype(BF16)

    def down(rows, s, slot):
        if s == 0:
            wdb_ref[...] = dbuf[slot].astype(BF16)
        o_ref[rows, :] += _dot(act_ref[rows, :], wdb_ref[...])

    def sub_tiles(body):
        for s in range(tm // sub):
            valid = valid_fn(s)
            if valid is None:
                body(pl.ds(s * sub, sub), s, sub)
                continue
            for z in list(range(SUB_STEP, sub, SUB_STEP)) + [sub]:
                on = valid > z - SUB_STEP
                if z < sub:
                    on = jnp.logical_and(on, valid <= z)
                pl.when(on)(functools.partial(body, pl.ds(s * sub, z), s, z))

    @pl.when(first)
    def _():
        start_up(e_cur, 0, 0)

    wait_up(e_cur, 0, 0)
    if nj > 1:
        start_up(e_cur, 1, 1)
    down_copy(e_cur, 0, 0).start()

    def first_body(rows, s, z):
        up(rows, s, 0)
        if init_fn is not None:
            o_ref[rows, :] = init_fn(rows)

    sub_tiles(first_body)

    def step(j, carry):
        slot = j % 2
        wait_up(e_cur, j, slot)
        down_copy(e_cur, j - 1, 1 - slot).wait()

        @pl.when(j + 1 < nj)
        def _():
            start_up(e_cur, j + 1, 1 - slot)

        down_copy(e_cur, j, slot).start()

        def body(rows, s, z):
            down(rows, s, 1 - slot)
            up(rows, s, slot)

        sub_tiles(body)
        return carry

    lax.fori_loop(1, nj, step, 0)

    last_slot = (nj - 1) % 2
    down_copy(e_cur, nj - 1, last_slot).wait()

    @pl.when(has_next)
    def _():
        start_up(e_next, 0, 0)

    def last_body(rows, s, z):
        down(rows, s, last_slot)
        if z < sub:
            o_ref[pl.ds(s * sub + z, sub - z), :] = jnp.zeros((sub - z, D), o_ref.dtype)

    sub_tiles(last_body)
    for s in range(tm // sub):
        valid = valid_fn(s)
        if valid is not None:
            @pl.when(valid <= 0)
            def _():
                o_ref[pl.ds(s * sub, sub), :] = jnp.zeros((sub, D), o_ref.dtype)


def _swiglu_scratch(tm, D, tf):
    return [pltpu.VMEM((2, D, tf), F32), pltpu.VMEM((2, D, tf), F32), pltpu.VMEM((2, tf, D), F32),
            pltpu.SemaphoreType.DMA((3, 2)),
            pltpu.VMEM((tm, tf), BF16), pltpu.VMEM((D, tf), BF16), pltpu.VMEM((D, tf), BF16),
            pltpu.VMEM((tf, D), BF16)]


def _ffn_dense_kernel(yml_ref, yrg_ref, h_ref, wo_ref, g_ref, wg_hbm, wu_hbm, wd_hbm, o_ref, un_ref,
                      gbuf, ubuf, dbuf, sems, *scratch, layer, tf):
    i = pl.program_id(0)
    tm = o_ref.shape[0]
    sub = min(SUB_ROWS, tm)
    half = yml_ref.shape[1]

    for s in range(tm // sub):
        rows = pl.ds(s * sub, sub)
        hn = (h_ref[rows, :] + _dot(yml_ref[rows, :], wo_ref[:half, :])
              + _dot(yrg_ref[rows, :], wo_ref[half:, :]))
        o_ref[rows, :] = hn
        un_ref[rows, :] = (_rms_rows(hn) * g_ref[...]).astype(BF16)

    _swiglu_tile(un_ref, o_ref, None, lambda s: None, (wg_hbm, wu_hbm, wd_hbm), layer, layer,
                 i == 0, i + 1 < pl.num_programs(0), (gbuf, ubuf, dbuf), sems, scratch, tf)


def _ffn_dense(yml, yrg, h, w_out, g_ffn, wg, wu, wd, layer, tm=1024, tf=256):
    T, D = h.shape
    half = yml.shape[1]
    F = wg.shape[2]
    tm = min(tm, T)
    tf = min(tf, F)
    row = lambda i: (i, 0)
    const = lambda i: (0, 0)
    hbm = pl.BlockSpec(memory_space=pl.ANY)
    kern = functools.partial(_ffn_dense_kernel, layer=layer, tf=tf)
    return pl.pallas_call(
        kern,
        grid=(T // tm,),
        in_specs=[pl.BlockSpec((tm, half), row), pl.BlockSpec((tm, half), row), pl.BlockSpec((tm, D), row),
                  pl.BlockSpec((2 * half, D), const), pl.BlockSpec((1, D), const), hbm, hbm, hbm],
        out_specs=pl.BlockSpec((tm, D), row),
        out_shape=jax.ShapeDtypeStruct((T, D), F32),
        scratch_shapes=[pltpu.VMEM((tm, D), BF16)] + _swiglu_scratch(tm, D, tf),
        compiler_params=_cparams("arbitrary"),
        name="ffn_dense",
    )(yml, yrg, h, w_out, g_ffn, wg, wu, wd)


def _ffn_grouped_kernel(te_ref, tr_ref, nt_ref, x_ref, wg_hbm, wu_hbm, wd_hbm, o_ref, xb_ref,
                        gbuf, ubuf, dbuf, sems, *scratch, first_expert, tf):
    i = pl.program_id(0)
    sub = min(SUB_ROWS, o_ref.shape[0])
    n_slots = pl.num_programs(0)

    @pl.when(i < nt_ref[0])
    def _():
        xb_ref[...] = x_ref[...].astype(BF16)
        e_next = first_expert + te_ref[jnp.minimum(i + 1, n_slots - 1)]
        _swiglu_tile(xb_ref, o_ref, lambda rows: jnp.zeros((rows.size, o_ref.shape[1]), F32),
                     lambda s: tr_ref[i] - s * sub, (wg_hbm, wu_hbm, wd_hbm), first_expert + te_ref[i],
                     e_next, i == 0, i + 1 < nt_ref[0], (gbuf, ubuf, dbuf), sems, scratch, tf)

    @pl.when(i >= nt_ref[0])
    def _():
        o_ref[...] = jnp.zeros_like(o_ref)


def _ffn_grouped(tile_expert, tile_rows, n_tiles, xs, wg, wu, wd, first_expert, tm, tf=512):
    P, D = xs.shape
    F = wg.shape[2]
    tf = min(tf, F)
    hbm = pl.BlockSpec(memory_space=pl.ANY)
    grid_spec = pltpu.PrefetchScalarGridSpec(
        num_scalar_prefetch=3,
        grid=(P // tm,),
        in_specs=[pl.BlockSpec((tm, D), lambda i, te, tr, nt: (jnp.minimum(i, nt[0] - 1), 0)), hbm, hbm, hbm],
        out_specs=pl.BlockSpec((tm, D), lambda i, te, tr, nt: (i, 0)),
        scratch_shapes=[pltpu.VMEM((tm, D), BF16)] + _swiglu_scratch(tm, D, tf),
    )
    kern = functools.partial(_ffn_grouped_kernel, first_expert=first_expert, tf=tf)
    return pl.pallas_call(
        kern,
        grid_spec=grid_spec,
        out_shape=jax.ShapeDtypeStruct((P, D), F32),
        compiler_params=_cparams("arbitrary"),
        name="ffn_grouped",
    )(tile_expert, tile_rows, n_tiles, xs, wg, wu, wd)


def _router_kernel(lg_ref, rt_ref, rtt_ref, cnt_ref, carry_ref):
    tm = lg_ref.shape[0]

    @pl.when(pl.program_id(0) == 0)
    def _():
        carry_ref[...] = jnp.zeros_like(carry_ref)

    lane = lax.broadcasted_iota(jnp.int32, (tm, LANES), 1).astype(F32)
    lg = jnp.where(lane < N_EXPERTS, lg_ref[...], -jnp.inf)
    v1 = jnp.max(lg, axis=1, keepdims=True)
    e1 = jnp.min(jnp.where(lg == v1, lane, float(LANES)), axis=1, keepdims=True)
    lg2 = jnp.where(lane == e1, -jnp.inf, lg)
    v2 = jnp.max(lg2, axis=1, keepdims=True)
    e2 = jnp.min(jnp.where(lg2 == v2, lane, float(LANES)), axis=1, keepdims=True)
    ex = jnp.exp(v2 - v1)
    w1 = 1.0 / (1.0 + ex)
    w2 = ex / (1.0 + ex)

    oh1 = (lane == e1).astype(F32)
    oh2 = (lane == e2).astype(F32)
    r_i = lax.broadcasted_iota(jnp.int32, (tm, tm), 0)
    c_i = lax.broadcasted_iota(jnp.int32, (tm, tm), 1)
    strict = (c_i < r_i).astype(BF16)
    before = _dot(strict, (oh1 + oh2).astype(BF16)) + carry_ref[0:1, :]
    rank1 = jnp.sum(before * oh1, axis=1, keepdims=True)
    rank2 = jnp.sum(before * oh2, axis=1, keepdims=True)
    total = carry_ref[0:1, :] + jnp.sum(oh1 + oh2, axis=0, keepdims=True)
    carry_ref[...] = jnp.broadcast_to(total, carry_ref.shape)
    cnt_ref[...] = jnp.broadcast_to(total, cnt_ref.shape)

    out = jnp.where(lane == 0, e1, 0.0)
    out = jnp.where(lane == 1, e2, out)
    out = jnp.where(lane == 2, w1, out)
    out = jnp.where(lane == 3, w2, out)
    out = jnp.where(lane == 4, rank1, out)
    out = jnp.where(lane == 5, rank2, out)
    rt_ref[...] = out
    rtt_ref[...] = jnp.transpose(out)[:SUBLANES, :]


def _router(logits, tm=1024):
    T = logits.shape[0]
    tm = min(tm, T)
    return pl.pallas_call(
        _router_kernel,
        grid=(T // tm,),
        in_specs=[pl.BlockSpec((tm, LANES), lambda i: (i, 0))],
        out_specs=[
            pl.BlockSpec((tm, LANES), lambda i: (i, 0)),
            pl.BlockSpec((SUBLANES, tm), lambda i: (0, i)),
            pl.BlockSpec((SUBLANES, LANES), lambda i: (0, 0)),
        ],
        out_shape=[
            jax.ShapeDtypeStruct((T, LANES), F32),
            jax.ShapeDtypeStruct((SUBLANES, T), F32),
            jax.ShapeDtypeStruct((SUBLANES, LANES), F32),
        ],
        scratch_shapes=[pltpu.VMEM((SUBLANES, LANES), F32)],
        compiler_params=_cparams("arbitrary"),
        name="router",
    )(logits)


def _dispatch_kernel(p1_ref, p2_ref, pad0_ref, padn_ref, nt_ref, u_ref, xs_hbm, zbuf_ref, sem, zsem,
                     *, pad_bits, tail_per_tile):
    tm = u_ref.shape[0]

    @pl.when(pl.program_id(0) == 0)
    def _():
        zbuf_ref[...] = jnp.zeros_like(zbuf_ref)

        def pad_copies(e, b):
            n = padn_ref[e]
            off = pad0_ref[e] + (n & ((1 << b) - 1))
            if (1 << b) >= SUBLANES:
                off = pl.multiple_of(off, SUBLANES)
                return [pltpu.make_async_copy(zbuf_ref.at[pl.ds(0, 1 << b)],
                                              xs_hbm.at[pl.ds(off, 1 << b)], zsem)]
            return [pltpu.make_async_copy(zbuf_ref.at[pl.ds(0, 1)], xs_hbm.at[pl.ds(off + r, 1)], zsem)
                    for r in range(1 << b)]

        for wait in (False, True):
            for e in range(N_EXPERTS):
                for b in range(pad_bits):
                    @pl.when(((padn_ref[e] >> b) & 1) == 1)
                    def _():
                        for cp in pad_copies(e, b):
                            if wait:
                                cp.wait()
                            else:
                                cp.start()

        zrows = zbuf_ref.shape[0]

        def tail_copy(t):
            off = pl.multiple_of(t * zrows, SUBLANES)
            return pltpu.make_async_copy(zbuf_ref, xs_hbm.at[pl.ds(off, zrows)], zsem)

        def tail_start(t, carry):
            tail_copy(t).start()
            return carry

        def tail_wait(t, carry):
            tail_copy(t).wait()
            return carry

        first, last = nt_ref[0] * tail_per_tile, xs_hbm.shape[0] // zrows
        lax.fori_loop(first, last, tail_start, 0)
        lax.fori_loop(first, last, tail_wait, 0)

    def row_copy(r, pos):
        return pltpu.make_async_copy(u_ref.at[pl.ds(r, 1)], xs_hbm.at[pl.ds(pos, 1)], sem)

    def issue(b, carry):
        for rr in range(ISSUE_UNROLL):
            r = b * ISSUE_UNROLL + rr
            row_copy(r, p1_ref[r]).start()
            row_copy(r, p2_ref[r]).start()
        return carry

    lax.fori_loop(0, tm // ISSUE_UNROLL, issue, 0)
    for _ in range(2):
        pltpu.make_async_copy(u_ref, xs_hbm.at[pl.ds(0, tm)], sem).wait()


def _dispatch(pos1, pos2, pad_start, pad_len, n_tiles, un, P, group_tile, tm=1024):
    T, D = un.shape
    tm = min(tm, T)
    pad_bits = group_tile.bit_length() - 1
    assert group_tile == 1 << pad_bits and pad_bits >= 1
    zrows = group_tile // 2
    kern = functools.partial(_dispatch_kernel, pad_bits=pad_bits, tail_per_tile=group_tile // zrows)
    smem = pl.BlockSpec(memory_space=pltpu.SMEM)
    return pl.pallas_call(
        kern,
        grid=(T // tm,),
        in_specs=[
            pl.BlockSpec((tm,), lambda i: (i,), memory_space=pltpu.SMEM),
            pl.BlockSpec((tm,), lambda i: (i,), memory_space=pltpu.SMEM),
            smem, smem, smem,
            pl.BlockSpec((tm, D), lambda i: (i, 0)),
        ],
        out_specs=pl.BlockSpec(memory_space=pl.ANY),
        out_shape=jax.ShapeDtypeStruct((P, D), un.dtype),
        scratch_shapes=[pltpu.VMEM((zrows, D), un.dtype),
                        pltpu.SemaphoreType.DMA(()), pltpu.SemaphoreType.DMA(())],
        compiler_params=_cparams("arbitrary"),
        name="dispatch",
    )(pos1, pos2, pad_start, pad_len, n_tiles, un)


def _combine_kernel(p1_ref, p2_ref, q1_ref, q2_ref, h_ref, rt_ref, g_ref, y_hbm, o_ref, buf_ref, sem,
                    *, final_norm):
    tm = h_ref.shape[0]
    i = pl.program_id(0)
    n = pl.num_programs(0)
    slot = i % 2

    def issue_tile(i1_ref, i2_ref, s):
        def row_copy(k, r, pos):
            return pltpu.make_async_copy(y_hbm.at[pl.ds(pos, 1)], buf_ref.at[s, k, pl.ds(r, 1)], sem.at[s])

        def issue(b, carry):
            for rr in range(ISSUE_UNROLL):
                r = b * ISSUE_UNROLL + rr
                row_copy(0, r, i1_ref[r]).start()
                row_copy(1, r, i2_ref[r]).start()
            return carry

        lax.fori_loop(0, tm // ISSUE_UNROLL, issue, 0)

    @pl.when(i == 0)
    def _():
        issue_tile(p1_ref, p2_ref, 0)

    @pl.when(i + 1 < n)
    def _():
        issue_tile(q1_ref, q2_ref, 1 - slot)

    for k in range(2):
        pltpu.make_async_copy(y_hbm.at[pl.ds(0, tm)], buf_ref.at[slot, k], sem.at[slot]).wait()

    rt = rt_ref[...]
    out = h_ref[...] + (rt[:, 2:3] * buf_ref[slot, 0] + rt[:, 3:4] * buf_ref[slot, 1])
    if final_norm:
        out = _rms_rows(out) * g_ref[...]
    o_ref[...] = out


def _combine(pos1, pos2, h, routing, g_final, y, final_norm, tm=1024):
    T, D = h.shape
    tm = min(tm, T)
    n = T // tm
    kern = functools.partial(_combine_kernel, final_norm=final_norm)
    cur = pl.BlockSpec((tm,), lambda i: (i,), memory_space=pltpu.SMEM)
    nxt = pl.BlockSpec((tm,), lambda i: (jnp.minimum(i + 1, n - 1),), memory_space=pltpu.SMEM)
    return pl.pallas_call(
        kern,
        grid=(n,),
        in_specs=[
            cur, cur, nxt, nxt,
            pl.BlockSpec((tm, D), lambda i: (i, 0)),
            pl.BlockSpec((tm, LANES), lambda i: (i, 0)),
            pl.BlockSpec((1, D), lambda i: (0, 0)),
            pl.BlockSpec(memory_space=pl.ANY),
        ],
        out_specs=pl.BlockSpec((tm, D), lambda i: (i, 0)),
        out_shape=jax.ShapeDtypeStruct((T, D), F32),
        scratch_shapes=[pltpu.VMEM((2, 2, tm, D), F32), pltpu.SemaphoreType.DMA((2,))],
        compiler_params=_cparams("arbitrary"),
        name="combine",
    )(pos1, pos2, pos1, pos2, h, routing, g_final, y)


def _final_norm_kernel(h_ref, g_ref, o_ref):
    o_ref[...] = _rms_rows(h_ref[...]) * g_ref[...]


def _final_norm(h, g, tm=1024):
    T, D = h.shape
    tm = min(tm, T)
    return pl.pallas_call(
        _final_norm_kernel,
        grid=(T // tm,),
        in_specs=[pl.BlockSpec((tm, D), lambda i: (i, 0)), pl.BlockSpec((1, D), lambda i: (0, 0))],
        out_specs=pl.BlockSpec((tm, D), lambda i: (i, 0)),
        out_shape=jax.ShapeDtypeStruct((T, D), F32),
        compiler_params=_cparams("parallel"),
        name="final_norm",
    )(h, g)


def _block_diag(w):
    G, n, _ = w.shape
    eye = jnp.eye(G, dtype=w.dtype)
    return (eye[:, None, :, None] * w[:, :, None, :]).reshape(G * n, G * n)


def _moe_layer(h, un, logits, w_gate, w_up, w_down, first_expert, g_final, final_norm, tm):
    T, D = h.shape
    E = N_EXPERTS
    routing, routing_t, counts = _router(logits)
    e1 = routing_t[0].astype(jnp.int32)
    e2 = routing_t[1].astype(jnp.int32)
    rank1 = routing_t[4].astype(jnp.int32)
    rank2 = routing_t[5].astype(jnp.int32)
    cnt = counts[0, :E].astype(jnp.int32)
    tiles_per = (cnt + tm - 1) // tm
    tile_end = jnp.cumsum(tiles_per)
    offs = (tile_end - tiles_per) * tm
    pos1 = offs[e1] + rank1
    pos2 = offs[e2] + rank2
    n_slots = (2 * T) // tm + E
    P = n_slots * tm
    tile_expert = jnp.minimum(
        jnp.sum(jnp.arange(n_slots, dtype=jnp.int32)[:, None] >= tile_end[None, :], axis=1), E - 1
    ).astype(jnp.int32)
    n_tiles = tile_end[E - 1:E].astype(jnp.int32)
    tile_in_group = jnp.arange(n_slots, dtype=jnp.int32) - (tile_end - tiles_per)[tile_expert]
    tile_rows = jnp.clip(cnt[tile_expert] - tile_in_group * tm, 0, tm).astype(jnp.int32)

    xs = _dispatch(pos1, pos2, offs + cnt, tiles_per * tm - cnt, n_tiles, un, P, tm)
    y = _ffn_grouped(tile_expert, tile_rows, n_tiles, xs, w_gate, w_up, w_down, first_expert, tm)
    return _combine(pos1, pos2, h, routing, g_final, y, final_norm)


def _forward(x, norm_mix_g, w_in, ml_b_if, ml_norm_g, rg_conv_w, rg_conv_b, rg_w_a, rg_b_a,
             rg_w_x, rg_b_x, rg_lam, rg_norm_g, w_out, norm_ffn_g, ffn_w_gate, ffn_w_up,
             ffn_w_down, moe_w_router, moe_w_gate, moe_w_up, moe_w_down, norm_final_g,
             moe_tile=2048):
    B, S, D = x.shape
    T = B * S
    depth = w_in.shape[0]
    ml_w = ML_HEADS * ML_HEAD_DIM
    n_q = 4 * ml_w
    n_if = 2 * ML_HEADS
    rg_w = rg_lam.shape[1]
    n_r = 2 * rg_w

    h = x.reshape(T, D)
    g_final = norm_final_g.reshape(1, D)
    moe_gate_all = moe_w_gate.reshape((-1,) + moe_w_gate.shape[2:])
    moe_up_all = moe_w_up.reshape((-1,) + moe_w_up.shape[2:])
    moe_down_all = moe_w_down.reshape((-1,) + moe_w_down.shape[2:])
    for l in range(depth):
        wl = w_in[l]
        w_all = jnp.concatenate(
            [wl[:, :ml_w], wl[:, 2 * ml_w:n_q], wl[:, n_q + n_if:], wl[:, n_q:n_q + n_if],
             jnp.zeros((D, LANES - n_if), wl.dtype)], axis=1).astype(BF16)
        wt = jnp.concatenate([wl[:, ml_w:2 * ml_w], wl[:, n_q:n_q + n_if]], axis=1).T.astype(BF16)
        b_if = jnp.concatenate([ml_b_if[l], jnp.zeros((LANES - n_if,), F32)]).reshape(1, LANES)
        zq, kt, zr, zg, zgt = _inproj(h, norm_mix_g[l].reshape(1, D), b_if, ml_b_if[l].reshape(n_if, 1),
                                      w_all, wt, 3 * ml_w, n_r)
        yml = _mlstm(zq, kt, zg, zgt, ml_norm_g[l].reshape(1, ml_w), B, S)

        w_gates = jnp.concatenate([_block_diag(rg_w_a[l]), _block_diag(rg_w_x[l])], axis=1).astype(BF16)
        b_gates = jnp.concatenate([rg_b_a[l], rg_b_x[l]]).reshape(1, n_r)
        yrg = _rglru(zr, rg_conv_w[l], rg_conv_b[l].reshape(1, rg_w), w_gates, b_gates,
                     rg_lam[l].reshape(1, rg_w), rg_norm_g[l].reshape(1, rg_w), B, S)

        j = l // 2
        is_moe = (l % 2 == 1)
        wo = w_out[l].astype(BF16)
        g_ffn = norm_ffn_g[l].reshape(1, D)
        last = (l == depth - 1)
        if is_moe:
            w_router = jnp.concatenate(
                [moe_w_router[j], jnp.zeros((D, LANES - N_EXPERTS), F32)], axis=1)
            h, un, logits = _outproj(yml, yrg, h, wo, g_ffn, w_router)
            h = _moe_layer(h, un, logits, moe_gate_all, moe_up_all, moe_down_all, j * N_EXPERTS,
                           g_final, last, moe_tile)
        else:
            h = _ffn_dense(yml, yrg, h, wo, g_ffn, ffn_w_gate, ffn_w_up, ffn_w_down, j)
            if last:
                h = _final_norm(h, g_final)
    return h.reshape(B, S, D)


def kernel(x, norm_mix_g, w_in, ml_b_if, ml_norm_g, rg_conv_w, rg_conv_b, rg_w_a, rg_b_a, rg_w_x,
           rg_b_x, rg_lam, rg_norm_g, w_out, norm_ffn_g, ffn_w_gate, ffn_w_up, ffn_w_down,
           moe_w_router, moe_w_gate, moe_w_up, moe_w_down, norm_final_g):
    return _forward(x, norm_mix_g, w_in, ml_b_if, ml_norm_g, rg_conv_w, rg_conv_b, rg_w_a, rg_b_a,
                    rg_w_x, rg_b_x, rg_lam, rg_norm_g, w_out, norm_ffn_g, ffn_w_gate, ffn_w_up,
                    ffn_w_down, moe_w_router, moe_w_gate, moe_w_up, moe_w_down, norm_final_g)
```

```python
import functools

import jax
import jax.numpy as jnp
from jax import lax
from jax.experimental import pallas as pl
from jax.experimental.pallas import tpu as pltpu

EPS = 1e-6
ML_HEADS = 4
ML_HEAD_DIM = 128
CHUNK = 128
RG_BLOCKS = 8
RG_C = 8.0
CONV_WIDTH = 4
N_EXPERTS = 8
LANES = 128
SUBLANES = 8
VMEM_LIMIT = 60 * 1024 * 1024
ISSUE_UNROLL = 8

BF16 = jnp.bfloat16
F32 = jnp.float32


def _cparams(*sem):
    return pltpu.CompilerParams(dimension_semantics=sem, vmem_limit_bytes=VMEM_LIMIT)


def _dot(a, b):
    return jnp.dot(a, b, preferred_element_type=F32)


def _dot_nt(a, b):
    return lax.dot_general(a, b, (((1,), (1,)), ((), ())), preferred_element_type=F32)


def _dot_tn(a, b):
    return lax.dot_general(a, b, (((0,), (0,)), ((), ())), preferred_element_type=F32)


def _dot_f32(a, b):
    return jnp.dot(a, b, preferred_element_type=F32, precision=lax.Precision.HIGHEST)


def _sigmoid(x):
    return 1.0 / (1.0 + jnp.exp(-x))


def _log_sigmoid(x):
    return jnp.minimum(x, 0.0) - jnp.log(1.0 + jnp.exp(-jnp.abs(x)))


def _softplus(x):
    return jnp.maximum(x, 0.0) + jnp.log(1.0 + jnp.exp(-jnp.abs(x)))


def _gelu_tanh(x):
    return 0.5 * x * (1.0 + jnp.tanh(0.7978845608028654 * (x + 0.044715 * (x * x * x))))


def _rms_rows(x):
    return x * lax.rsqrt(jnp.mean(x * x, axis=-1, keepdims=True) + EPS)


def _inproj_kernel(h_ref, g_ref, bt_ref, w_ref, wt_ref, zq_ref, kt_ref, zr_ref, zg_ref, zgt_ref,
                   *, n_q, n_r, n_col):
    xn = (_rms_rows(h_ref[...]) * g_ref[...]).astype(BF16)
    for c0 in range(0, n_q, n_col):
        zq_ref[:, c0:c0 + n_col] = _dot(xn, w_ref[:, c0:c0 + n_col]).astype(BF16)
    for c0 in range(0, n_r, n_col):
        zr_ref[:, c0:c0 + n_col] = _dot(xn, w_ref[:, n_q + c0:n_q + c0 + n_col]).astype(BF16)
    n_k = kt_ref.shape[1]
    t = _dot_nt(wt_ref[...], xn)
    pad = jnp.zeros((CHUNK - SUBLANES, CHUNK), F32)
    for c in range(kt_ref.shape[0]):
        kt_ref[c] = t[:n_k, c * CHUNK:(c + 1) * CHUNK].astype(BF16)
        gates_t = t[n_k:, c * CHUNK:(c + 1) * CHUNK] + bt_ref[...]
        zgt_ref[c] = gates_t
        zg_ref[c * CHUNK:(c + 1) * CHUNK, :] = jnp.transpose(jnp.concatenate([gates_t, pad], axis=0))


def _inproj(h, g, b_if_t, w_all, wt, n_q, n_r, tm=1024):
    T, D = h.shape
    tm = min(tm, T)
    n_all = w_all.shape[1]
    assert n_all == n_q + n_r
    n_g = LANES
    n_k = wt.shape[0] - SUBLANES
    kern = functools.partial(_inproj_kernel, n_q=n_q, n_r=n_r, n_col=512)
    const = lambda i: (0, 0)
    return pl.pallas_call(
        kern,
        grid=(T // tm,),
        in_specs=[
            pl.BlockSpec((tm, D), lambda i: (i, 0)),
            pl.BlockSpec((1, D), const),
            pl.BlockSpec((SUBLANES, 1), const),
            pl.BlockSpec((D, n_all), const),
            pl.BlockSpec((n_k + SUBLANES, D), const),
        ],
        out_specs=[
            pl.BlockSpec((tm, n_q), lambda i: (i, 0)),
            pl.BlockSpec((tm // CHUNK, n_k, CHUNK), lambda i: (i, 0, 0)),
            pl.BlockSpec((tm, n_r), lambda i: (i, 0)),
            pl.BlockSpec((tm, n_g), lambda i: (i, 0)),
            pl.BlockSpec((tm // CHUNK, SUBLANES, CHUNK), lambda i: (i, 0, 0)),
        ],
        out_shape=[
            jax.ShapeDtypeStruct((T, n_q), BF16),
            jax.ShapeDtypeStruct((T // CHUNK, n_k, CHUNK), BF16),
            jax.ShapeDtypeStruct((T, n_r), BF16),
            jax.ShapeDtypeStruct((T, n_g), F32),
            jax.ShapeDtypeStruct((T // CHUNK, SUBLANES, CHUNK), F32),
        ],
        compiler_params=_cparams("parallel"),
        name="inproj",
    )(h, g, b_if_t, w_all, wt)


def _mlstm_kernel(zq_ref, kt_ref, zg_ref, zgt_ref, gain_ref, y_ref, c_ref, m_ref, pcol_ref, prow_ref,
                  *, n_chunks):
    H, Dh, L = ML_HEADS, ML_HEAD_DIM, CHUNK
    W = H * Dh
    scale = Dh ** -0.5

    @pl.when(pl.program_id(1) == 0)
    def _():
        c_ref[...] = jnp.zeros_like(c_ref)
        m_ref[...] = jnp.zeros_like(m_ref)

    row = lax.broadcasted_iota(jnp.int32, (L, L), 0)
    col = lax.broadcasted_iota(jnp.int32, (L, L), 1)
    causal = col <= row
    tri_l = causal.astype(F32)
    tri_u = (row <= col).astype(F32)
    ones_blk = jnp.ones((L, Dh), BF16)

    def gate_prefix(c, slot):
        r0 = pl.multiple_of(c * L, L)
        gcol = zg_ref[pl.ds(r0, L), :]
        grow = zgt_ref[c]
        bcol_all = _dot_f32(tri_l, _log_sigmoid(gcol))
        brow_all = _dot_f32(_log_sigmoid(grow), tri_u)
        cm = gcol - pltpu.roll(bcol_all, LANES - H, 1)
        d = 1
        while d < L:
            cm = jnp.maximum(cm, jnp.where(row >= d, pltpu.roll(cm, d, 0), -jnp.inf))
            d *= 2
        pcol_ref[slot, 0] = bcol_all
        pcol_ref[slot, 1] = cm
        prow_ref[slot] = brow_all

    gate_prefix(0, 0)

    def chunk_body(c, carry):
        r0 = pl.multiple_of(c * L, L)
        slot = c % 2
        grow = zgt_ref[c]
        bcol_all = pcol_ref[slot, 0]
        cm = pcol_ref[slot, 1]
        brow_all = prow_ref[slot]

        for hd in range(H):
            q = zq_ref[pl.ds(r0, L), hd * Dh:(hd + 1) * Dh]
            v = zq_ref[pl.ds(r0, L), W + hd * Dh:W + (hd + 1) * Dh]
            o = zq_ref[pl.ds(r0, L), 2 * W + hd * Dh:2 * W + (hd + 1) * Dh]
            kt = kt_ref[c, hd * Dh:(hd + 1) * Dh, :]
            v_aug = jnp.concatenate([v, ones_blk], axis=1)

            li_row = grow[hd:hd + 1, :]
            b_row = brow_all[H + hd:H + hd + 1, :]
            g_tot = b_row[:, L - 1:L]
            r_row = li_row - b_row

            c_prev = c_ref[hd]
            m_prev = m_ref[hd]

            mx = jnp.maximum(jnp.broadcast_to(cm[:, hd:hd + 1], (L, L)), m_prev)
            bb = jnp.broadcast_to(bcol_all[:, H + hd:H + hd + 1], (L, L))
            p = jnp.exp(jnp.where(causal, r_row - mx, -jnp.inf))
            s_w = p * (_dot(q, kt) * scale)
            inter = jnp.exp(m_prev - mx)
            intra = _dot(s_w.astype(BF16), v_aug)
            cross = _dot(q, c_prev.astype(BF16))
            num = intra[:, :Dh] + inter * cross[:, :Dh]
            den = intra[:, Dh:] + inter * cross[:, Dh:]
            hh = num / jnp.maximum(jnp.abs(den), jnp.exp(-(bb + mx)))
            ht = _rms_rows(hh) * gain_ref[:, hd * Dh:(hd + 1) * Dh]
            y_ref[pl.ds(r0, L), hd * Dh:(hd + 1) * Dh] = (_sigmoid(o.astype(F32)) * ht).astype(BF16)

            a = g_tot + r_row
            m_loc = jnp.max(a, axis=1, keepdims=True)
            w = jnp.exp(a - m_loc) * scale
            c_loc = _dot((kt.astype(F32) * w).astype(BF16), v_aug)
            m_new = jnp.maximum(g_tot + m_prev, m_loc)
            s_old = jnp.exp(g_tot + m_prev - m_new)
            s_loc = jnp.exp(m_loc - m_new)
            c_ref[hd] = s_old * c_prev + s_loc * c_loc
            m_ref[hd] = m_new
        gate_prefix(jnp.minimum(c + 1, n_chunks - 1), 1 - slot)
        return carry

    lax.fori_loop(0, n_chunks, chunk_body, 0)


def _mlstm(zq, kt, zg, zgt, gain, B, S, rows=2048):
    T = B * S
    rows = min(rows, S)
    n_chunks = rows // CHUNK
    steps = S // rows
    W = ML_HEADS * ML_HEAD_DIM
    kern = functools.partial(_mlstm_kernel, n_chunks=n_chunks)
    return pl.pallas_call(
        kern,
        grid=(B, steps),
        in_specs=[
            pl.BlockSpec((rows, 3 * W), lambda b, s: (b * steps + s, 0)),
            pl.BlockSpec((n_chunks, W, CHUNK), lambda b, s: (b * steps + s, 0, 0)),
            pl.BlockSpec((rows, LANES), lambda b, s: (b * steps + s, 0)),
            pl.BlockSpec((n_chunks, SUBLANES, CHUNK), lambda b, s: (b * steps + s, 0, 0)),
            pl.BlockSpec((1, W), lambda b, s: (0, 0)),
        ],
        out_specs=pl.BlockSpec((rows, W), lambda b, s: (b * steps + s, 0)),
        out_shape=jax.ShapeDtypeStruct((T, W), BF16),
        scratch_shapes=[
            pltpu.VMEM((ML_HEADS, ML_HEAD_DIM, 2 * ML_HEAD_DIM), F32),
            pltpu.VMEM((ML_HEADS, 1, 1), F32),
            pltpu.VMEM((2, 2, CHUNK, LANES), F32),
            pltpu.VMEM((2, SUBLANES, CHUNK), F32),
        ],
        compiler_params=_cparams("parallel", "arbitrary"),
        name="mlstm",
    )(zq, kt, zg, zgt, gain)


def _rglru_kernel(zr_ref, cw_ref, cb_ref, wg_ref, bg_ref, lam_ref, gain_ref, y_ref,
                  nat_ref, ost_ref, a_ref, u_ref, p_ref, cx_ref, hc_ref, *, rows):
    Wd = y_ref.shape[1]
    NB = SUBLANES
    BL = rows // NB
    PITCH = BL + SUBLANES
    n_slab = Wd // LANES
    taps = CONV_WIDTH - 1

    @pl.when(pl.program_id(1) == 0)
    def _():
        cx_ref[...] = jnp.zeros_like(cx_ref)
        hc_ref[...] = jnp.zeros_like(hc_ref)

    for s in range(NB):
        blk = zr_ref[s * BL:(s + 1) * BL, :].astype(F32)
        for c in range(2 * n_slab):
            nat_ref[c, s * PITCH:s * PITCH + BL, :] = blk[:, c * LANES:(c + 1) * LANES]

    def interleaved(c):
        return jnp.concatenate([nat_ref[c, pl.ds(i, NB, stride=PITCH), :] for i in range(BL)], axis=0)

    first_sub = lax.broadcasted_iota(jnp.int32, (NB, LANES), 0) == 0
    xcs = []
    for c in range(n_slab):
        lanes = slice(c * LANES, (c + 1) * LANES)
        x = interleaved(c)
        head = []
        for k in range(taps, 0, -1):
            tail = x[(BL - k) * NB:(BL - k + 1) * NB, :]
            prev = cx_ref[c * taps + k - 1]
            head.append(jnp.where(first_sub, pltpu.roll(prev, 1, 0), pltpu.roll(tail, 1, 0)))
            cx_ref[c * taps + k - 1] = tail
        ext = jnp.concatenate(head + [x], axis=0)
        xc = cb_ref[:, lanes] + jnp.zeros((rows, LANES), F32)
        for j in range(CONV_WIDTH):
            xc = xc + cw_ref[j:j + 1, lanes] * ext[j * NB:j * NB + rows, :]
        xcs.append(xc)
    xc = jnp.concatenate(xcs, axis=1)

    gates = _dot(xc.astype(BF16), wg_ref[...]) + bg_ref[...]
    r = _sigmoid(gates[:, :Wd])
    ig = _sigmoid(gates[:, Wd:])
    a = jnp.exp(r * ((-RG_C) * _softplus(-lam_ref[...])))
    a_ref[...] = a
    d = 1.0 - a * a
    u_ref[...] = jnp.where(d > 0.0, d * lax.rsqrt(d), 0.0) * (ig * xc)

    for c in range(n_slab):
        lanes = slice(c * LANES, (c + 1) * LANES)
        h = jnp.zeros((NB, LANES), F32)
        p = jnp.ones((NB, LANES), F32)
        for i in range(BL):
            ai = a_ref[i * NB:(i + 1) * NB, lanes]
            h = ai * h + u_ref[i * NB:(i + 1) * NB, lanes]
            p = ai * p
            u_ref[i * NB:(i + 1) * NB, lanes] = h
            p_ref[i * NB:(i + 1) * NB, lanes] = p
    h_end = u_ref[(BL - 1) * NB:BL * NB, :]
    p_end = p_ref[(BL - 1) * NB:BL * NB, :]
    h_in = [hc_ref[0:1, :]]
    for s in range(NB):
        h_in.append(h_end[s:s + 1, :] + p_end[s:s + 1, :] * h_in[s])
    hc_ref[...] = jnp.broadcast_to(h_in[NB], hc_ref.shape)
    h_enter = jnp.concatenate(h_in[:NB], axis=0)
    h_all = u_ref[...] + p_ref[...] * jnp.concatenate([h_enter] * BL, axis=0)

    gate = jnp.concatenate([interleaved(n_slab + c) for c in range(n_slab)], axis=1)
    yv = h_all * _gelu_tanh(gate)
    yn = _rms_rows(yv) * gain_ref[...]
    for c in range(n_slab):
        for i in range(BL):
            ost_ref[c, pl.ds(i, NB, stride=PITCH), :] = yn[i * NB:(i + 1) * NB, c * LANES:(c + 1) * LANES]
    for s in range(NB):
        y_ref[s * BL:(s + 1) * BL, :] = jnp.concatenate(
            [ost_ref[c, s * PITCH:s * PITCH + BL, :] for c in range(n_slab)], axis=1).astype(BF16)


def _rglru(zr, conv_w, conv_b, w_gates, b_gates, lam, gain, B, S, rows=1024):
    T = B * S
    Wd = zr.shape[1] // 2
    rows = min(rows, S)
    steps = S // rows
    pitch_rows = SUBLANES * (rows // SUBLANES + SUBLANES)
    n_slab = Wd // LANES
    kern = functools.partial(_rglru_kernel, rows=rows)
    const = lambda b, s: (0, 0)
    return pl.pallas_call(
        kern,
        grid=(B, steps),
        in_specs=[
            pl.BlockSpec((rows, 2 * Wd), lambda b, s: (b * steps + s, 0)),
            pl.BlockSpec((CONV_WIDTH, Wd), const),
            pl.BlockSpec((1, Wd), const),
            pl.BlockSpec((Wd, 2 * Wd), const),
            pl.BlockSpec((1, 2 * Wd), const),
            pl.BlockSpec((1, Wd), const),
            pl.BlockSpec((1, Wd), const),
        ],
        out_specs=pl.BlockSpec((rows, Wd), lambda b, s: (b * steps + s, 0)),
        out_shape=jax.ShapeDtypeStruct((T, Wd), BF16),
        scratch_shapes=[
            pltpu.VMEM((2 * n_slab, pitch_rows, LANES), F32),
            pltpu.VMEM((n_slab, pitch_rows, LANES), F32),
            pltpu.VMEM((rows, Wd), F32),
            pltpu.VMEM((rows, Wd), F32),
            pltpu.VMEM((rows, Wd), F32),
            pltpu.VMEM((n_slab * (CONV_WIDTH - 1), SUBLANES, LANES), F32),
            pltpu.VMEM((SUBLANES, Wd), F32),
        ],
        compiler_params=_cparams("parallel", "arbitrary"),
        name="rglru",
    )(zr, conv_w, conv_b, w_gates, b_gates, lam, gain)


def _outproj_kernel(yml_ref, yrg_ref, h_ref, w_ref, g_ref, wr_ref, hn_ref, un_ref, lg_ref):
    half = yml_ref.shape[1]
    hn = (h_ref[...] + _dot(yml_ref[...], w_ref[:half, :]) + _dot(yrg_ref[...], w_ref[half:, :]))
    un = _rms_rows(hn) * g_ref[...]
    wr = wr_ref[...]
    w_hi = wr.astype(BF16)
    w_lo = (wr - w_hi.astype(F32)).astype(BF16)
    u_hi = un.astype(BF16)
    u_lo = (un - u_hi.astype(F32)).astype(BF16)
    both = _dot(u_hi, jnp.concatenate([w_hi, w_lo], axis=1))
    lg_ref[...] = both[:, :LANES] + (both[:, LANES:] + _dot(u_lo, w_hi))
    hn_ref[...] = hn
    un_ref[...] = un


def _outproj(yml, yrg, h, w_out, g_ffn, w_router, tm=1024):
    T, D = h.shape
    half = yml.shape[1]
    tm = min(tm, T)
    row = lambda i: (i, 0)
    const = lambda i: (0, 0)
    return pl.pallas_call(
        _outproj_kernel,
        grid=(T // tm,),
        in_specs=[
            pl.BlockSpec((tm, half), row),
            pl.BlockSpec((tm, half), row),
            pl.BlockSpec((tm, D), row),
            pl.BlockSpec((2 * half, D), const),
            pl.BlockSpec((1, D), const),
            pl.BlockSpec((D, LANES), const),
        ],
        out_specs=[pl.BlockSpec((tm, D), row), pl.BlockSpec((tm, D), row), pl.BlockSpec((tm, LANES), row)],
        out_shape=[jax.ShapeDtypeStruct((T, D), F32), jax.ShapeDtypeStruct((T, D), F32),
                   jax.ShapeDtypeStruct((T, LANES), F32)],
        compiler_params=_cparams("parallel"),
        name="outproj_router",
    )(yml, yrg, h, w_out, g_ffn, w_router)


SUB_ROWS = 1024
SUB_STEP = 256


def _swiglu_tile(x_ref, o_ref, init_fn, valid_fn, w_hbm, e_cur, e_next, first, has_next, wbufs, sems,
                 scratch, tf):
    wg_hbm, wu_hbm, wd_hbm = w_hbm
    gbuf, ubuf, dbuf = wbufs
    act_ref, wgb_ref, wub_ref, wdb_ref = scratch
    tm, D = o_ref.shape
    sub = min(SUB_ROWS, tm)
    nj = wg_hbm.shape[2] // tf

    def up_copies(e, j, slot):
        cols = pl.ds(pl.multiple_of(j * tf, LANES), tf)
        return (pltpu.make_async_copy(wg_hbm.at[e, :, cols], gbuf.at[slot], sems.at[0, slot]),
                pltpu.make_async_copy(wu_hbm.at[e, :, cols], ubuf.at[slot], sems.at[1, slot]))

    def down_copy(e, j, slot):
        rws = pl.ds(pl.multiple_of(j * tf, SUBLANES), tf)
        return pltpu.make_async_copy(wd_hbm.at[e, rws, :], dbuf.at[slot], sems.at[2, slot])

    def start_up(e, j, slot):
        for cp in up_copies(e, j, slot):
            cp.start()

    def wait_up(e, j, slot):
        for cp in up_copies(e, j, slot):
            cp.wait()

    def up(rows, s, slot):
        if s == 0:
            wgb_ref[...] = gbuf[slot].astype(BF16)
            wub_ref[...] = ubuf[slot].astype(BF16)
        x = x_ref[rows, :]
        g = _dot(x, wgb_ref[...])
        u = _dot(x, wub_ref[...])
        act_ref[rows, :] = (g * _sigmoid(g) * u).astype(BF16)

    def down(rows, s, slot):
        if s == 0:
            wdb_ref[...] = dbuf[slot].astype(BF16)
        o_ref[rows, :] += _dot(act_ref[rows, :], wdb_ref[...])

    def sub_tiles(body):
        for s in range(tm // sub):
            valid = valid_fn(s)
            if valid is None:
                body(pl.ds(s * sub, sub), s, sub)
                continue
            for z in list(range(SUB_STEP, sub, SUB_STEP)) + [sub]:
                on = valid > z - SUB_STEP
                if z < sub:
                    on = jnp.logical_and(on, valid <= z)
                pl.when(on)(functools.partial(body, pl.ds(s * sub, z), s, z))

    @pl.when(first)
    def _():
        start_up(e_cur, 0, 0)

    wait_up(e_cur, 0, 0)
    if nj > 1:
        start_up(e_cur, 1, 1)
    down_copy(e_cur, 0, 0).start()

    def first_body(rows, s, z):
        up(rows, s, 0)
        if init_fn is not None:
            o_ref[rows, :] = init_fn(rows)

    sub_tiles(first_body)

    def step(j, carry):
        slot = j % 2
        wait_up(e_cur, j, slot)
        down_copy(e_cur, j - 1, 1 - slot).wait()

        @pl.when(j + 1 < nj)
        def _():
            start_up(e_cur, j + 1, 1 - slot)

        down_copy(e_cur, j, slot).start()

        def body(rows, s, z):
            down(rows, s, 1 - slot)
            up(rows, s, slot)

        sub_tiles(body)
        return carry

    lax.fori_loop(1, nj, step, 0)

    last_slot = (nj - 1) % 2
    down_copy(e_cur, nj - 1, last_slot).wait()

    @pl.when(has_next)
    def _():
        start_up(e_next, 0, 0)

    def last_body(rows, s, z):
        down(rows, s, last_slot)
        if z < sub:
            o_ref[pl.ds(s * sub + z, sub - z), :] = jnp.zeros((sub - z, D), o_ref.dtype)

    sub_tiles(last_body)
    for s in range(tm // sub):
        valid = valid_fn(s)
        if valid is not None:
            @pl.when(valid <= 0)
            def _():
                o_ref[pl.ds(s * sub, sub), :] = jnp.zeros((sub, D), o_ref.dtype)


def _swiglu_scratch(tm, D, tf):
    return [pltpu.VMEM((2, D, tf), F32), pltpu.VMEM((2, D, tf), F32), pltpu.VMEM((2, tf, D), F32),
            pltpu.SemaphoreType.DMA((3, 2)),
            pltpu.VMEM((tm, tf), BF16), pltpu.VMEM((D, tf), BF16), pltpu.VMEM((D, tf), BF16),
            pltpu.VMEM((tf, D), BF16)]


def _ffn_dense_kernel(yml_ref, yrg_ref, h_ref, wo_ref, g_ref, wg_hbm, wu_hbm, wd_hbm, o_ref, un_ref,
                      gbuf, ubuf, dbuf, sems, *scratch, layer, tf):
    i = pl.program_id(0)
    tm = o_ref.shape[0]
    sub = min(SUB_ROWS, tm)
    half = yml_ref.shape[1]

    for s in range(tm // sub):
        rows = pl.ds(s * sub, sub)
        hn = (h_ref[rows, :] + _dot(yml_ref[rows, :], wo_ref[:half, :])
              + _dot(yrg_ref[rows, :], wo_ref[half:, :]))
        o_ref[rows, :] = hn
        un_ref[rows, :] = (_rms_rows(hn) * g_ref[...]).astype(BF16)

    _swiglu_tile(un_ref, o_ref, None, lambda s: None, (wg_hbm, wu_hbm, wd_hbm), layer, layer,
                 i == 0, i + 1 < pl.num_programs(0), (gbuf, ubuf, dbuf), sems, scratch, tf)


def _ffn_dense(yml, yrg, h, w_out, g_ffn, wg, wu, wd, layer, tm=1024, tf=256):
    T, D = h.shape
    half = yml.shape[1]
    F = wg.shape[2]
    tm = min(tm, T)
    tf = min(tf, F)
    row = lambda i: (i, 0)
    const = lambda i: (0, 0)
    hbm = pl.BlockSpec(memory_space=pl.ANY)
    kern = functools.partial(_ffn_dense_kernel, layer=layer, tf=tf)
    return pl.pallas_call(
        kern,
        grid=(T // tm,),
        in_specs=[pl.BlockSpec((tm, half), row), pl.BlockSpec((tm, half), row), pl.BlockSpec((tm, D), row),
                  pl.BlockSpec((2 * half, D), const), pl.BlockSpec((1, D), const), hbm, hbm, hbm],
        out_specs=pl.BlockSpec((tm, D), row),
        out_shape=jax.ShapeDtypeStruct((T, D), F32),
        scratch_shapes=[pltpu.VMEM((tm, D), BF16)] + _swiglu_scratch(tm, D, tf),
        compiler_params=_cparams("arbitrary"),
        name="ffn_dense",
    )(yml, yrg, h, w_out, g_ffn, wg, wu, wd)


def _ffn_grouped_kernel(te_ref, tr_ref, nt_ref, x_ref, wg_hbm, wu_hbm, wd_hbm, o_ref, xb_ref,
                        gbuf, ubuf, dbuf, sems, *scratch, first_expert, tf):
    i = pl.program_id(0)
    sub = min(SUB_ROWS, o_ref.shape[0])
    n_slots = pl.num_programs(0)

    @pl.when(i < nt_ref[0])
    def _():
        xb_ref[...] = x_ref[...].astype(BF16)
        e_next = first_expert + te_ref[jnp.minimum(i + 1, n_slots - 1)]
        _swiglu_tile(xb_ref, o_ref, lambda rows: jnp.zeros((rows.size, o_ref.shape[1]), F32),
                     lambda s: tr_ref[i] - s * sub, (wg_hbm, wu_hbm, wd_hbm), first_expert + te_ref[i],
                     e_next, i == 0, i + 1 < nt_ref[0], (gbuf, ubuf, dbuf), sems, scratch, tf)

    @pl.when(i >= nt_ref[0])
    def _():
        o_ref[...] = jnp.zeros_like(o_ref)


def _ffn_grouped(tile_expert, tile_rows, n_tiles, xs, wg, wu, wd, first_expert, tm, tf=512):
    P, D = xs.shape
    F = wg.shape[2]
    tf = min(tf, F)
    hbm = pl.BlockSpec(memory_space=pl.ANY)
    grid_spec = pltpu.PrefetchScalarGridSpec(
        num_scalar_prefetch=3,
        grid=(P // tm,),
        in_specs=[pl.BlockSpec((tm, D), lambda i, te, tr, nt: (jnp.minimum(i, nt[0] - 1), 0)), hbm, hbm, hbm],
        out_specs=pl.BlockSpec((tm, D), lambda i, te, tr, nt: (i, 0)),
        scratch_shapes=[pltpu.VMEM((tm, D), BF16)] + _swiglu_scratch(tm, D, tf),
    )
    kern = functools.partial(_ffn_grouped_kernel, first_expert=first_expert, tf=tf)
    return pl.pallas_call(
        kern,
        grid_spec=grid_spec,
        out_shape=jax.ShapeDtypeStruct((P, D), F32),
        compiler_params=_cparams("arbitrary"),
        name="ffn_grouped",
    )(tile_expert, tile_rows, n_tiles, xs, wg, wu, wd)


def _router_kernel(lg_ref, rt_ref, rtt_ref, cnt_ref, carry_ref):
    tm = lg_ref.shape[0]

    @pl.when(pl.program_id(0) == 0)
    def _():
        carry_ref[...] = jnp.zeros_like(carry_ref)

    lane = lax.broadcasted_iota(jnp.int32, (tm, LANES), 1).astype(F32)
    lg = jnp.where(lane < N_EXPERTS, lg_ref[...], -jnp.inf)
    v1 = jnp.max(lg, axis=1, keepdims=True)
    e1 = jnp.min(jnp.where(lg == v1, lane, float(LANES)), axis=1, keepdims=True)
    lg2 = jnp.where(lane == e1, -jnp.inf, lg)
    v2 = jnp.max(lg2, axis=1, keepdims=True)
    e2 = jnp.min(jnp.where(lg2 == v2, lane, float(LANES)), axis=1, keepdims=True)
    ex = jnp.exp(v2 - v1)
    w1 = 1.0 / (1.0 + ex)
    w2 = ex / (1.0 + ex)

    oh1 = (lane == e1).astype(F32)
    oh2 = (lane == e2).astype(F32)
    r_i = lax.broadcasted_iota(jnp.int32, (tm, tm), 0)
    c_i = lax.broadcasted_iota(jnp.int32, (tm, tm), 1)
    strict = (c_i < r_i).astype(BF16)
    before = _dot(strict, (oh1 + oh2).astype(BF16)) + carry_ref[0:1, :]
    rank1 = jnp.sum(before * oh1, axis=1, keepdims=True)
    rank2 = jnp.sum(before * oh2, axis=1, keepdims=True)
    total = carry_ref[0:1, :] + jnp.sum(oh1 + oh2, axis=0, keepdims=True)
    carry_ref[...] = jnp.broadcast_to(total, carry_ref.shape)
    cnt_ref[...] = jnp.broadcast_to(total, cnt_ref.shape)

    out = jnp.where(lane == 0, e1, 0.0)
    out = jnp.where(lane == 1, e2, out)
    out = jnp.where(lane == 2, w1, out)
    out = jnp.where(lane == 3, w2, out)
    out = jnp.where(lane == 4, rank1, out)
    out = jnp.where(lane == 5, rank2, out)
    rt_ref[...] = out
    rtt_ref[...] = jnp.transpose(out)[:SUBLANES, :]


def _router(logits, tm=1024):
    T = logits.shape[0]
    tm = min(tm, T)
    return pl.pallas_call(
        _router_kernel,
        grid=(T // tm,),
        in_specs=[pl.BlockSpec((tm, LANES), lambda i: (i, 0))],
        out_specs=[
            pl.BlockSpec((tm, LANES), lambda i: (i, 0)),
            pl.BlockSpec((SUBLANES, tm), lambda i: (0, i)),
            pl.BlockSpec((SUBLANES, LANES), lambda i: (0, 0)),
        ],
        out_shape=[
            jax.ShapeDtypeStruct((T, LANES), F32),
            jax.ShapeDtypeStruct((SUBLANES, T), F32),
            jax.ShapeDtypeStruct((SUBLANES, LANES), F32),
        ],
        scratch_shapes=[pltpu.VMEM((SUBLANES, LANES), F32)],
        compiler_params=_cparams("arbitrary"),
        name="router",
    )(logits)


def _dispatch_kernel(p1_ref, p2_ref, pad0_ref, padn_ref, nt_ref, u_ref, xs_hbm, zbuf_ref, sem, zsem,
                     *, pad_bits, tail_per_tile):
    tm = u_ref.shape[0]

    @pl.when(pl.program_id(0) == 0)
    def _():
        zbuf_ref[...] = jnp.zeros_like(zbuf_ref)

        def pad_copies(e, b):
            n = padn_ref[e]
            off = pad0_ref[e] + (n & ((1 << b) - 1))
            if (1 << b) >= SUBLANES:
                off = pl.multiple_of(off, SUBLANES)
                return [pltpu.make_async_copy(zbuf_ref.at[pl.ds(0, 1 << b)],
                                              xs_hbm.at[pl.ds(off, 1 << b)], zsem)]
            return [pltpu.make_async_copy(zbuf_ref.at[pl.ds(0, 1)], xs_hbm.at[pl.ds(off + r, 1)], zsem)
                    for r in range(1 << b)]

        for wait in (False, True):
            for e in range(N_EXPERTS):
                for b in range(pad_bits):
                    @pl.when(((padn_ref[e] >> b) & 1) == 1)
                    def _():
                        for cp in pad_copies(e, b):
                            if wait:
                                cp.wait()
                            else:
                                cp.start()

        zrows = zbuf_ref.shape[0]

        def tail_copy(t):
            off = pl.multiple_of(t * zrows, SUBLANES)
            return pltpu.make_async_copy(zbuf_ref, xs_hbm.at[pl.ds(off, zrows)], zsem)

        def tail_start(t, carry):
            tail_copy(t).start()
            return carry

        def tail_wait(t, carry):
            tail_copy(t).wait()
            return carry

        first, last = nt_ref[0] * tail_per_tile, xs_hbm.shape[0] // zrows
        lax.fori_loop(first, last, tail_start, 0)
        lax.fori_loop(first, last, tail_wait, 0)

    def row_copy(r, pos):
        return pltpu.make_async_copy(u_ref.at[pl.ds(r, 1)], xs_hbm.at[pl.ds(pos, 1)], sem)

    def issue(b, carry):
        for rr in range(ISSUE_UNROLL):
            r = b * ISSUE_UNROLL + rr
            row_copy(r, p1_ref[r]).start()
            row_copy(r, p2_ref[r]).start()
        return carry

    lax.fori_loop(0, tm // ISSUE_UNROLL, issue, 0)
    for _ in range(2):
        pltpu.make_async_copy(u_ref, xs_hbm.at[pl.ds(0, tm)], sem).wait()


def _dispatch(pos1, pos2, pad_start, pad_len, n_tiles, un, P, group_tile, tm=1024):
    T, D = un.shape
    tm = min(tm, T)
    pad_bits = group_tile.bit_length() - 1
    assert group_tile == 1 << pad_bits and pad_bits >= 1
    zrows = group_tile // 2
    kern = functools.partial(_dispatch_kernel, pad_bits=pad_bits, tail_per_tile=group_tile // zrows)
    smem = pl.BlockSpec(memory_space=pltpu.SMEM)
    return pl.pallas_call(
        kern,
        grid=(T // tm,),
        in_specs=[
            pl.BlockSpec((tm,), lambda i: (i,), memory_space=pltpu.SMEM),
            pl.BlockSpec((tm,), lambda i: (i,), memory_space=pltpu.SMEM),
            smem, smem, smem,
            pl.BlockSpec((tm, D), lambda i: (i, 0)),
        ],
        out_specs=pl.BlockSpec(memory_space=pl.ANY),
        out_shape=jax.ShapeDtypeStruct((P, D), un.dtype),
        scratch_shapes=[pltpu.VMEM((zrows, D), un.dtype),
                        pltpu.SemaphoreType.DMA(()), pltpu.SemaphoreType.DMA(())],
        compiler_params=_cparams("arbitrary"),
        name="dispatch",
    )(pos1, pos2, pad_start, pad_len, n_tiles, un)


def _combine_kernel(p1_ref, p2_ref, q1_ref, q2_ref, h_ref, rt_ref, g_ref, y_hbm, o_ref, buf_ref, sem,
                    *, final_norm):
    tm = h_ref.shape[0]
    i = pl.program_id(0)
    n = pl.num_programs(0)
    slot = i % 2

    def issue_tile(i1_ref, i2_ref, s):
        def row_copy(k, r, pos):
            return pltpu.make_async_copy(y_hbm.at[pl.ds(pos, 1)], buf_ref.at[s, k, pl.ds(r, 1)], sem.at[s])

        def issue(b, carry):
            for rr in range(ISSUE_UNROLL):
                r = b * ISSUE_UNROLL + rr
                row_copy(0, r, i1_ref[r]).start()
                row_copy(1, r, i2_ref[r]).start()
            return carry

        lax.fori_loop(0, tm // ISSUE_UNROLL, issue, 0)

    @pl.when(i == 0)
    def _():
        issue_tile(p1_ref, p2_ref, 0)

    @pl.when(i + 1 < n)
    def _():
        issue_tile(q1_ref, q2_ref, 1 - slot)

    for k in range(2):
        pltpu.make_async_copy(y_hbm.at[pl.ds(0, tm)], buf_ref.at[slot, k], sem.at[slot]).wait()

    rt = rt_ref[...]
    out = h_ref[...] + (rt[:, 2:3] * buf_ref[slot, 0] + rt[:, 3:4] * buf_ref[slot, 1])
    if final_norm:
        out = _rms_rows(out) * g_ref[...]
    o_ref[...] = out


def _combine(pos1, pos2, h, routing, g_final, y, final_norm, tm=1024):
    T, D = h.shape
    tm = min(tm, T)
    n = T // tm
    kern = functools.partial(_combine_kernel, final_norm=final_norm)
    cur = pl.BlockSpec((tm,), lambda i: (i,), memory_space=pltpu.SMEM)
    nxt = pl.BlockSpec((tm,), lambda i: (jnp.minimum(i + 1, n - 1),), memory_space=pltpu.SMEM)
    return pl.pallas_call(
        kern,
        grid=(n,),
        in_specs=[
            cur, cur, nxt, nxt,
            pl.BlockSpec((tm, D), lambda i: (i, 0)),
            pl.BlockSpec((tm, LANES), lambda i: (i, 0)),
            pl.BlockSpec((1, D), lambda i: (0, 0)),
            pl.BlockSpec(memory_space=pl.ANY),
        ],
        out_specs=pl.BlockSpec((tm, D), lambda i: (i, 0)),
        out_shape=jax.ShapeDtypeStruct((T, D), F32),
        scratch_shapes=[pltpu.VMEM((2, 2, tm, D), F32), pltpu.SemaphoreType.DMA((2,))],
        compiler_params=_cparams("arbitrary"),
        name="combine",
    )(pos1, pos2, pos1, pos2, h, routing, g_final, y)


def _final_norm_kernel(h_ref, g_ref, o_ref):
    o_ref[...] = _rms_rows(h_ref[...]) * g_ref[...]


def _final_norm(h, g, tm=1024):
    T, D = h.shape
    tm = min(tm, T)
    return pl.pallas_call(
        _final_norm_kernel,
        grid=(T // tm,),
        in_specs=[pl.BlockSpec((tm, D), lambda i: (i, 0)), pl.BlockSpec((1, D), lambda i: (0, 0))],
        out_specs=pl.BlockSpec((tm, D), lambda i: (i, 0)),
        out_shape=jax.ShapeDtypeStruct((T, D), F32),
        compiler_params=_cparams("parallel"),
        name="final_norm",
    )(h, g)


def _block_diag(w):
    G, n, _ = w.shape
    eye = jnp.eye(G, dtype=w.dtype)
    return (eye[:, None, :, None] * w[:, :, None, :]).reshape(G * n, G * n)


def _moe_layer(h, un, logits, w_gate, w_up, w_down, first_expert, g_final, final_norm, tm):
    T, D = h.shape
    E = N_EXPERTS
    routing, routing_t, counts = _router(logits)
    e1 = routing_t[0].astype(jnp.int32)
    e2 = routing_t[1].astype(jnp.int32)
    rank1 = routing_t[4].astype(jnp.int32)
    rank2 = routing_t[5].astype(jnp.int32)
    cnt = counts[0, :E].astype(jnp.int32)
    tiles_per = (cnt + tm - 1) // tm
    tile_end = jnp.cumsum(tiles_per)
    offs = (tile_end - tiles_per) * tm
    pos1 = offs[e1] + rank1
    pos2 = offs[e2] + rank2
    n_slots = (2 * T) // tm + E
    P = n_slots * tm
    tile_expert = jnp.minimum(
        jnp.sum(jnp.arange(n_slots, dtype=jnp.int32)[:, None] >= tile_end[None, :], axis=1), E - 1
    ).astype(jnp.int32)
    n_tiles = tile_end[E - 1:E].astype(jnp.int32)
    tile_in_group = jnp.arange(n_slots, dtype=jnp.int32) - (tile_end - tiles_per)[tile_expert]
    tile_rows = jnp.clip(cnt[tile_expert] - tile_in_group * tm, 0, tm).astype(jnp.int32)

    xs = _dispatch(pos1, pos2, offs + cnt, tiles_per * tm - cnt, n_tiles, un, P, tm)
    y = _ffn_grouped(tile_expert, tile_rows, n_tiles, xs, w_gate, w_up, w_down, first_expert, tm)
    return _combine(pos1, pos2, h, routing, g_final, y, final_norm)


def _forward(x, norm_mix_g, w_in, ml_b_if, ml_norm_g, rg_conv_w, rg_conv_b, rg_w_a, rg_b_a,
             rg_w_x, rg_b_x, rg_lam, rg_norm_g, w_out, norm_ffn_g, ffn_w_gate, ffn_w_up,
             ffn_w_down, moe_w_router, moe_w_gate, moe_w_up, moe_w_down, norm_final_g,
             moe_tile=2048):
    B, S, D = x.shape
    T = B * S
    depth = w_in.shape[0]
    ml_w = ML_HEADS * ML_HEAD_DIM
    n_q = 4 * ml_w
    n_if = 2 * ML_HEADS
    rg_w = rg_lam.shape[1]
    n_r = 2 * rg_w

    h = x.reshape(T, D)
    g_final = norm_final_g.reshape(1, D)
    moe_gate_all = moe_w_gate.reshape((-1,) + moe_w_gate.shape[2:])
    moe_up_all = moe_w_up.reshape((-1,) + moe_w_up.shape[2:])
    moe_down_all = moe_w_down.reshape((-1,) + moe_w_down.shape[2:])
    for l in range(depth):
        wl = w_in[l]
        w_all = jnp.concatenate([wl[:, :ml_w], wl[:, 2 * ml_w:n_q], wl[:, n_q + n_if:]], axis=1).astype(BF16)
        wt = jnp.concatenate([wl[:, ml_w:2 * ml_w], wl[:, n_q:n_q + n_if]], axis=1).T.astype(BF16)
        zq, kt, zr, zg, zgt = _inproj(h, norm_mix_g[l].reshape(1, D), ml_b_if[l].reshape(n_if, 1),
                                      w_all, wt, 3 * ml_w, n_r)
        yml = _mlstm(zq, kt, zg, zgt, ml_norm_g[l].reshape(1, ml_w), B, S)

        w_gates = jnp.concatenate([_block_diag(rg_w_a[l]), _block_diag(rg_w_x[l])], axis=1).astype(BF16)
        b_gates = jnp.concatenate([rg_b_a[l], rg_b_x[l]]).reshape(1, n_r)
        yrg = _rglru(zr, rg_conv_w[l], rg_conv_b[l].reshape(1, rg_w), w_gates, b_gates,
                     rg_lam[l].reshape(1, rg_w), rg_norm_g[l].reshape(1, rg_w), B, S)

        j = l // 2
        is_moe = (l % 2 == 1)
        wo = w_out[l].astype(BF16)
        g_ffn = norm_ffn_g[l].reshape(1, D)
        last = (l == depth - 1)
        if is_moe:
            w_router = jnp.concatenate(
                [moe_w_router[j], jnp.zeros((D, LANES - N_EXPERTS), F32)], axis=1)
            h, un, logits = _outproj(yml, yrg, h, wo, g_ffn, w_router)
            h = _moe_layer(h, un, logits, moe_gate_all, moe_up_all, moe_down_all, j * N_EXPERTS,
                           g_final, last, moe_tile)
        else:
            h = _ffn_dense(yml, yrg, h, wo, g_ffn, ffn_w_gate, ffn_w_up, ffn_w_down, j)
            if last:
                h = _final_norm(h, g_final)
    return h.reshape(B, S, D)


def kernel(x, norm_mix_g, w_in, ml_b_if, ml_norm_g, rg_conv_w, rg_conv_b, rg_w_a, rg_b_a, rg_w_x,
           rg_b_x, rg_lam, rg_norm_g, w_out, norm_ffn_g, ffn_w_gate, ffn_w_up, ffn_w_down,
           moe_w_router, moe_w_gate, moe_w_up, moe_w_down, norm_final_g):
    return _forward(x, norm_mix_g, w_in, ml_b_if, ml_norm_g, rg_conv_w, rg_conv_b, rg_w_a, rg_b_a,
                    rg_w_x, rg_b_x, rg_lam, rg_norm_g, w_out, norm_ffn_g, ffn_w_gate, ffn_w_up,
                    ffn_w_down, moe_w_router, moe_w_gate, moe_w_up, moe_w_down, norm_final_g)
```

```python
import functools

import jax
import jax.numpy as jnp
from jax import lax
from jax.experimental import pallas as pl
from jax.experimental.pallas import tpu as pltpu

EPS = 1e-6
ML_HEADS = 4
ML_HEAD_DIM = 128
CHUNK = 128
RG_BLOCKS = 8
RG_C = 8.0
CONV_WIDTH = 4
N_EXPERTS = 8
LANES = 128
SUBLANES = 8
VMEM_LIMIT = 60 * 1024 * 1024
ISSUE_UNROLL = 8

BF16 = jnp.bfloat16
F32 = jnp.float32


def _cparams(*sem):
    return pltpu.CompilerParams(dimension_semantics=sem, vmem_limit_bytes=VMEM_LIMIT)


def _dot(a, b):
    return jnp.dot(a, b, preferred_element_type=F32)


def _dot_nt(a, b):
    return lax.dot_general(a, b, (((1,), (1,)), ((), ())), preferred_element_type=F32)


def _dot_tn(a, b):
    return lax.dot_general(a, b, (((0,), (0,)), ((), ())), preferred_element_type=F32)


def _dot_f32(a, b):
    return jnp.dot(a, b, preferred_element_type=F32, precision=lax.Precision.HIGHEST)


def _sigmoid(x):
    return 1.0 / (1.0 + jnp.exp(-x))


def _log_sigmoid(x):
    return jnp.minimum(x, 0.0) - jnp.log(1.0 + jnp.exp(-jnp.abs(x)))


def _softplus(x):
    return jnp.maximum(x, 0.0) + jnp.log(1.0 + jnp.exp(-jnp.abs(x)))


def _gelu_tanh(x):
    return 0.5 * x * (1.0 + jnp.tanh(0.7978845608028654 * (x + 0.044715 * (x * x * x))))


def _rms_rows(x):
    return x * lax.rsqrt(jnp.mean(x * x, axis=-1, keepdims=True) + EPS)


def _inproj_kernel(h_ref, g_ref, bt_ref, w_ref, wt_ref, zq_ref, kt_ref, zr_ref, zg_ref, zgt_ref,
                   *, n_q, n_r, n_col):
    xn = (_rms_rows(h_ref[...]) * g_ref[...]).astype(BF16)
    for c0 in range(0, n_q, n_col):
        zq_ref[:, c0:c0 + n_col] = _dot(xn, w_ref[:, c0:c0 + n_col]).astype(BF16)
    for c0 in range(0, n_r, n_col):
        zr_ref[:, c0:c0 + n_col] = _dot(xn, w_ref[:, n_q + c0:n_q + c0 + n_col]).astype(BF16)
    n_k = kt_ref.shape[1]
    t = _dot_nt(wt_ref[...], xn)
    pad = jnp.zeros((CHUNK - SUBLANES, CHUNK), F32)
    for c in range(kt_ref.shape[0]):
        kt_ref[c] = t[:n_k, c * CHUNK:(c + 1) * CHUNK].astype(BF16)
        gates_t = t[n_k:, c * CHUNK:(c + 1) * CHUNK] + bt_ref[...]
        zgt_ref[c] = gates_t
        zg_ref[c * CHUNK:(c + 1) * CHUNK, :] = jnp.transpose(jnp.concatenate([gates_t, pad], axis=0))


def _inproj(h, g, b_if_t, w_all, wt, n_q, n_r, tm=1024):
    T, D = h.shape
    tm = min(tm, T)
    n_all = w_all.shape[1]
    assert n_all == n_q + n_r
    n_g = LANES
    n_k = wt.shape[0] - SUBLANES
    kern = functools.partial(_inproj_kernel, n_q=n_q, n_r=n_r, n_col=512)
    const = lambda i: (0, 0)
    return pl.pallas_call(
        kern,
        grid=(T // tm,),
        in_specs=[
            pl.BlockSpec((tm, D), lambda i: (i, 0)),
            pl.BlockSpec((1, D), const),
            pl.BlockSpec((SUBLANES, 1), const),
            pl.BlockSpec((D, n_all), const),
            pl.BlockSpec((n_k + SUBLANES, D), const),
        ],
        out_specs=[
            pl.BlockSpec((tm, n_q), lambda i: (i, 0)),
            pl.BlockSpec((tm // CHUNK, n_k, CHUNK), lambda i: (i, 0, 0)),
            pl.BlockSpec((tm, n_r), lambda i: (i, 0)),
            pl.BlockSpec((tm, n_g), lambda i: (i, 0)),
            pl.BlockSpec((tm // CHUNK, SUBLANES, CHUNK), lambda i: (i, 0, 0)),
        ],
        out_shape=[
            jax.ShapeDtypeStruct((T, n_q), BF16),
            jax.ShapeDtypeStruct((T // CHUNK, n_k, CHUNK), BF16),
            jax.ShapeDtypeStruct((T, n_r), BF16),
            jax.ShapeDtypeStruct((T, n_g), F32),
            jax.ShapeDtypeStruct((T // CHUNK, SUBLANES, CHUNK), F32),
        ],
        compiler_params=_cparams("parallel"),
        name="inproj",
    )(h, g, b_if_t, w_all, wt)


def _mlstm_kernel(zq_ref, kt_ref, zg_ref, zgt_ref, gain_ref, y_ref, c_ref, m_ref, pcol_ref, prow_ref,
                  *, n_chunks):
    H, Dh, L = ML_HEADS, ML_HEAD_DIM, CHUNK
    W = H * Dh
    scale = Dh ** -0.5

    @pl.when(pl.program_id(1) == 0)
    def _():
        c_ref[...] = jnp.zeros_like(c_ref)
        m_ref[...] = jnp.zeros_like(m_ref)

    row = lax.broadcasted_iota(jnp.int32, (L, L), 0)
    col = lax.broadcasted_iota(jnp.int32, (L, L), 1)
    causal = col <= row
    tri_l = causal.astype(F32)
    tri_u = (row <= col).astype(F32)
    ones_blk = jnp.ones((L, Dh), BF16)

    def gate_prefix(c, slot):
        r0 = pl.multiple_of(c * L, L)
        gcol = zg_ref[pl.ds(r0, L), :]
        grow = zgt_ref[c]
        bcol_all = _dot_f32(tri_l, _log_sigmoid(gcol))
        brow_all = _dot_f32(_log_sigmoid(grow), tri_u)
        cm = gcol - pltpu.roll(bcol_all, LANES - H, 1)
        d = 1
        while d < L:
            cm = jnp.maximum(cm, jnp.where(row >= d, pltpu.roll(cm, d, 0), -jnp.inf))
            d *= 2
        pcol_ref[slot, 0] = bcol_all
        pcol_ref[slot, 1] = cm
        prow_ref[slot] = brow_all

    gate_prefix(0, 0)

    def chunk_body(c, carry):
        r0 = pl.multiple_of(c * L, L)
        slot = c % 2
        grow = zgt_ref[c]
        bcol_all = pcol_ref[slot, 0]
        cm = pcol_ref[slot, 1]
        brow_all = prow_ref[slot]

        for hd in range(H):
            q = zq_ref[pl.ds(r0, L), hd * Dh:(hd + 1) * Dh]
            v = zq_ref[pl.ds(r0, L), W + hd * Dh:W + (hd + 1) * Dh]
            o = zq_ref[pl.ds(r0, L), 2 * W + hd * Dh:2 * W + (hd + 1) * Dh]
            kt = kt_ref[c, hd * Dh:(hd + 1) * Dh, :]
            v_aug = jnp.concatenate([v, ones_blk], axis=1)

            li_row = grow[hd:hd + 1, :]
            b_row = brow_all[H + hd:H + hd + 1, :]
            g_tot = b_row[:, L - 1:L]
            r_row = li_row - b_row

            c_prev = c_ref[hd]
            m_prev = m_ref[hd]

            mx = jnp.maximum(jnp.broadcast_to(cm[:, hd:hd + 1], (L, L)), m_prev)
            bb = jnp.broadcast_to(bcol_all[:, H + hd:H + hd + 1], (L, L))
            p = jnp.exp(jnp.where(causal, r_row - mx, -jnp.inf))
            s_w = p * (_dot(q, kt) * scale)
            inter = jnp.exp(m_prev - mx)
            intra = _dot(s_w.astype(BF16), v_aug)
            cross = _dot(q, c_prev.astype(BF16))
            num = intra[:, :Dh] + inter * cross[:, :Dh]
            den = intra[:, Dh:] + inter * cross[:, Dh:]
            hh = num / jnp.maximum(jnp.abs(den), jnp.exp(-(bb + mx)))
            ht = _rms_rows(hh) * gain_ref[:, hd * Dh:(hd + 1) * Dh]
            y_ref[pl.ds(r0, L), hd * Dh:(hd + 1) * Dh] = (_sigmoid(o.astype(F32)) * ht).astype(BF16)

            a = g_tot + r_row
            m_loc = jnp.max(a, axis=1, keepdims=True)
            w = jnp.exp(a - m_loc) * scale
            c_loc = _dot((kt.astype(F32) * w).astype(BF16), v_aug)
            m_new = jnp.maximum(g_tot + m_prev, m_loc)
            s_old = jnp.exp(g_tot + m_prev - m_new)
            s_loc = jnp.exp(m_loc - m_new)
            c_ref[hd] = s_old * c_prev + s_loc * c_loc
            m_ref[hd] = m_new
        gate_prefix(jnp.minimum(c + 1, n_chunks - 1), 1 - slot)
        return carry

    lax.fori_loop(0, n_chunks, chunk_body, 0)


def _mlstm(zq, kt, zg, zgt, gain, B, S, rows=2048):
    T = B * S
    rows = min(rows, S)
    n_chunks = rows // CHUNK
    steps = S // rows
    W = ML_HEADS * ML_HEAD_DIM
    kern = functools.partial(_mlstm_kernel, n_chunks=n_chunks)
    return pl.pallas_call(
        kern,
        grid=(B, steps),
        in_specs=[
            pl.BlockSpec((rows, 3 * W), lambda b, s: (b * steps + s, 0)),
            pl.BlockSpec((n_chunks, W, CHUNK), lambda b, s: (b * steps + s, 0, 0)),
            pl.BlockSpec((rows, LANES), lambda b, s: (b * steps + s, 0)),
            pl.BlockSpec((n_chunks, SUBLANES, CHUNK), lambda b, s: (b * steps + s, 0, 0)),
            pl.BlockSpec((1, W), lambda b, s: (0, 0)),
        ],
        out_specs=pl.BlockSpec((rows, W), lambda b, s: (b * steps + s, 0)),
        out_shape=jax.ShapeDtypeStruct((T, W), BF16),
        scratch_shapes=[
            pltpu.VMEM((ML_HEADS, ML_HEAD_DIM, 2 * ML_HEAD_DIM), F32),
            pltpu.VMEM((ML_HEADS, 1, 1), F32),
            pltpu.VMEM((2, 2, CHUNK, LANES), F32),
            pltpu.VMEM((2, SUBLANES, CHUNK), F32),
        ],
        compiler_params=_cparams("parallel", "arbitrary"),
        name="mlstm",
    )(zq, kt, zg, zgt, gain)


def _rglru_kernel(zr_ref, cw_ref, cb_ref, wg_ref, bg_ref, lam_ref, gain_ref, y_ref,
                  nat_ref, ost_ref, a_ref, u_ref, p_ref, cx_ref, hc_ref, *, rows):
    Wd = y_ref.shape[1]
    NB = SUBLANES
    BL = rows // NB
    PITCH = BL + SUBLANES
    n_slab = Wd // LANES
    taps = CONV_WIDTH - 1

    @pl.when(pl.program_id(1) == 0)
    def _():
        cx_ref[...] = jnp.zeros_like(cx_ref)
        hc_ref[...] = jnp.zeros_like(hc_ref)

    for s in range(NB):
        blk = zr_ref[s * BL:(s + 1) * BL, :].astype(F32)
        for c in range(2 * n_slab):
            nat_ref[c, s * PITCH:s * PITCH + BL, :] = blk[:, c * LANES:(c + 1) * LANES]

    def interleaved(c):
        return jnp.concatenate([nat_ref[c, pl.ds(i, NB, stride=PITCH), :] for i in range(BL)], axis=0)

    first_sub = lax.broadcasted_iota(jnp.int32, (NB, LANES), 0) == 0
    xcs = []
    for c in range(n_slab):
        lanes = slice(c * LANES, (c + 1) * LANES)
        x = interleaved(c)
        head = []
        for k in range(taps, 0, -1):
            tail = x[(BL - k) * NB:(BL - k + 1) * NB, :]
            prev = cx_ref[c * taps + k - 1]
            head.append(jnp.where(first_sub, pltpu.roll(prev, 1, 0), pltpu.roll(tail, 1, 0)))
            cx_ref[c * taps + k - 1] = tail
        ext = jnp.concatenate(head + [x], axis=0)
        xc = cb_ref[:, lanes] + jnp.zeros((rows, LANES), F32)
        for j in range(CONV_WIDTH):
            xc = xc + cw_ref[j:j + 1, lanes] * ext[j * NB:j * NB + rows, :]
        xcs.append(xc)
    xc = jnp.concatenate(xcs, axis=1)

    gates = _dot(xc.astype(BF16), wg_ref[...]) + bg_ref[...]
    r = _sigmoid(gates[:, :Wd])
    ig = _sigmoid(gates[:, Wd:])
    a = jnp.exp(r * ((-RG_C) * _softplus(-lam_ref[...])))
    a_ref[...] = a
    d = 1.0 - a * a
    u_ref[...] = jnp.where(d > 0.0, d * lax.rsqrt(d), 0.0) * (ig * xc)

    for c in range(n_slab):
        lanes = slice(c * LANES, (c + 1) * LANES)
        h = jnp.zeros((NB, LANES), F32)
        p = jnp.ones((NB, LANES), F32)
        for i in range(BL):
            ai = a_ref[i * NB:(i + 1) * NB, lanes]
            h = ai * h + u_ref[i * NB:(i + 1) * NB, lanes]
            p = ai * p
            u_ref[i * NB:(i + 1) * NB, lanes] = h
            p_ref[i * NB:(i + 1) * NB, lanes] = p
    h_end = u_ref[(BL - 1) * NB:BL * NB, :]
    p_end = p_ref[(BL - 1) * NB:BL * NB, :]
    h_in = [hc_ref[0:1, :]]
    for s in range(NB):
        h_in.append(h_end[s:s + 1, :] + p_end[s:s + 1, :] * h_in[s])
    hc_ref[...] = jnp.broadcast_to(h_in[NB], hc_ref.shape)
    h_enter = jnp.concatenate(h_in[:NB], axis=0)
    h_all = u_ref[...] + p_ref[...] * jnp.concatenate([h_enter] * BL, axis=0)

    gate = jnp.concatenate([interleaved(n_slab + c) for c in range(n_slab)], axis=1)
    yv = h_all * _gelu_tanh(gate)
    yn = _rms_rows(yv) * gain_ref[...]
    for c in range(n_slab):
        for i in range(BL):
            ost_ref[c, pl.ds(i, NB, stride=PITCH), :] = yn[i * NB:(i + 1) * NB, c * LANES:(c + 1) * LANES]
    for s in range(NB):
        y_ref[s * BL:(s + 1) * BL, :] = jnp.concatenate(
            [ost_ref[c, s * PITCH:s * PITCH + BL, :] for c in range(n_slab)], axis=1).astype(BF16)


def _rglru(zr, conv_w, conv_b, w_gates, b_gates, lam, gain, B, S, rows=1024):
    T = B * S
    Wd = zr.shape[1] // 2
    rows = min(rows, S)
    steps = S // rows
    pitch_rows = SUBLANES * (rows // SUBLANES + SUBLANES)
    n_slab = Wd // LANES
    kern = functools.partial(_rglru_kernel, rows=rows)
    const = lambda b, s: (0, 0)
    return pl.pallas_call(
        kern,
        grid=(B, steps),
        in_specs=[
            pl.BlockSpec((rows, 2 * Wd), lambda b, s: (b * steps + s, 0)),
            pl.BlockSpec((CONV_WIDTH, Wd), const),
            pl.BlockSpec((1, Wd), const),
            pl.BlockSpec((Wd, 2 * Wd), const),
            pl.BlockSpec((1, 2 * Wd), const),
            pl.BlockSpec((1, Wd), const),
            pl.BlockSpec((1, Wd), const),
        ],
        out_specs=pl.BlockSpec((rows, Wd), lambda b, s: (b * steps + s, 0)),
        out_shape=jax.ShapeDtypeStruct((T, Wd), BF16),
        scratch_shapes=[
            pltpu.VMEM((2 * n_slab, pitch_rows, LANES), F32),
            pltpu.VMEM((n_slab, pitch_rows, LANES), F32),
            pltpu.VMEM((rows, Wd), F32),
            pltpu.VMEM((rows, Wd), F32),
            pltpu.VMEM((rows, Wd), F32),
            pltpu.VMEM((n_slab * (CONV_WIDTH - 1), SUBLANES, LANES), F32),
            pltpu.VMEM((SUBLANES, Wd), F32),
        ],
        compiler_params=_cparams("parallel", "arbitrary"),
        name="rglru",
    )(zr, conv_w, conv_b, w_gates, b_gates, lam, gain)


def _outproj_kernel(yml_ref, yrg_ref, h_ref, w_ref, g_ref, wr_ref, hn_ref, un_ref, lg_ref):
    half = yml_ref.shape[1]
    hn = (h_ref[...] + _dot(yml_ref[...], w_ref[:half, :]) + _dot(yrg_ref[...], w_ref[half:, :]))
    un = _rms_rows(hn) * g_ref[...]
    wr = wr_ref[...]
    w_hi = wr.astype(BF16)
    w_lo = (wr - w_hi.astype(F32)).astype(BF16)
    u_hi = un.astype(BF16)
    u_lo = (un - u_hi.astype(F32)).astype(BF16)
    both = _dot(u_hi, jnp.concatenate([w_hi, w_lo], axis=1))
    lg_ref[...] = both[:, :LANES] + (both[:, LANES:] + _dot(u_lo, w_hi))
    hn_ref[...] = hn
    un_ref[...] = un


def _outproj(yml, yrg, h, w_out, g_ffn, w_router, tm=1024):
    T, D = h.shape
    half = yml.shape[1]
    tm = min(tm, T)
    row = lambda i: (i, 0)
    const = lambda i: (0, 0)
    return pl.pallas_call(
        _outproj_kernel,
        grid=(T // tm,),
        in_specs=[
            pl.BlockSpec((tm, half), row),
            pl.BlockSpec((tm, half), row),
            pl.BlockSpec((tm, D), row),
            pl.BlockSpec((2 * half, D), const),
            pl.BlockSpec((1, D), const),
            pl.BlockSpec((D, LANES), const),
        ],
        out_specs=[pl.BlockSpec((tm, D), row), pl.BlockSpec((tm, D), row), pl.BlockSpec((tm, LANES), row)],
        out_shape=[jax.ShapeDtypeStruct((T, D), F32), jax.ShapeDtypeStruct((T, D), F32),
                   jax.ShapeDtypeStruct((T, LANES), F32)],
        compiler_params=_cparams("parallel"),
        name="outproj_router",
    )(yml, yrg, h, w_out, g_ffn, w_router)


SUB_ROWS = 1024
SUB_STEP = 256


def _swiglu_tile(x_ref, o_ref, init_fn, valid_fn, w_hbm, e_cur, e_next, first, has_next, wbufs, sems,
                 scratch, tf):
    wg_hbm, wu_hbm, wd_hbm = w_hbm
    gbuf, ubuf, dbuf = wbufs
    act_ref, wgb_ref, wub_ref, wdb_ref = scratch
    tm, D = o_ref.shape
    sub = min(SUB_ROWS, tm)
    nj = wg_hbm.shape[2] // tf

    def up_copies(e, j, slot):
        cols = pl.ds(pl.multiple_of(j * tf, LANES), tf)
        return (pltpu.make_async_copy(wg_hbm.at[e, :, cols], gbuf.at[slot], sems.at[0, slot]),
                pltpu.make_async_copy(wu_hbm.at[e, :, cols], ubuf.at[slot], sems.at[1, slot]))

    def down_copy(e, j, slot):
        rws = pl.ds(pl.multiple_of(j * tf, SUBLANES), tf)
        return pltpu.make_async_copy(wd_hbm.at[e, rws, :], dbuf.at[slot], sems.at[2, slot])

    def start_up(e, j, slot):
        for cp in up_copies(e, j, slot):
            cp.start()

    def wait_up(e, j, slot):
        for cp in up_copies(e, j, slot):
            cp.wait()

    def up(rows, s, slot):
        if s == 0:
            wgb_ref[...] = gbuf[slot].astype(BF16)
            wub_ref[...] = ubuf[slot].astype(BF16)
        x = x_ref[rows, :]
        g = _dot(x, wgb_ref[...])
        u = _dot(x, wub_ref[...])
        act_ref[rows, :] = (g * _sigmoid(g) * u).astype(BF16)

    def down(rows, s, slot):
        if s == 0:
            wdb_ref[...] = dbuf[slot].astype(BF16)
        o_ref[rows, :] += _dot(act_ref[rows, :], wdb_ref[...])

    def sub_tiles(body):
        for s in range(tm // sub):
            valid = valid_fn(s)
            if valid is None:
                body(pl.ds(s * sub, sub), s, sub)
                continue
            for z in list(range(SUB_STEP, sub, SUB_STEP)) + [sub]:
                on = valid > z - SUB_STEP
                if z < sub:
                    on = jnp.logical_and(on, valid <= z)
                pl.when(on)(functools.partial(body, pl.ds(s * sub, z), s, z))

    @pl.when(first)
    def _():
        start_up(e_cur, 0, 0)

    wait_up(e_cur, 0, 0)
    if nj > 1:
        start_up(e_cur, 1, 1)
    down_copy(e_cur, 0, 0).start()

    def first_body(rows, s, z):
        up(rows, s, 0)
        if init_fn is not None:
            o_ref[rows, :] = init_fn(rows)

    sub_tiles(first_body)

    def step(j, carry):
        slot = j % 2
        wait_up(e_cur, j, slot)
        down_copy(e_cur, j - 1, 1 - slot).wait()

        @pl.when(j + 1 < nj)
        def _():
            start_up(e_cur, j + 1, 1 - slot)

        down_copy(e_cur, j, slot).start()

        def body(rows, s, z):
            down(rows, s, 1 - slot)
            up(rows, s, slot)

        sub_tiles(body)
        return carry

    lax.fori_loop(1, nj, step, 0)

    last_slot = (nj - 1) % 2
    down_copy(e_cur, nj - 1, last_slot).wait()

    @pl.when(has_next)
    def _():
        start_up(e_next, 0, 0)

    def last_body(rows, s, z):
        down(rows, s, last_slot)
        if z < sub:
            o_ref[pl.ds(s * sub + z, sub - z), :] = jnp.zeros((sub - z, D), o_ref.dtype)

    sub_tiles(last_body)
    for s in range(tm // sub):
        valid = valid_fn(s)
        if valid is not None:
            @pl.when(valid <= 0)
            def _():
                o_ref[pl.ds(s * sub, sub), :] = jnp.zeros((sub, D), o_ref.dtype)


def _swiglu_scratch(tm, D, tf):
    return [pltpu.VMEM((2, D, tf), F32), pltpu.VMEM((2, D, tf), F32), pltpu.VMEM((2, tf, D), F32),
            pltpu.SemaphoreType.DMA((3, 2)),
            pltpu.VMEM((tm, tf), BF16), pltpu.VMEM((D, tf), BF16), pltpu.VMEM((D, tf), BF16),
            pltpu.VMEM((tf, D), BF16)]


def _ffn_dense_kernel(yml_ref, yrg_ref, h_ref, wo_ref, g_ref, wg_hbm, wu_hbm, wd_hbm, o_ref, un_ref,
                      gbuf, ubuf, dbuf, sems, *scratch, layer, tf):
    i = pl.program_id(0)
    tm = o_ref.shape[0]
    sub = min(SUB_ROWS, tm)
    half = yml_ref.shape[1]

    for s in range(tm // sub):
        rows = pl.ds(s * sub, sub)
        hn = (h_ref[rows, :] + _dot(yml_ref[rows, :], wo_ref[:half, :])
              + _dot(yrg_ref[rows, :], wo_ref[half:, :]))
        o_ref[rows, :] = hn
        un_ref[rows, :] = (_rms_rows(hn) * g_ref[...]).astype(BF16)

    _swiglu_tile(un_ref, o_ref, None, lambda s: None, (wg_hbm, wu_hbm, wd_hbm), layer, layer,
                 i == 0, i + 1 < pl.num_programs(0), (gbuf, ubuf, dbuf), sems, scratch, tf)


def _ffn_dense(yml, yrg, h, w_out, g_ffn, wg, wu, wd, layer, tm=1024, tf=256):
    T, D = h.shape
    half = yml.shape[1]
    F = wg.shape[2]
    tm = min(tm, T)
    tf = min(tf, F)
    row = lambda i: (i, 0)
    const = lambda i: (0, 0)
    hbm = pl.BlockSpec(memory_space=pl.ANY)
    kern = functools.partial(_ffn_dense_kernel, layer=layer, tf=tf)
    return pl.pallas_call(
        kern,
        grid=(T // tm,),
        in_specs=[pl.BlockSpec((tm, half), row), pl.BlockSpec((tm, half), row), pl.BlockSpec((tm, D), row),
                  pl.BlockSpec((2 * half, D), const), pl.BlockSpec((1, D), const), hbm, hbm, hbm],
        out_specs=pl.BlockSpec((tm, D), row),
        out_shape=jax.ShapeDtypeStruct((T, D), F32),
        scratch_shapes=[pltpu.VMEM((tm, D), BF16)] + _swiglu_scratch(tm, D, tf),
        compiler_params=_cparams("arbitrary"),
        name="ffn_dense",
    )(yml, yrg, h, w_out, g_ffn, wg, wu, wd)


def _ffn_grouped_kernel(te_ref, tr_ref, nt_ref, x_ref, wg_hbm, wu_hbm, wd_hbm, o_ref, xb_ref,
                        gbuf, ubuf, dbuf, sems, *scratch, first_expert, tf):
    i = pl.program_id(0)
    sub = min(SUB_ROWS, o_ref.shape[0])
    n_slots = pl.num_programs(0)

    @pl.when(i < nt_ref[0])
    def _():
        xb_ref[...] = x_ref[...].astype(BF16)
        e_next = first_expert + te_ref[jnp.minimum(i + 1, n_slots - 1)]
        _swiglu_tile(xb_ref, o_ref, lambda rows: jnp.zeros((rows.size, o_ref.shape[1]), F32),
                     lambda s: tr_ref[i] - s * sub, (wg_hbm, wu_hbm, wd_hbm), first_expert + te_ref[i],
                     e_next, i == 0, i + 1 < nt_ref[0], (gbuf, ubuf, dbuf), sems, scratch, tf)

    @pl.when(i >= nt_ref[0])
    def _():
        o_ref[...] = jnp.zeros_like(o_ref)


def _ffn_grouped(tile_expert, tile_rows, n_tiles, xs, wg, wu, wd, first_expert, tm, tf=512):
    P, D = xs.shape
    F = wg.shape[2]
    tf = min(tf, F)
    hbm = pl.BlockSpec(memory_space=pl.ANY)
    grid_spec = pltpu.PrefetchScalarGridSpec(
        num_scalar_prefetch=3,
        grid=(P // tm,),
        in_specs=[pl.BlockSpec((tm, D), lambda i, te, tr, nt: (jnp.minimum(i, nt[0] - 1), 0)), hbm, hbm, hbm],
        out_specs=pl.BlockSpec((tm, D), lambda i, te, tr, nt: (i, 0)),
        scratch_shapes=[pltpu.VMEM((tm, D), BF16)] + _swiglu_scratch(tm, D, tf),
    )
    kern = functools.partial(_ffn_grouped_kernel, first_expert=first_expert, tf=tf)
    return pl.pallas_call(
        kern,
        grid_spec=grid_spec,
        out_shape=jax.ShapeDtypeStruct((P, D), F32),
        compiler_params=_cparams("arbitrary"),
        name="ffn_grouped",
    )(tile_expert, tile_rows, n_tiles, xs, wg, wu, wd)


def _router_kernel(lg_ref, rt_ref, rtt_ref, cnt_ref, carry_ref):
    tm = lg_ref.shape[0]

    @pl.when(pl.program_id(0) == 0)
    def _():
        carry_ref[...] = jnp.zeros_like(carry_ref)

    lane = lax.broadcasted_iota(jnp.int32, (tm, LANES), 1).astype(F32)
    lg = jnp.where(lane < N_EXPERTS, lg_ref[...], -jnp.inf)
    v1 = jnp.max(lg, axis=1, keepdims=True)
    e1 = jnp.min(jnp.where(lg == v1, lane, float(LANES)), axis=1, keepdims=True)
    lg2 = jnp.where(lane == e1, -jnp.inf, lg)
    v2 = jnp.max(lg2, axis=1, keepdims=True)
    e2 = jnp.min(jnp.where(lg2 == v2, lane, float(LANES)), axis=1, keepdims=True)
    ex = jnp.exp(v2 - v1)
    w1 = 1.0 / (1.0 + ex)
    w2 = ex / (1.0 + ex)

    oh1 = (lane == e1).astype(F32)
    oh2 = (lane == e2).astype(F32)
    r_i = lax.broadcasted_iota(jnp.int32, (tm, tm), 0)
    c_i = lax.broadcasted_iota(jnp.int32, (tm, tm), 1)
    strict = (c_i < r_i).astype(BF16)
    before = _dot(strict, (oh1 + oh2).astype(BF16)) + carry_ref[0:1, :]
    rank1 = jnp.sum(before * oh1, axis=1, keepdims=True)
    rank2 = jnp.sum(before * oh2, axis=1, keepdims=True)
    total = carry_ref[0:1, :] + jnp.sum(oh1 + oh2, axis=0, keepdims=True)
    carry_ref[...] = jnp.broadcast_to(total, carry_ref.shape)
    cnt_ref[...] = jnp.broadcast_to(total, cnt_ref.shape)

    out = jnp.where(lane == 0, e1, 0.0)
    out = jnp.where(lane == 1, e2, out)
    out = jnp.where(lane == 2, w1, out)
    out = jnp.where(lane == 3, w2, out)
    out = jnp.where(lane == 4, rank1, out)
    out = jnp.where(lane == 5, rank2, out)
    rt_ref[...] = out
    rtt_ref[...] = jnp.transpose(out)[:SUBLANES, :]


def _router(logits, tm=1024):
    T = logits.shape[0]
    tm = min(tm, T)
    return pl.pallas_call(
        _router_kernel,
        grid=(T // tm,),
        in_specs=[pl.BlockSpec((tm, LANES), lambda i: (i, 0))],
        out_specs=[
            pl.BlockSpec((tm, LANES), lambda i: (i, 0)),
            pl.BlockSpec((SUBLANES, tm), lambda i: (0, i)),
            pl.BlockSpec((SUBLANES, LANES), lambda i: (0, 0)),
        ],
        out_shape=[
            jax.ShapeDtypeStruct((T, LANES), F32),
            jax.ShapeDtypeStruct((SUBLANES, T), F32),
            jax.ShapeDtypeStruct((SUBLANES, LANES), F32),
        ],
        scratch_shapes=[pltpu.VMEM((SUBLANES, LANES), F32)],
        compiler_params=_cparams("arbitrary"),
        name="router",
    )(logits)


def _dispatch_kernel(p1_ref, p2_ref, pad0_ref, padn_ref, nt_ref, u_ref, xs_hbm, zbuf_ref, sem, zsem,
                     *, pad_bits, tail_per_tile):
    tm = u_ref.shape[0]

    @pl.when(pl.program_id(0) == 0)
    def _():
        zbuf_ref[...] = jnp.zeros_like(zbuf_ref)

        def pad_copies(e, b):
            n = padn_ref[e]
            off = pad0_ref[e] + (n & ((1 << b) - 1))
            if (1 << b) >= SUBLANES:
                off = pl.multiple_of(off, SUBLANES)
                return [pltpu.make_async_copy(zbuf_ref.at[pl.ds(0, 1 << b)],
                                              xs_hbm.at[pl.ds(off, 1 << b)], zsem)]
            return [pltpu.make_async_copy(zbuf_ref.at[pl.ds(0, 1)], xs_hbm.at[pl.ds(off + r, 1)], zsem)
                    for r in range(1 << b)]

        for wait in (False, True):
            for e in range(N_EXPERTS):
                for b in range(pad_bits):
                    @pl.when(((padn_ref[e] >> b) & 1) == 1)
                    def _():
                        for cp in pad_copies(e, b):
                            if wait:
                                cp.wait()
                            else:
                                cp.start()

        zrows = zbuf_ref.shape[0]

        def tail_copy(t):
            off = pl.multiple_of(t * zrows, SUBLANES)
            return pltpu.make_async_copy(zbuf_ref, xs_hbm.at[pl.ds(off, zrows)], zsem)

        def tail_start(t, carry):
            tail_copy(t).start()
            return carry

        def tail_wait(t, carry):
            tail_copy(t).wait()
            return carry

        first, last = nt_ref[0] * tail_per_tile, xs_hbm.shape[0] // zrows
        lax.fori_loop(first, last, tail_start, 0)
        lax.fori_loop(first, last, tail_wait, 0)

    def row_copy(r, pos):
        return pltpu.make_async_copy(u_ref.at[pl.ds(r, 1)], xs_hbm.at[pl.ds(pos, 1)], sem)

    def issue(b, carry):
        for rr in range(ISSUE_UNROLL):
            r = b * ISSUE_UNROLL + rr
            row_copy(r, p1_ref[r]).start(priority=0)
            row_copy(r, p2_ref[r]).start(priority=1)
        return carry

    lax.fori_loop(0, tm // ISSUE_UNROLL, issue, 0)
    for _ in range(2):
        pltpu.make_async_copy(u_ref, xs_hbm.at[pl.ds(0, tm)], sem).wait()


def _dispatch(pos1, pos2, pad_start, pad_len, n_tiles, un, P, group_tile, tm=1024):
    T, D = un.shape
    tm = min(tm, T)
    pad_bits = group_tile.bit_length() - 1
    assert group_tile == 1 << pad_bits and pad_bits >= 1
    zrows = group_tile // 2
    kern = functools.partial(_dispatch_kernel, pad_bits=pad_bits, tail_per_tile=group_tile // zrows)
    smem = pl.BlockSpec(memory_space=pltpu.SMEM)
    return pl.pallas_call(
        kern,
        grid=(T // tm,),
        in_specs=[
            pl.BlockSpec((tm,), lambda i: (i,), memory_space=pltpu.SMEM),
            pl.BlockSpec((tm,), lambda i: (i,), memory_space=pltpu.SMEM),
            smem, smem, smem,
            pl.BlockSpec((tm, D), lambda i: (i, 0)),
        ],
        out_specs=pl.BlockSpec(memory_space=pl.ANY),
        out_shape=jax.ShapeDtypeStruct((P, D), un.dtype),
        scratch_shapes=[pltpu.VMEM((zrows, D), un.dtype),
                        pltpu.SemaphoreType.DMA(()), pltpu.SemaphoreType.DMA(())],
        compiler_params=_cparams("arbitrary"),
        name="dispatch",
    )(pos1, pos2, pad_start, pad_len, n_tiles, un)


def _combine_kernel(p1_ref, p2_ref, q1_ref, q2_ref, h_ref, rt_ref, g_ref, y_hbm, o_ref, buf_ref, sem,
                    *, final_norm):
    tm = h_ref.shape[0]
    i = pl.program_id(0)
    n = pl.num_programs(0)
    slot = i % 2

    def issue_tile(i1_ref, i2_ref, s):
        def row_copy(k, r, pos):
            return pltpu.make_async_copy(y_hbm.at[pl.ds(pos, 1)], buf_ref.at[s, k, pl.ds(r, 1)], sem.at[s])

        def issue(b, carry):
            for rr in range(ISSUE_UNROLL):
                r = b * ISSUE_UNROLL + rr
                row_copy(0, r, i1_ref[r]).start(priority=0)
                row_copy(1, r, i2_ref[r]).start(priority=1)
            return carry

        lax.fori_loop(0, tm // ISSUE_UNROLL, issue, 0)

    @pl.when(i == 0)
    def _():
        issue_tile(p1_ref, p2_ref, 0)

    @pl.when(i + 1 < n)
    def _():
        issue_tile(q1_ref, q2_ref, 1 - slot)

    for k in range(2):
        pltpu.make_async_copy(y_hbm.at[pl.ds(0, tm)], buf_ref.at[slot, k], sem.at[slot]).wait()

    rt = rt_ref[...]
    out = h_ref[...] + (rt[:, 2:3] * buf_ref[slot, 0] + rt[:, 3:4] * buf_ref[slot, 1])
    if final_norm:
        out = _rms_rows(out) * g_ref[...]
    o_ref[...] = out


def _combine(pos1, pos2, h, routing, g_final, y, final_norm, tm=1024):
    T, D = h.shape
    tm = min(tm, T)
    n = T // tm
    kern = functools.partial(_combine_kernel, final_norm=final_norm)
    cur = pl.BlockSpec((tm,), lambda i: (i,), memory_space=pltpu.SMEM)
    nxt = pl.BlockSpec((tm,), lambda i: (jnp.minimum(i + 1, n - 1),), memory_space=pltpu.SMEM)
    return pl.pallas_call(
        kern,
        grid=(n,),
        in_specs=[
            cur, cur, nxt, nxt,
            pl.BlockSpec((tm, D), lambda i: (i, 0)),
            pl.BlockSpec((tm, LANES), lambda i: (i, 0)),
            pl.BlockSpec((1, D), lambda i: (0, 0)),
            pl.BlockSpec(memory_space=pl.ANY),
        ],
        out_specs=pl.BlockSpec((tm, D), lambda i: (i, 0)),
        out_shape=jax.ShapeDtypeStruct((T, D), F32),
        scratch_shapes=[pltpu.VMEM((2, 2, tm, D), F32), pltpu.SemaphoreType.DMA((2,))],
        compiler_params=_cparams("arbitrary"),
        name="combine",
    )(pos1, pos2, pos1, pos2, h, routing, g_final, y)


def _final_norm_kernel(h_ref, g_ref, o_ref):
    o_ref[...] = _rms_rows(h_ref[...]) * g_ref[...]


def _final_norm(h, g, tm=1024):
    T, D = h.shape
    tm = min(tm, T)
    return pl.pallas_call(
        _final_norm_kernel,
        grid=(T // tm,),
        in_specs=[pl.BlockSpec((tm, D), lambda i: (i, 0)), pl.BlockSpec((1, D), lambda i: (0, 0))],
        out_specs=pl.BlockSpec((tm, D), lambda i: (i, 0)),
        out_shape=jax.ShapeDtypeStruct((T, D), F32),
        compiler_params=_cparams("parallel"),
        name="final_norm",
    )(h, g)


def _block_diag(w):
    G, n, _ = w.shape
    eye = jnp.eye(G, dtype=w.dtype)
    return (eye[:, None, :, None] * w[:, :, None, :]).reshape(G * n, G * n)


def _moe_layer(h, un, logits, w_gate, w_up, w_down, first_expert, g_final, final_norm, tm):
    T, D = h.shape
    E = N_EXPERTS
    routing, routing_t, counts = _router(logits)
    e1 = routing_t[0].astype(jnp.int32)
    e2 = routing_t[1].astype(jnp.int32)
    rank1 = routing_t[4].astype(jnp.int32)
    rank2 = routing_t[5].astype(jnp.int32)
    cnt = counts[0, :E].astype(jnp.int32)
    tiles_per = (cnt + tm - 1) // tm
    tile_end = jnp.cumsum(tiles_per)
    offs = (tile_end - tiles_per) * tm
    pos1 = offs[e1] + rank1
    pos2 = offs[e2] + rank2
    n_slots = (2 * T) // tm + E
    P = n_slots * tm
    tile_expert = jnp.minimum(
        jnp.sum(jnp.arange(n_slots, dtype=jnp.int32)[:, None] >= tile_end[None, :], axis=1), E - 1
    ).astype(jnp.int32)
    n_tiles = tile_end[E - 1:E].astype(jnp.int32)
    tile_in_group = jnp.arange(n_slots, dtype=jnp.int32) - (tile_end - tiles_per)[tile_expert]
    tile_rows = jnp.clip(cnt[tile_expert] - tile_in_group * tm, 0, tm).astype(jnp.int32)

    xs = _dispatch(pos1, pos2, offs + cnt, tiles_per * tm - cnt, n_tiles, un, P, tm)
    y = _ffn_grouped(tile_expert, tile_rows, n_tiles, xs, w_gate, w_up, w_down, first_expert, tm)
    return _combine(pos1, pos2, h, routing, g_final, y, final_norm)


def _forward(x, norm_mix_g, w_in, ml_b_if, ml_norm_g, rg_conv_w, rg_conv_b, rg_w_a, rg_b_a,
             rg_w_x, rg_b_x, rg_lam, rg_norm_g, w_out, norm_ffn_g, ffn_w_gate, ffn_w_up,
             ffn_w_down, moe_w_router, moe_w_gate, moe_w_up, moe_w_down, norm_final_g,
             moe_tile=2048):
    B, S, D = x.shape
    T = B * S
    depth = w_in.shape[0]
    ml_w = ML_HEADS * ML_HEAD_DIM
    n_q = 4 * ml_w
    n_if = 2 * ML_HEADS
    rg_w = rg_lam.shape[1]
    n_r = 2 * rg_w

    h = x.reshape(T, D)
    g_final = norm_final_g.reshape(1, D)
    moe_gate_all = moe_w_gate.reshape((-1,) + moe_w_gate.shape[2:])
    moe_up_all = moe_w_up.reshape((-1,) + moe_w_up.shape[2:])
    moe_down_all = moe_w_down.reshape((-1,) + moe_w_down.shape[2:])
    for l in range(depth):
        wl = w_in[l]
        w_all = jnp.concatenate([wl[:, :ml_w], wl[:, 2 * ml_w:n_q], wl[:, n_q + n_if:]], axis=1).astype(BF16)
        wt = jnp.concatenate([wl[:, ml_w:2 * ml_w], wl[:, n_q:n_q + n_if]], axis=1).T.astype(BF16)
        zq, kt, zr, zg, zgt = _inproj(h, norm_mix_g[l].reshape(1, D), ml_b_if[l].reshape(n_if, 1),
                                      w_all, wt, 3 * ml_w, n_r)
        yml = _mlstm(zq, kt, zg, zgt, ml_norm_g[l].reshape(1, ml_w), B, S)

        w_gates = jnp.concatenate([_block_diag(rg_w_a[l]), _block_diag(rg_w_x[l])], axis=1).astype(BF16)
        b_gates = jnp.concatenate([rg_b_a[l], rg_b_x[l]]).reshape(1, n_r)
        yrg = _rglru(zr, rg_conv_w[l], rg_conv_b[l].reshape(1, rg_w), w_gates, b_gates,
                     rg_lam[l].reshape(1, rg_w), rg_norm_g[l].reshape(1, rg_w), B, S)

        j = l // 2
        is_moe = (l % 2 == 1)
        wo = w_out[l].astype(BF16)
        g_ffn = norm_ffn_g[l].reshape(1, D)
        last = (l == depth - 1)
        if is_moe:
            w_router = jnp.concatenate(
                [moe_w_router[j], jnp.zeros((D, LANES - N_EXPERTS), F32)], axis=1)
            h, un, logits = _outproj(yml, yrg, h, wo, g_ffn, w_router)
            h = _moe_layer(h, un, logits, moe_gate_all, moe_up_all, moe_down_all, j * N_EXPERTS,
                           g_final, last, moe_tile)
        else:
            h = _ffn_dense(yml, yrg, h, wo, g_ffn, ffn_w_gate, ffn_w_up, ffn_w_down, j)
            if last:
                h = _final_norm(h, g_final)
    return h.reshape(B, S, D)


def kernel(x, norm_mix_g, w_in, ml_b_if, ml_norm_g, rg_conv_w, rg_conv_b, rg_w_a, rg_b_a, rg_w_x,
           rg_b_x, rg_lam, rg_norm_g, w_out, norm_ffn_g, ffn_w_gate, ffn_w_up, ffn_w_down,
           moe_w_router, moe_w_gate, moe_w_up, moe_w_down, norm_final_g):
    return _forward(x, norm_mix_g, w_in, ml_b_if, ml_norm_g, rg_conv_w, rg_conv_b, rg_w_a, rg_b_a,
                    rg_w_x, rg_b_x, rg_lam, rg_norm_g, w_out, norm_ffn_g, ffn_w_gate, ffn_w_up,
                    ffn_w_down, moe_w_router, moe_w_gate, moe_w_up, moe_w_down, norm_final_g)
```
